```python
import math
import jax
import jax.numpy as jnp
from jax import lax
import numpy as np

D_MODEL = 1024
BATCH = 32
SEQ = 256
DEPTH = 2
DEC_BATCH = 8
DEC_SEQ = 1024
PAST_LEN = 512

GRID_W = 64
HEAD_DIM = 64
GROUP_W = D_MODEL // 4
D_MIX = 4 * GROUP_W
H_A = GROUP_W // HEAD_DIM
DK_A = HEAD_DIM
DV_A = HEAD_DIM
W_A = H_A * HEAD_DIM
H_B = GROUP_W // HEAD_DIM
DK_B = HEAD_DIM
DV_B = HEAD_DIM
W_B = H_B * HEAD_DIM
H_C = GROUP_W // HEAD_DIM
DK_C = HEAD_DIM
DV_C = HEAD_DIM
W_C = H_C * HEAD_DIM
W_D = GROUP_W
GDN_CONV = 5
HYENA_CONV = 3
HYENA_ORDER = 2
CHUNK = 64
HGRN_CHUNK = 16
POS_BANDS = 16
POS_DIM = 1 + 2 * POS_BANDS
FILTER_HIDDEN = 64
HYENA_TARGET = 1e-2
HYENA_FAST_DECAY = 0.3
HYENA_SLOW_DECAY = 1.5
N_EXPERTS = 64
TOP_K = 8
D_EXPERT = 128
D_SHARED = 128
ROUTED_SCALE = 2.5
EPS = 1e-6
IN_SIZES = (W_A, W_A, W_A, W_A, 2 * H_A, 2 * H_A,
            W_B, W_B, W_B, 2 * W_B,
            W_C, W_C, W_C, W_C, 2 * H_C, 2 * H_C,
            3 * W_D)
N_IN = 4 * W_A + 4 * H_A + 5 * W_B + 4 * W_C + 4 * H_C + 3 * W_D

kernel_name = 'hybrid_flow_trunk_ctx_and_denoise'


def _rmsnorm(x, g):
    xf = x.astype(jnp.float32)
    y = xf * lax.rsqrt(jnp.mean(xf * xf, axis=-1, keepdims=True) + EPS)
    return (y * g.astype(jnp.float32)).astype(x.dtype)


def _head_rmsnorm(o, g):
    return o * lax.rsqrt(jnp.mean(o * o, axis=-1, keepdims=True) + EPS) * g


def _l2norm(x):
    return x * lax.rsqrt(jnp.sum(x * x, axis=-1, keepdims=True) + EPS)


def _split(t, sizes):
    out, start = [], 0
    for s in sizes:
        out.append(t[..., start:start + s])
        start += s
    return out


def _heads(t, n_heads):
    b, l, _ = t.shape
    return t.reshape(b, l, n_heads, -1).transpose(0, 2, 1, 3)


def _merge(t):
    b, h, l, d = t.shape
    return t.transpose(0, 2, 1, 3).reshape(b, l, h * d)


def _flip(t):
    return jnp.flip(t, axis=2)


def _to_colmajor(t, rows):
    b, l, ch = t.shape
    return t.reshape(b, rows, GRID_W, ch).transpose(0, 2, 1, 3).reshape(b, l, ch)


def _from_colmajor(t, rows):
    b, l, ch = t.shape
    return t.reshape(b, GRID_W, rows, ch).transpose(0, 2, 1, 3).reshape(b, l, ch)


def _dwconv(x, w):
    width = w.shape[0]
    return lax.conv_general_dilated(x, w[:, None, :].astype(x.dtype), (1,), [(width // 2, width // 2)],
                                    dimension_numbers=('NWC', 'WIO', 'NWC'),
                                    feature_group_count=x.shape[-1])


def _chunks(t, size):
    b, h, l = t.shape[:3]
    return jnp.moveaxis(t.reshape(b, h, l // size, size, *t.shape[3:]), 2, 0)


def _unchunk(t):
    n, b, h, c = t.shape[:4]
    return jnp.moveaxis(t, 0, 2).reshape(b, h, n * c, *t.shape[4:])


def _adaln(cond, w_mod, b_mod):
    m = jax.nn.silu(cond) @ w_mod + b_mod
    return m.reshape(cond.shape[0], 6, D_MODEL)


def _hgrn_lower_bounds(lb_logits):
    p = jax.nn.softmax(lb_logits.astype(jnp.float32), axis=0)
    return jnp.cumsum(p, axis=0) - p[0:1]


def _gdn_chunked(q, k, v, g, beta, s0):
    c = CHUNK
    q, k, v, g, beta = (_chunks(t, c) for t in (q, k, v, g, beta))
    gc = jnp.cumsum(g, axis=-1)
    incl = jnp.tril(jnp.ones((c, c), dtype=bool))
    strict = jnp.tril(jnp.ones((c, c), dtype=bool), -1)
    decay = jnp.exp(jnp.where(incl, gc[..., :, None] - gc[..., None, :], -jnp.inf))
    kb = k * beta[..., None]
    lmat = jnp.where(strict, jnp.einsum('nbhid,nbhjd->nbhij', kb, k) * decay, 0.0)
    eye = jnp.eye(c, dtype=q.dtype)
    tmat = lax.linalg.triangular_solve(eye + lmat, jnp.broadcast_to(eye, lmat.shape),
                                       left_side=True, lower=True, unit_diagonal=True)
    u = tmat @ (v * beta[..., None])
    w = tmat @ (kb * jnp.exp(gc)[..., None])
    attn = jnp.einsum('nbhid,nbhjd->nbhij', q, k) * decay
    q_dec = q * jnp.exp(gc)[..., None]
    k_dec = k * jnp.exp(gc[..., -1:] - gc)[..., None]
    g_last = jnp.exp(gc[..., -1])

    def step(s, xs):
        u_i, w_i, qd_i, kd_i, a_i, gl_i = xs
        v_new = u_i - w_i @ s
        o = qd_i @ s + a_i @ v_new
        s = s * gl_i[..., None, None] + jnp.einsum('bhcd,bhce->bhde', kd_i, v_new)
        return s, o

    s_fin, o = lax.scan(step, s0, (u, w, q_dec, k_dec, attn, g_last))
    return _unchunk(o), s_fin


def _gdn_mixer(q, k, v, gate, a, b, s0, conv_w, a_log, dt_bias, norm_g):
    bsz, l, _ = q.shape
    qkv = jax.nn.silu(_dwconv(jnp.concatenate([q, k, v], axis=-1), conv_w))
    q, k, v = jnp.split(qkv, 3, axis=-1)
    q = _l2norm(_heads(q, H_A)) * DK_A ** -0.5
    k = _l2norm(_heads(k, H_A))
    v = _heads(v, H_A)
    a = a.reshape(bsz, l, 2, H_A).transpose(2, 0, 3, 1)
    b = b.reshape(bsz, l, 2, H_A).transpose(2, 0, 3, 1)
    g = -jnp.exp(a_log)[:, None, :, None] * jax.nn.softplus(a + dt_bias[:, None, :, None])
    beta = jax.nn.sigmoid(b)
    o_f, s_f = _gdn_chunked(q, k, v, g[0], beta[0], s0[:, 0])
    o_b, s_b = _gdn_chunked(_flip(q), _flip(k), _flip(v), _flip(g[1]), _flip(beta[1]), s0[:, 1])
    o = _merge(_head_rmsnorm(o_f + _flip(o_b), norm_g))
    return o * jax.nn.silu(gate), jnp.stack([s_f, s_b], axis=1)


def _gla_chunked(q, k, v, logf, s0):
    c = HGRN_CHUNK
    q, k, v, logf = (_chunks(t, c) for t in (q, k, v, logf))
    gc = jnp.cumsum(logf, axis=-2)
    incl = jnp.tril(jnp.ones((c, c), dtype=bool))[..., None]
    decay = jnp.exp(jnp.where(incl, gc[..., :, None, :] - gc[..., None, :, :], -jnp.inf))
    attn = jnp.einsum('nbhid,nbhjd,nbhijd->nbhij', q, k, decay)
    q_dec = q * jnp.exp(gc)
    k_dec = k * jnp.exp(gc[..., -1:, :] - gc)
    g_last = jnp.exp(gc[..., -1, :])

    def step(s, xs):
        qd_i, kd_i, v_i, a_i, gl_i = xs
        o = qd_i @ s + a_i @ v_i
        s = s * gl_i[..., None] + jnp.einsum('bhcd,bhce->bhde', kd_i, v_i)
        return s, o

    s_fin, o = lax.scan(step, s0, (q_dec, k_dec, v, attn, g_last))
    return _unchunk(o), s_fin


def _hgrn_mixer(q, i, gate, f, s0, lb, norm_g, rows):
    if rows is not None:
        q, i, f = (_to_colmajor(t, rows) for t in (q, i, f))
    bsz, l, _ = q.shape
    forget = lb + (1.0 - lb) * jax.nn.sigmoid(f.reshape(bsz, l, 2, W_B))
    logf = jnp.log(forget)
    key = 1.0 - forget
    q = _heads(q, H_B)
    i = _heads(i, H_B)
    o_f, s_f = _gla_chunked(q, _heads(key[:, :, 0], H_B), i, _heads(logf[:, :, 0], H_B), s0[:, 0])
    o_b, s_b = _gla_chunked(_flip(q), _flip(_heads(key[:, :, 1], H_B)), _flip(i),
                            _flip(_heads(logf[:, :, 1], H_B)), s0[:, 1])
    o = _merge(_head_rmsnorm(o_f + _flip(o_b), norm_g))
    if rows is not None:
        o = _from_colmajor(o, rows)
    return o * jax.nn.sigmoid(gate), jnp.stack([s_f, s_b], axis=1)


def _mlstm_chunked(q, k, v, ig, logf, c0, n0, m0):
    c = CHUNK
    q, k, v, ig, logf = (_chunks(t, c) for t in (q, k, v, ig, logf))
    bc = jnp.cumsum(logf, axis=-1)
    incl = jnp.tril(jnp.ones((c, c), dtype=bool))
    dmat = jnp.where(incl, bc[..., :, None] - bc[..., None, :] + ig[..., None, :], -jnp.inf)
    dmax = jnp.max(dmat, axis=-1)
    qk = jnp.einsum('nbhid,nbhjd->nbhij', q, k)

    def step(carry, xs):
        cm, nv, m = carry
        q_i, k_i, v_i, b_i, d_i, dmax_i, qk_i = xs
        a = b_i + m[..., None]
        m_t = jnp.maximum(a, dmax_i)
        inter = jnp.exp(a - m_t)
        wmat = jnp.exp(d_i - m_t[..., None]) * qk_i
        num = inter[..., None] * (q_i @ cm) + wmat @ v_i
        den = inter * jnp.einsum('bhcd,bhd->bhc', q_i, nv) + jnp.sum(wmat, axis=-1)
        h = num / jnp.maximum(jnp.abs(den), jnp.exp(-m_t))[..., None]
        m_new = m_t[..., -1]
        carry_w = jnp.exp(a[..., -1] - m_new)
        wk = jnp.exp(d_i[..., -1, :] - m_new[..., None])
        cm = carry_w[..., None, None] * cm + jnp.einsum('bhc,bhcd,bhce->bhde', wk, k_i, v_i)
        nv = carry_w[..., None] * nv + jnp.einsum('bhc,bhcd->bhd', wk, k_i)
        return (cm, nv, m_new), h

    (cf, nf, mf), h = lax.scan(step, (c0, n0, m0), (q, k, v, bc, dmat, dmax, qk))
    return _unchunk(h), cf, nf, mf


def _mlstm_mixer(q, k, v, ogate, ig, fg, c0, n0, m0, f_bias, norm_g):
    bsz, l, _ = q.shape
    q = _heads(q, H_C)
    k = _heads(k, H_C) * DK_C ** -0.5
    v = _heads(v, H_C)
    ig = ig.reshape(bsz, l, 2, H_C).transpose(2, 0, 3, 1)
    logf = jax.nn.log_sigmoid(fg.reshape(bsz, l, 2, H_C).transpose(2, 0, 3, 1) + f_bias[:, None, :, None])
    h_f, cf, nf, mf = _mlstm_chunked(q, k, v, ig[0], logf[0], c0[:, 0], n0[:, 0], m0[:, 0])
    h_b, cb, nb, mb = _mlstm_chunked(_flip(q), _flip(k), _flip(v), _flip(ig[1]), _flip(logf[1]),
                                     c0[:, 1], n0[:, 1], m0[:, 1])
    h = _merge(_head_rmsnorm(h_f + _flip(h_b), norm_g))
    return (h * jax.nn.sigmoid(ogate), jnp.stack([cf, cb], axis=1), jnp.stack([nf, nb], axis=1),
            jnp.stack([mf, mb], axis=1))


def _hyena_filters(length, w1, b1, w2, b2, w3, freq):
    t = jnp.arange(length, dtype=jnp.float32)
    tn = jnp.linspace(0.0, 1.0, length, dtype=jnp.float32)
    bands = jnp.linspace(1e-4, POS_BANDS - 1, POS_BANDS, dtype=jnp.float32)
    ang = (2.0 * math.pi / length) * t[:, None] * bands[None, :]
    feats = jnp.concatenate([tn[:, None], jnp.cos(ang), -jnp.sin(ang)], axis=-1)
    hid = jnp.sin(freq * (feats @ w1 + b1))
    hid = jnp.sin(freq * (hid @ w2 + b2))
    h = (hid @ w3).reshape(length, HYENA_ORDER, 2, W_D)
    deltas = jnp.abs(jnp.linspace(math.log(HYENA_TARGET) / HYENA_SLOW_DECAY,
                                  math.log(HYENA_TARGET) / HYENA_FAST_DECAY, W_D, dtype=jnp.float32))
    h = h * jnp.exp(-tn[:, None] * deltas[None, :])[:, None, None, :]
    zero = jnp.zeros((1, HYENA_ORDER, W_D), jnp.float32)
    circ = jnp.concatenate([h[:, :, 0], zero, h[:0:-1, :, 1]], axis=0)
    circ = circ * lax.rsqrt(jnp.sum(circ * circ, axis=0, keepdims=True) + EPS)
    return jnp.fft.rfft(circ, axis=0)


def _hyena_mixer(u, conv_w, w1, b1, w2, b2, w3, freq, dbias):
    bsz, l, _ = u.shape
    u = _dwconv(u, conv_w)
    v, x1, x2 = jnp.split(u, 3, axis=-1)
    hf = _hyena_filters(l, w1, b1, w2, b2, w3, freq)

    def longconv(z, o):
        zf = jnp.fft.rfft(z, n=2 * l, axis=1)
        return jnp.fft.irfft(zf * hf[None, :, o], n=2 * l, axis=1)[:, :l] + dbias[o] * z

    z = x1 * longconv(v, 0)
    return x2 * longconv(z, 1)


def _moe(x, router_w, router_bias, w_gate, w_up, w_down, sh_gate, sh_up, sh_down):
    bsz, l, d = x.shape
    t = x.reshape(bsz * l, d)
    scores = jax.nn.sigmoid((t @ router_w).astype(jnp.float32))
    _, idx = lax.top_k(scores + router_bias.astype(jnp.float32), TOP_K)
    sel = jnp.take_along_axis(scores, idx, axis=-1)
    wts = sel / jnp.sum(sel, axis=-1, keepdims=True) * ROUTED_SCALE
    gates = jnp.einsum('tk,tke->te', wts, jax.nn.one_hot(idx, N_EXPERTS, dtype=jnp.float32)).astype(x.dtype)
    hid = jax.nn.silu(jnp.einsum('td,edf->tef', t, w_gate)) * jnp.einsum('td,edf->tef', t, w_up)
    routed = jnp.einsum('tef,efd->td', hid * gates[..., None], w_down)
    shared = (jax.nn.silu(t @ sh_gate) * (t @ sh_up)) @ sh_down
    return (routed + shared).reshape(bsz, l, d)


def _mixing(h, states, prm, lb, rows):
    f32 = lambda t: t.astype(jnp.float32)
    s_gdn, s_hgrn, s_c, s_n, s_m = (f32(s) for s in states)
    proj = f32(h @ prm['w_in'])
    qa, ka, va, ga, aa, ba, qb, ib, gb, fb, qc, kc, vc, oc, ic, fc, ud = _split(proj, IN_SIZES)
    ya, st_a = _gdn_mixer(qa, ka, va, ga, aa, ba, s_gdn, f32(prm['gdn_conv']), f32(prm['gdn_a_log']),
                          f32(prm['gdn_dt_bias']), f32(prm['gdn_norm_g']))
    yb, st_b = _hgrn_mixer(qb, ib, gb, fb, s_hgrn, lb, f32(prm['hgrn_norm_g']), rows)
    yc, st_c, st_n, st_m = _mlstm_mixer(qc, kc, vc, oc, ic, fc, s_c, s_n, s_m,
                                        f32(prm['mlstm_f_bias']), f32(prm['mlstm_norm_g']))
    yd = _hyena_mixer(ud, f32(prm['hyena_conv']), f32(prm['filt_w1']), f32(prm['filt_b1']),
                      f32(prm['filt_w2']), f32(prm['filt_b2']), f32(prm['filt_w3']),
                      f32(prm['filt_freq']), f32(prm['hyena_d']))
    y = jnp.concatenate([ya, yb, yc, yd], axis=-1).astype(h.dtype) @ prm['w_out']
    return y, (st_a, st_b, st_c, st_n, st_m)


def _block(x, mod, states, prm, lb, rows):
    shift1, scale1, gate1, shift2, scale2, gate2 = (mod[:, j][:, None, :] for j in range(6))
    h = _rmsnorm(x, prm['norm1_g']) * (1 + scale1) + shift1
    y, new_states = _mixing(h, states, prm, lb, rows)
    x = x + gate1 * y
    h = _rmsnorm(x, prm['norm2_g']) * (1 + scale2) + shift2
    x = x + gate2 * _moe(h, prm['router_w'], prm['router_bias'], prm['exp_w_gate'], prm['exp_w_up'],
                         prm['exp_w_down'], prm['sh_w_gate'], prm['sh_w_up'], prm['sh_w_down'])
    return x, new_states


def setup_inputs(seed: int = 0) -> dict:
    key = jax.random.key(seed)
    k = jax.random.split(key, 40)
    f32 = jnp.float32

    def nrm(i, shape, scale):
        return scale * jax.random.normal(k[i], shape, f32)

    def gain(i, shape):
        return 1.0 + 0.02 * jax.random.normal(k[i], shape, f32)

    dt = jnp.exp(jax.random.uniform(k[16], (DEPTH, 2, H_A), f32, math.log(1e-3), math.log(1e-1)))
    return {
        'x_prompt': nrm(0, (BATCH, SEQ, D_MODEL), 1.0),
        'x_sample': nrm(1, (DEC_BATCH, DEC_SEQ, D_MODEL), 1.0),
        'c': nrm(2, (DEC_BATCH, D_MODEL), 1.0),
        'state_gdn': nrm(3, (DEC_BATCH, DEPTH, 2, H_A, DK_A, DV_A), 0.3),
        'state_hgrn': nrm(4, (DEC_BATCH, DEPTH, 2, H_B, DK_B, DV_B), 0.5),
        'state_mlstm_c': nrm(5, (DEC_BATCH, DEPTH, 2, H_C, DK_C, DV_C), 0.3),
        'state_mlstm_n': nrm(6, (DEC_BATCH, DEPTH, 2, H_C, DK_C), 0.3),
        'state_mlstm_m': nrm(7, (DEC_BATCH, DEPTH, 2, H_C), 1.0),
        'c_ctx': nrm(8, (D_MODEL,), 1.0),
        'norm1_g': gain(9, (DEPTH, D_MODEL)),
        'norm2_g': gain(10, (DEPTH, D_MODEL)),
        'w_mod': nrm(11, (DEPTH, D_MODEL, 6 * D_MODEL), 0.5 * D_MODEL ** -0.5),
        'b_mod': nrm(12, (DEPTH, 6 * D_MODEL), 0.02),
        'w_in': nrm(13, (DEPTH, D_MODEL, N_IN), D_MODEL ** -0.5),
        'gdn_conv': nrm(14, (DEPTH, GDN_CONV, 3 * W_A), GDN_CONV ** -0.5),
        'gdn_a_log': jnp.log(jax.random.uniform(k[15], (DEPTH, 2, H_A), f32, 1.0, 16.0)),
        'gdn_dt_bias': dt + jnp.log(-jnp.expm1(-dt)),
        'gdn_norm_g': gain(17, (DEPTH, DV_A)),
        'hgrn_lb_logits': nrm(18, (DEPTH, 2, W_B), 0.5),
        'hgrn_norm_g': gain(19, (DEPTH, DV_B)),
        'mlstm_f_bias': jax.random.uniform(k[20], (DEPTH, 2, H_C), f32, 3.0, 6.0),
        'mlstm_norm_g': gain(21, (DEPTH, DV_C)),
        'hyena_conv': nrm(22, (DEPTH, HYENA_CONV, 3 * W_D), HYENA_CONV ** -0.5),
        'filt_w1': nrm(23, (DEPTH, POS_DIM, FILTER_HIDDEN), POS_DIM ** -0.5),
        'filt_b1': nrm(24, (DEPTH, FILTER_HIDDEN), 0.1),
        'filt_w2': nrm(25, (DEPTH, FILTER_HIDDEN, FILTER_HIDDEN), FILTER_HIDDEN ** -0.5),
        'filt_b2': nrm(26, (DEPTH, FILTER_HIDDEN), 0.1),
        'filt_w3': nrm(27, (DEPTH, FILTER_HIDDEN, HYENA_ORDER * 2 * W_D), FILTER_HIDDEN ** -0.5),
        'filt_freq': gain(28, (DEPTH, FILTER_HIDDEN)),
        'hyena_d': nrm(29, (DEPTH, HYENA_ORDER, W_D), 0.3),
        'w_out': nrm(30, (DEPTH, D_MIX, D_MODEL), D_MIX ** -0.5),
        'router_w': nrm(31, (DEPTH, D_MODEL, N_EXPERTS), D_MODEL ** -0.5),
        'router_bias': nrm(32, (DEPTH, N_EXPERTS), 0.01),
        'exp_w_gate': nrm(33, (DEPTH, N_EXPERTS, D_MODEL, D_EXPERT), D_MODEL ** -0.5),
        'exp_w_up': nrm(34, (DEPTH, N_EXPERTS, D_MODEL, D_EXPERT), D_MODEL ** -0.5),
        'exp_w_down': nrm(35, (DEPTH, N_EXPERTS, D_EXPERT, D_MODEL), D_EXPERT ** -0.5),
        'sh_w_gate': nrm(36, (DEPTH, D_MODEL, D_SHARED), D_MODEL ** -0.5),
        'sh_w_up': nrm(37, (DEPTH, D_MODEL, D_SHARED), D_MODEL ** -0.5),
        'sh_w_down': nrm(38, (DEPTH, D_SHARED, D_MODEL), D_SHARED ** -0.5),
        'final_g': gain(39, (D_MODEL,)),
    }


def reference(x_prompt, x_sample, c, state_gdn, state_hgrn, state_mlstm_c, state_mlstm_n, state_mlstm_m,
              c_ctx, norm1_g, norm2_g, w_mod, b_mod, w_in, gdn_conv, gdn_a_log, gdn_dt_bias, gdn_norm_g,
              hgrn_lb_logits, hgrn_norm_g, mlstm_f_bias, mlstm_norm_g, hyena_conv, filt_w1, filt_b1,
              filt_w2, filt_b2, filt_w3, filt_freq, hyena_d, w_out, router_w, router_bias, exp_w_gate,
              exp_w_up, exp_w_down, sh_w_gate, sh_w_up, sh_w_down, final_g):
    lower = _hgrn_lower_bounds(hgrn_lb_logits)

    def layer_params(l):
        return {'norm1_g': norm1_g[l], 'norm2_g': norm2_g[l], 'w_in': w_in[l], 'gdn_conv': gdn_conv[l],
                'gdn_a_log': gdn_a_log[l], 'gdn_dt_bias': gdn_dt_bias[l], 'gdn_norm_g': gdn_norm_g[l],
                'hgrn_norm_g': hgrn_norm_g[l], 'mlstm_f_bias': mlstm_f_bias[l], 'mlstm_norm_g': mlstm_norm_g[l],
                'hyena_conv': hyena_conv[l], 'filt_w1': filt_w1[l], 'filt_b1': filt_b1[l],
                'filt_w2': filt_w2[l], 'filt_b2': filt_b2[l], 'filt_w3': filt_w3[l],
                'filt_freq': filt_freq[l], 'hyena_d': hyena_d[l], 'w_out': w_out[l],
                'router_w': router_w[l], 'router_bias': router_bias[l], 'exp_w_gate': exp_w_gate[l],
                'exp_w_up': exp_w_up[l], 'exp_w_down': exp_w_down[l], 'sh_w_gate': sh_w_gate[l],
                'sh_w_up': sh_w_up[l], 'sh_w_down': sh_w_down[l]}

    bsz = x_prompt.shape[0]
    zero_states = (jnp.zeros((bsz, 2, H_A, DK_A, DV_A), jnp.float32),
                   jnp.zeros((bsz, 2, H_B, DK_B, DV_B), jnp.float32),
                   jnp.zeros((bsz, 2, H_C, DK_C, DV_C), jnp.float32),
                   jnp.zeros((bsz, 2, H_C, DK_C), jnp.float32),
                   jnp.zeros((bsz, 2, H_C), jnp.float32))
    new_gdn, new_hgrn, new_c, new_n, new_m = [], [], [], [], []
    h = x_prompt
    for l in range(DEPTH):
        mod = _adaln(c_ctx[None, :], w_mod[l], b_mod[l])
        h, st = _block(h, mod, zero_states, layer_params(l), lower[l], None)
        new_gdn.append(st[0].astype(x_prompt.dtype))
        new_hgrn.append(st[1].astype(x_prompt.dtype))
        new_c.append(st[2].astype(x_prompt.dtype))
        new_n.append(st[3].astype(x_prompt.dtype))
        new_m.append(st[4].astype(x_prompt.dtype))
    y_prompt = _rmsnorm(h, final_g)

    rows = x_sample.shape[1] // GRID_W
    z = x_sample
    for l in range(DEPTH):
        mod = _adaln(c, w_mod[l], b_mod[l])
        cached = (state_gdn[:, l], state_hgrn[:, l], state_mlstm_c[:, l], state_mlstm_n[:, l],
                  state_mlstm_m[:, l])
        z, _ = _block(z, mod, cached, layer_params(l), lower[l], rows)
    y_sample = _rmsnorm(z, final_g)

    return (y_prompt, y_sample, jnp.stack(new_gdn, axis=1), jnp.stack(new_hgrn, axis=1),
            jnp.stack(new_c, axis=1), jnp.stack(new_n, axis=1), jnp.stack(new_m, axis=1))
```

```python
import functools
import math

import numpy as np
import jax
import jax.numpy as jnp
from jax import lax
from jax.experimental import pallas as pl
from jax.experimental.pallas import tpu as pltpu

F32 = jnp.float32
BF16 = jnp.bfloat16

D_MODEL = 1024
N_HEADS = 4
HEAD_DIM = 64
GROUP_W = N_HEADS * HEAD_DIM
GRID_W = 64
CHUNK = 64
HGRN_CHUNK = 16
GDN_CONV = 5
HYENA_CONV = 3
HYENA_ORDER = 2
POS_BANDS = 16
POS_DIM = 1 + 2 * POS_BANDS
FILTER_HIDDEN = 64
N_EXPERTS = 64
TOP_K = 8
D_EXPERT = 128
ROUTED_SCALE = 2.5
EPS = 1e-6
LANES = 128
CONV_PAD = 8
EXPERTS_PER_STEP = 2
N_COND_ROWS = 16
VMEM_LIMIT = 56 * 1024 * 1024

W_SLAB_A = 4 * GROUP_W + LANES
W_SLAB_B = 5 * GROUP_W
W_SLAB_C = 4 * GROUP_W + LANES
W_SLAB_D = 3 * GROUP_W


def _split2(x):
    hi = x.astype(BF16)
    lo = (x - hi.astype(F32)).astype(BF16)
    return hi, lo


def _split3(x):
    p1 = x.astype(BF16)
    r = x - p1.astype(F32)
    p2 = r.astype(BF16)
    p3 = (r - p2.astype(F32)).astype(BF16)
    return p1, p2, p3


def _dot(a, b):
    return jnp.dot(a, b, preferred_element_type=F32)


def _dot_nt(a, b):
    return lax.dot_general(a, b, (((1,), (1,)), ((), ())), preferred_element_type=F32)


def _dot_tn(a, b):
    return lax.dot_general(a, b, (((0,), (0,)), ((), ())), preferred_element_type=F32)


def _dot3(a, b):
    ah, al = _split2(a)
    bh, bl = _split2(b)
    return _dot(ah, bh) + _dot(al, bh) + _dot(ah, bl)


def _dot_sel(x, sel):
    p1, p2, p3 = _split3(x)
    return _dot(p1, sel) + _dot(p2, sel) + _dot(p3, sel)


def _sel_dot(sel, x):
    p1, p2, p3 = _split3(x)
    return _dot(sel, p1) + _dot(sel, p2) + _dot(sel, p3)


def _iota(shape, dim):
    return lax.broadcasted_iota(jnp.int32, shape, dim)


def _head_ones():
    r = _iota((GROUP_W, GROUP_W), 0) // HEAD_DIM
    c = _iota((GROUP_W, GROUP_W), 1) // HEAD_DIM
    return r == c


def _seg_sum(x, ones_bd):
    hi, lo = _split2(x)
    return _dot(hi, ones_bd) + _dot(lo, ones_bd)


def _blockdiag(x, bd_mask):
    return jnp.where(bd_mask, jnp.concatenate([x] * N_HEADS, axis=0), jnp.zeros((), x.dtype))


def _sigmoid(x):
    return 1.0 / (1.0 + jnp.exp(-x))


def _silu(x):
    return x * _sigmoid(x)


def _softplus(x):
    return jnp.maximum(x, 0.0) + jnp.log(1.0 + jnp.exp(-jnp.abs(x)))


def _seg_cumsum(x, chunk, reverse):
    n = x.shape[0]
    pos = _iota(x.shape, 0) % chunk
    s = 1
    while s < chunk:
        if reverse:
            x = x + jnp.where(pos < chunk - s, pltpu.roll(x, n - s, axis=0), 0.0)
        else:
            x = x + jnp.where(pos >= s, pltpu.roll(x, s, axis=0), 0.0)
        s *= 2
    return x


def _short_conv(pad_ref, x, w_ref, width, n):
    ch = x.shape[1]
    pad_ref[0:CONV_PAD, :] = jnp.zeros((CONV_PAD, ch), F32)
    pad_ref[CONV_PAD + n:2 * CONV_PAD + n, :] = jnp.zeros((CONV_PAD, ch), F32)
    pad_ref[CONV_PAD:CONV_PAD + n, :] = x
    acc = None
    for j in range(width):
        off = CONV_PAD - width // 2 + j
        term = pad_ref[off:off + n, :] * w_ref[j:j + 1, :]
        acc = term if acc is None else acc + term
    return acc


def _chunk_masks(reverse):
    r = _iota((CHUNK, GROUP_W), 0)
    j = _iota((CHUNK, GROUP_W), 1) % HEAD_DIM
    if reverse:
        return j >= r, j > r, j == r
    return j <= r, j < r, j == r


def _adaln_kernel(c_ref, w_ref, b_ref, o_ref):
    c = c_ref[...]
    o_ref[...] = _dot3(_silu(c), w_ref[...]) + b_ref[...]


def _adaln(cond, w_mod, b_mod):
    n_out = w_mod.shape[1]
    tn = 1536
    return pl.pallas_call(
        _adaln_kernel,
        grid=(n_out // tn,),
        in_specs=[pl.BlockSpec((N_COND_ROWS, D_MODEL), lambda j: (0, 0)),
                  pl.BlockSpec((D_MODEL, tn), lambda j: (0, j)),
                  pl.BlockSpec((1, tn), lambda j: (0, j))],
        out_specs=pl.BlockSpec((N_COND_ROWS, tn), lambda j: (0, j)),
        out_shape=jax.ShapeDtypeStruct((N_COND_ROWS, n_out), F32),
        compiler_params=pltpu.CompilerParams(dimension_semantics=("arbitrary",), vmem_limit_bytes=VMEM_LIMIT),
        name="adaln",
    )(cond, w_mod, b_mod.reshape(1, n_out))


def _modulated_norm(x, g, shift, scale):
    ms = jnp.mean(x * x, axis=-1, keepdims=True)
    return (x * lax.rsqrt(ms + EPS) * g) * (1.0 + scale) + shift


def _inproj_kernel(x_ref, mod_ref, g_ref, wa_ref, wb_ref, wc_ref, wd_ref, oa_ref, ob_ref, oc_ref, od_ref):
    h = _modulated_norm(x_ref[...], g_ref[...], mod_ref[0, 0:1, :], mod_ref[0, 1:2, :]).astype(BF16)
    oa_ref[...] = _dot(h, wa_ref[...])
    ob_ref[...] = _dot(h, wb_ref[...])
    oc_ref[...] = _dot(h, wc_ref[...])
    od_ref[...] = _dot(h, wd_ref[...])


def _inproj(x, mod, g, w_slabs, seq_len, tm):
    t = x.shape[0]
    per_seq = seq_len // tm if mod.shape[0] > 1 else None

    def mod_idx(i):
        return (i // per_seq if per_seq else 0, 0, 0)

    widths = (W_SLAB_A, W_SLAB_B, W_SLAB_C, W_SLAB_D)
    return pl.pallas_call(
        _inproj_kernel,
        grid=(t // tm,),
        in_specs=[pl.BlockSpec((tm, D_MODEL), lambda i: (i, 0)),
                  pl.BlockSpec((1, 6, D_MODEL), mod_idx),
                  pl.BlockSpec((1, D_MODEL), lambda i: (0, 0))]
                 + [pl.BlockSpec((D_MODEL, w), lambda i: (0, 0)) for w in widths],
        out_specs=[pl.BlockSpec((tm, w), lambda i: (i, 0)) for w in widths],
        out_shape=[jax.ShapeDtypeStruct((t, w), F32) for w in widths],
        compiler_params=pltpu.CompilerParams(dimension_semantics=("parallel",), vmem_limit_bytes=VMEM_LIMIT),
        name="inproj",
    )(x, mod, g, *w_slabs)


def _tri_inverse(lm, diag_mask, bd_mask):
    r = _iota((CHUNK, GROUP_W), 0)
    j = _iota((CHUNK, GROUP_W), 1) % HEAD_DIM
    same16 = (r // 16) == (j // 16)
    same32 = (r // 32) == (j // 32)

    def prod(a, b):
        bh, bl = _split2(b)
        ah, al = _split2(a)
        bdh = _blockdiag(bh, bd_mask)
        bdl = _blockdiag(bl, bd_mask)
        return _dot(ah, bdh) + _dot(al, bdh) + _dot(ah, bdl)

    eye = jnp.where(diag_mask, 1.0, 0.0)
    m = jnp.where(same16, -lm, 0.0)
    p = eye + m
    m = prod(m, m)
    for _ in range(2):
        pm = prod(jnp.concatenate([p, m], axis=0), m)
        p = p + pm[0:CHUNK]
        m = pm[CHUNK:2 * CHUNK]
    p = p + prod(p, m)
    off = jnp.where(same32 & jnp.logical_not(same16), lm, 0.0)
    p = p - prod(p, prod(off, p))
    off = jnp.where(jnp.logical_not(same32), lm, 0.0)
    p = p - prod(p, prod(off, p))
    return p


def _gdn_kernel(p_ref, conv_ref, alog_ref, dtb_ref, ng_ref, ea_ref, t4_ref, t4t_ref, s0_ref,
                y_ref, sout_ref,
                q_s, k_s, v_s, gc_s, beta_s, o_s, st_s, pad_s, *, seq_len):
    n_chunks = seq_len // CHUNK
    ones_bd = jnp.where(_head_ones(), 1.0, 0.0).astype(BF16)
    bd_mask = _head_ones()

    qkv = _silu(_short_conv(pad_s, p_ref[0, :, 0:3 * GROUP_W], conv_ref, GDN_CONV, seq_len))
    q = qkv[:, 0:GROUP_W]
    k = qkv[:, GROUP_W:2 * GROUP_W]
    q_s[...] = q * lax.rsqrt(_seg_sum(q * q, ones_bd) + EPS) * (HEAD_DIM ** -0.5)
    k_s[...] = k * lax.rsqrt(_seg_sum(k * k, ones_bd) + EPS)
    v_s[...] = qkv[:, 2 * GROUP_W:3 * GROUP_W]

    ab = _dot_sel(p_ref[0, :, 4 * GROUP_W:4 * GROUP_W + LANES], ea_ref[...])
    for d in range(2):
        a = ab[:, d * GROUP_W:(d + 1) * GROUP_W]
        b = ab[:, (2 + d) * GROUP_W:(3 + d) * GROUP_W]
        g = -jnp.exp(alog_ref[d:d + 1, :]) * _softplus(a + dtb_ref[d:d + 1, :])
        gc_s[d] = _seg_cumsum(g, CHUNK, reverse=(d == 1))
        beta_s[d] = _sigmoid(b)
        st_s[d] = jnp.where(bd_mask, _dot_sel(s0_ref[0, d], t4_ref[...]), 0.0)

    masks = [_chunk_masks(False), _chunk_masks(True)]
    ones_cc = jnp.ones((CHUNK, CHUNK), BF16)

    def chunk_step(d, r0):
        incl, strict, diag = masks[d]
        last = 0 if d == 1 else CHUNK - 1
        rows = pl.ds(r0, CHUNK)
        q_c = q_s[rows, :]
        k_c = k_s[rows, :]
        v_c = v_s[rows, :]
        gc = gc_s[d, rows, :]
        beta = beta_s[d, rows, :]
        gc_row = _sel_dot(ones_cc, jnp.where(diag, gc, 0.0))
        decay = jnp.where(incl, jnp.exp(jnp.minimum(gc - gc_row, 0.0)), 0.0)
        kb = k_c * beta
        k_bd = _blockdiag(k_c.astype(BF16), bd_mask)
        kq = _dot_nt(jnp.concatenate([kb, q_c], axis=0).astype(BF16), k_bd)
        lm = jnp.where(strict, kq[0:CHUNK] * decay, 0.0)
        attn = kq[CHUNK:2 * CHUNK] * decay
        tmat = _tri_inverse(lm, diag, bd_mask).astype(BF16)
        egc = jnp.exp(gc)
        rhs = jnp.concatenate([_blockdiag((v_c * beta).astype(BF16), bd_mask),
                               _blockdiag((kb * egc).astype(BF16), bd_mask)], axis=1)
        uw = _dot(tmat, rhs)
        s_bd = st_s[d]
        wq = jnp.concatenate([uw[:, GROUP_W:2 * GROUP_W], q_c * egc], axis=0).astype(BF16)
        ws_qs = _dot(wq, s_bd.astype(BF16))
        v_new = uw[:, 0:GROUP_W] - ws_qs[0:CHUNK]
        v_new_b = v_new.astype(BF16)
        o_s[d, rows, :] = ws_qs[CHUNK:2 * CHUNK] + _dot(attn.astype(BF16), _blockdiag(v_new_b, bd_mask))
        gc_last = gc[last:last + 1, :]
        k_dec = (k_c * jnp.exp(gc_last - gc)).astype(BF16)
        st_s[d] = s_bd * jnp.exp(gc_last) + jnp.where(bd_mask, _dot_tn(k_dec, v_new_b), 0.0)

    def body(n, carry):
        chunk_step(0, pl.multiple_of(n * CHUNK, CHUNK))
        chunk_step(1, pl.multiple_of((n_chunks - 1 - n) * CHUNK, CHUNK))
        return carry

    lax.fori_loop(0, n_chunks, body, 0)

    o = o_s[0] + o_s[1]
    o = o * lax.rsqrt(_seg_sum(o * o, ones_bd) * (1.0 / HEAD_DIM) + EPS) * ng_ref[...]
    y_ref[0] = (o * _silu(p_ref[0, :, 3 * GROUP_W:4 * GROUP_W])).astype(BF16)
    for d in range(2):
        sout_ref[0, d] = _dot_sel(st_s[d], t4t_ref[...])


def _expand_rows(m):
    return jnp.repeat(m.astype(F32), HEAD_DIM, axis=1)


def _gate_expander(n_kinds):
    sel = np.zeros((LANES, n_kinds * 2 * GROUP_W), np.float32)
    for kind in range(n_kinds):
        for d in range(2):
            for h in range(N_HEADS):
                c0 = (kind * 2 + d) * GROUP_W + h * HEAD_DIM
                sel[kind * 2 * N_HEADS + d * N_HEADS + h, c0:c0 + HEAD_DIM] = 1.0
    return jnp.asarray(sel, BF16)


def _tile4():
    t = np.concatenate([np.eye(HEAD_DIM, dtype=np.float32)] * N_HEADS, axis=1)
    return jnp.asarray(t, BF16), jnp.asarray(t.T, BF16)


def _seq_spec(seq_len, width):
    return pl.BlockSpec((1, seq_len, width), lambda b: (b, 0, 0))


def _full_spec(shape):
    nd = len(shape)
    return pl.BlockSpec(shape, lambda b: (0,) * nd)


def _state_spec():
    return pl.BlockSpec((1, 2, GROUP_W, HEAD_DIM), lambda b: (b, 0, 0, 0))


def _gdn(slab, conv_w, a_log, dt_bias, norm_g, s0, seq_len):
    bsz = slab.shape[0]
    t4, t4t = _tile4()
    return pl.pallas_call(
        functools.partial(_gdn_kernel, seq_len=seq_len),
        grid=(bsz,),
        in_specs=[_seq_spec(seq_len, W_SLAB_A), _full_spec((GDN_CONV, 3 * GROUP_W)), _full_spec((2, GROUP_W)),
                  _full_spec((2, GROUP_W)), _full_spec((1, GROUP_W)), _full_spec((LANES, 4 * GROUP_W)),
                  _full_spec((HEAD_DIM, GROUP_W)), _full_spec((GROUP_W, HEAD_DIM)), _state_spec()],
        out_specs=[_seq_spec(seq_len, GROUP_W), _state_spec()],
        out_shape=[jax.ShapeDtypeStruct((bsz, seq_len, GROUP_W), BF16),
                   jax.ShapeDtypeStruct((bsz, 2, GROUP_W, HEAD_DIM), F32)],
        scratch_shapes=[pltpu.VMEM((seq_len, GROUP_W), F32), pltpu.VMEM((seq_len, GROUP_W), F32),
                        pltpu.VMEM((seq_len, GROUP_W), F32), pltpu.VMEM((2, seq_len, GROUP_W), F32),
                        pltpu.VMEM((2, seq_len, GROUP_W), F32), pltpu.VMEM((2, seq_len, GROUP_W), F32),
                        pltpu.VMEM((2, GROUP_W, GROUP_W), F32),
                        pltpu.VMEM((seq_len + 2 * CONV_PAD, 3 * GROUP_W), F32)],
        compiler_params=pltpu.CompilerParams(dimension_semantics=("parallel",), vmem_limit_bytes=VMEM_LIMIT),
        name="gdn",
    )(slab, conv_w, _expand_rows(a_log), _expand_rows(dt_bias), jnp.tile(norm_g, N_HEADS)[None, :],
      _gate_expander(2), t4, t4t, s0)


def _mlstm_kernel(p_ref, fb_ref, ng_ref, eg_ref, t4_ref, t4t_ref, c0_ref, n0_ref, m0_ref,
                  y_ref, cout_ref, nout_ref, mout_ref,
                  k_s, bc_s, ig_s, h_s, c_s, n_s, m_s, *, seq_len):
    n_chunks = seq_len // CHUNK
    bd_mask = _head_ones()
    ones_bd = jnp.where(bd_mask, 1.0, 0.0).astype(BF16)
    lane_head = _iota((CHUNK, GROUP_W), 1) // HEAD_DIM

    k_s[...] = p_ref[0, :, GROUP_W:2 * GROUP_W] * (HEAD_DIM ** -0.5)
    gates = _dot_sel(p_ref[0, :, 4 * GROUP_W:4 * GROUP_W + LANES], eg_ref[...])
    for d in range(2):
        ig_s[d] = gates[:, d * GROUP_W:(d + 1) * GROUP_W]
        logf = -_softplus(-(gates[:, (2 + d) * GROUP_W:(3 + d) * GROUP_W] + fb_ref[d:d + 1, :]))
        bc_s[d] = _seg_cumsum(logf, CHUNK, reverse=(d == 1))
        c_s[d] = jnp.where(bd_mask, _dot_sel(c0_ref[0, d], t4_ref[...]), 0.0)
        n_s[d] = n0_ref[0, d]
        m_s[d] = m0_ref[0, d]

    masks = [_chunk_masks(False), _chunk_masks(True)]
    ones_cc = jnp.ones((CHUNK, CHUNK), BF16)

    def chunk_step(d, r0):
        incl, _, diag = masks[d]
        last = 0 if d == 1 else CHUNK - 1
        rows = pl.ds(r0, CHUNK)
        q_c = p_ref[0, rows, 0:GROUP_W]
        k_c = k_s[rows, :]
        v_c = p_ref[0, rows, 2 * GROUP_W:3 * GROUP_W]
        bc = bc_s[d, rows, :]
        x = ig_s[d, rows, :] - bc
        x_row = _sel_dot(ones_cc, jnp.where(diag, x, 0.0))
        dm = jnp.where(incl, bc + x_row, -jnp.inf)
        dmax = jnp.zeros((CHUNK, GROUP_W), F32)
        for h in range(N_HEADS):
            in_head = lane_head == h
            mh = jnp.max(jnp.where(in_head, dm, -jnp.inf), axis=1, keepdims=True)
            dmax = jnp.where(in_head, mh, dmax)
        m_prev = m_s[d]
        a = bc + m_prev
        m_t = jnp.maximum(a, dmax)
        inter = jnp.exp(a - m_t)
        q_b = q_c.astype(BF16)
        qk = _dot_nt(q_b, _blockdiag(k_c.astype(BF16), bd_mask))
        w = (jnp.exp(dm - m_t) * qk).astype(BF16)
        c_bd = c_s[d]
        n_row = n_s[d]
        q_cm = _dot(q_b, c_bd.astype(BF16))
        q_n = _seg_sum(q_c * n_row, ones_bd)
        wv = _dot(w, jnp.concatenate([_blockdiag(v_c.astype(BF16), bd_mask), ones_bd], axis=1))
        num = inter * q_cm + wv[:, 0:GROUP_W]
        den = inter * q_n + wv[:, GROUP_W:2 * GROUP_W]
        h_s[d, rows, :] = num / jnp.maximum(jnp.abs(den), jnp.exp(-m_t))
        m_new = m_t[last:last + 1, :]
        carry_w = jnp.exp(a[last:last + 1, :] - m_new)
        wk = jnp.exp(bc[last:last + 1, :] + x - m_new)
        kw = k_c * wk
        c_s[d] = carry_w * c_bd + jnp.where(bd_mask, _dot_tn(kw.astype(BF16), v_c.astype(BF16)), 0.0)
        n_s[d] = carry_w * n_row + jnp.sum(kw, axis=0, keepdims=True)
        m_s[d] = m_new

    def body(n, carry):
        chunk_step(0, pl.multiple_of(n * CHUNK, CHUNK))
        chunk_step(1, pl.multiple_of((n_chunks - 1 - n) * CHUNK, CHUNK))
        return carry

    lax.fori_loop(0, n_chunks, body, 0)

    h = h_s[0] + h_s[1]
    h = h * lax.rsqrt(_seg_sum(h * h, ones_bd) * (1.0 / HEAD_DIM) + EPS) * ng_ref[...]
    y_ref[0] = (h * _sigmoid(p_ref[0, :, 3 * GROUP_W:4 * GROUP_W])).astype(BF16)
    for d in range(2):
        cout_ref[0, d] = _dot_sel(c_s[d], t4t_ref[...])
        nout_ref[0, d] = n_s[d]
        mout_ref[0, d] = m_s[d]


def _row_state_spec():
    return pl.BlockSpec((1, 2, 1, GROUP_W), lambda b: (b, 0, 0, 0))


def _mlstm(slab, f_bias, norm_g, c0, n0, m0, seq_len):
    bsz = slab.shape[0]
    t4, t4t = _tile4()
    return pl.pallas_call(
        functools.partial(_mlstm_kernel, seq_len=seq_len),
        grid=(bsz,),
        in_specs=[_seq_spec(seq_len, W_SLAB_C), _full_spec((2, GROUP_W)), _full_spec((1, GROUP_W)),
                  _full_spec((LANES, 4 * GROUP_W)), _full_spec((HEAD_DIM, GROUP_W)),
                  _full_spec((GROUP_W, HEAD_DIM)), _state_spec(), _row_state_spec(), _row_state_spec()],
        out_specs=[_seq_spec(seq_len, GROUP_W), _state_spec(), _row_state_spec(), _row_state_spec()],
        out_shape=[jax.ShapeDtypeStruct((bsz, seq_len, GROUP_W), BF16),
                   jax.ShapeDtypeStruct((bsz, 2, GROUP_W, HEAD_DIM), F32),
                   jax.ShapeDtypeStruct((bsz, 2, 1, GROUP_W), F32),
                   jax.ShapeDtypeStruct((bsz, 2, 1, GROUP_W), F32)],
        scratch_shapes=[pltpu.VMEM((seq_len, GROUP_W), F32), pltpu.VMEM((2, seq_len, GROUP_W), F32),
                        pltpu.VMEM((2, seq_len, GROUP_W), F32), pltpu.VMEM((2, seq_len, GROUP_W), F32),
                        pltpu.VMEM((2, GROUP_W, GROUP_W), F32), pltpu.VMEM((2, 1, GROUP_W), F32),
                        pltpu.VMEM((2, 1, GROUP_W), F32)],
        compiler_params=pltpu.CompilerParams(dimension_semantics=("parallel",), vmem_limit_bytes=VMEM_LIMIT),
        name="mlstm",
    )(slab, _expand_rows(f_bias), jnp.tile(norm_g, N_HEADS)[None, :], _gate_expander(2), t4, t4t, c0, n0, m0)


def _hgrn_kernel(p_ref, lb_ref, ng_ref, t4_ref, t4t_ref, s0_ref, y_ref, sout_ref,
                 k_s, gc_s, o_s, st_s, *, seq_len):
    c = HGRN_CHUNK
    n_chunks = seq_len // c
    bd_mask = _head_ones()
    ones_bd = jnp.where(bd_mask, 1.0, 0.0).astype(BF16)
    sub = _iota((c, GROUP_W), 0)

    for d in range(2):
        lb = lb_ref[d:d + 1, :]
        forget = lb + (1.0 - lb) * _sigmoid(p_ref[0, :, (3 + d) * GROUP_W:(4 + d) * GROUP_W])
        k_s[d] = 1.0 - forget
        gc_s[d] = _seg_cumsum(jnp.log(forget), c, reverse=(d == 1))
        s_bd = jnp.where(bd_mask, _dot_sel(s0_ref[0, d], t4_ref[...]), 0.0)
        st_s[d] = s_bd.T

    def chunk_step(d, r0):
        last = 0 if d == 1 else c - 1
        rows = pl.ds(r0, c)
        q_c = p_ref[0, rows, 0:GROUP_W]
        v_c = p_ref[0, rows, GROUP_W:2 * GROUP_W]
        k_c = k_s[d, rows, :]
        gc = gc_s[d, rows, :]
        pieces = []
        for i in range(c):
            keep = (sub >= i) if d == 1 else (sub <= i)
            e_i = jnp.where(keep, jnp.exp(jnp.minimum(gc[i:i + 1, :] - gc, 0.0)), 0.0)
            pieces.append(e_i * q_c[i:i + 1, :] * k_c)
        attn = _seg_sum(jnp.concatenate(pieces, axis=0), ones_bd)
        o_intra = jnp.zeros((c, GROUP_W), F32)
        for i in range(c):
            row = jnp.sum(attn[i * c:(i + 1) * c, :] * v_c, axis=0, keepdims=True)
            o_intra = jnp.where(sub == i, row, o_intra)
        st = st_s[d]
        o_s[d, rows, :] = o_intra + _dot_nt((q_c * jnp.exp(gc)).astype(BF16), st.astype(BF16))
        gc_last = gc[last:last + 1, :]
        k_dec = (k_c * jnp.exp(gc_last - gc)).astype(BF16)
        st_s[d] = st * jnp.exp(gc_last) + jnp.where(bd_mask, _dot_tn(v_c.astype(BF16), k_dec), 0.0)

    def body(n, carry):
        chunk_step(0, pl.multiple_of(n * c, c))
        chunk_step(1, pl.multiple_of((n_chunks - 1 - n) * c, c))
        return carry

    lax.fori_loop(0, n_chunks, body, 0)

    o = o_s[0] + o_s[1]
    o = o * lax.rsqrt(_seg_sum(o * o, ones_bd) * (1.0 / HEAD_DIM) + EPS) * ng_ref[...]
    y_ref[0] = (o * _sigmoid(p_ref[0, :, 2 * GROUP_W:3 * GROUP_W])).astype(BF16)
    for d in range(2):
        sout_ref[0, d] = _dot_sel(st_s[d].T, t4t_ref[...])


def _hgrn(slab, lb, norm_g, s0, seq_len):
    bsz = slab.shape[0]
    t4, t4t = _tile4()
    return pl.pallas_call(
        functools.partial(_hgrn_kernel, seq_len=seq_len),
        grid=(bsz,),
        in_specs=[_seq_spec(seq_len, W_SLAB_B), _full_spec((2, GROUP_W)), _full_spec((1, GROUP_W)),
                  _full_spec((HEAD_DIM, GROUP_W)), _full_spec((GROUP_W, HEAD_DIM)), _state_spec()],
        out_specs=[_seq_spec(seq_len, GROUP_W), _state_spec()],
        out_shape=[jax.ShapeDtypeStruct((bsz, seq_len, GROUP_W), BF16),
                   jax.ShapeDtypeStruct((bsz, 2, GROUP_W, HEAD_DIM), F32)],
        scratch_shapes=[pltpu.VMEM((2, seq_len, GROUP_W), F32), pltpu.VMEM((2, seq_len, GROUP_W), F32),
                        pltpu.VMEM((2, seq_len, GROUP_W), F32), pltpu.VMEM((2, GROUP_W, GROUP_W), F32)],
        compiler_params=pltpu.CompilerParams(dimension_semantics=("parallel",), vmem_limit_bytes=VMEM_LIMIT),
        name="hgrn",
    )(slab, lb.astype(F32), jnp.tile(norm_g, N_HEADS)[None, :], t4, t4t, s0)


def _dft_tables(n):
    k = np.arange(n, dtype=np.float64)[:, None]
    s = np.arange(n, dtype=np.float64)[None, :]
    ang = np.pi * k * s / n
    cos_m = np.cos(ang)
    sin_m = np.sin(ang)
    alt = (-1.0) ** np.arange(n)
    sin_m[0, :] = alt
    fwd = np.concatenate([cos_m, sin_m], axis=0)
    wgt = np.full((n,), 2.0 / (2 * n))
    wgt[0] = 1.0 / (2 * n)
    inv = np.concatenate([cos_m.T * wgt[None, :], sin_m.T * wgt[None, :]], axis=1)
    out = []
    for m in (fwd, inv):
        m32 = m.astype(np.float32)
        hi = m32.astype(BF16)
        lo = (m32 - hi.astype(np.float32)).astype(BF16)
        out += [jnp.asarray(hi), jnp.asarray(lo)]
    return out


def _dft_apply(m_hi, m_lo, x):
    xh, xl = _split2(x)
    return _dot(m_hi, xh) + _dot(m_lo, xh) + _dot(m_hi, xl)


def _filter_kernel(feat_ref, env_ref, w1_ref, b1_ref, w2_ref, b2_ref, w3_ref, freq_ref, fh_ref, fl_ref,
                   hr_ref, hs_ref, *, seq_len):
    freq = freq_ref[...]
    hid = jnp.sin(freq * (_dot3(feat_ref[...], w1_ref[...]) + b1_ref[...]))
    hid = jnp.sin(freq * (_dot3(hid, w2_ref[...]) + b2_ref[...]))
    h = _dot3(hid, w3_ref[...])
    env = env_ref[...]
    not_first = _iota((seq_len, GROUP_W), 0) > 0
    first_row = _iota((seq_len, GROUP_W), 0) == 0
    for o in range(HYENA_ORDER):
        pos = h[:, (2 * o) * GROUP_W:(2 * o + 1) * GROUP_W] * env
        neg = jnp.where(not_first, h[:, (2 * o + 1) * GROUP_W:(2 * o + 2) * GROUP_W] * env, 0.0)
        ss = jnp.sum(pos * pos + neg * neg, axis=0, keepdims=True)
        scale = lax.rsqrt(ss + EPS)
        fa = _dft_apply(fh_ref[...], fl_ref[...], (pos + neg) * scale)
        fb = _dft_apply(fh_ref[...], fl_ref[...], (pos - neg) * scale)
        hr_ref[o] = fa[0:seq_len]
        hs_ref[o] = jnp.where(first_row, fa[seq_len:2 * seq_len], fb[seq_len:2 * seq_len])


def _hyena_filters(seq_len, w1, b1, w2, b2, w3, freq, tables):
    t = np.arange(seq_len, dtype=np.float64)
    tn = np.linspace(0.0, 1.0, seq_len)
    bands = np.linspace(1e-4, POS_BANDS - 1, POS_BANDS)
    ang = (2.0 * math.pi / seq_len) * t[:, None] * bands[None, :]
    feats = np.zeros((seq_len, LANES), np.float32)
    feats[:, 0:POS_DIM] = np.concatenate([tn[:, None], np.cos(ang), -np.sin(ang)], axis=-1)
    deltas = np.abs(np.linspace(math.log(1e-2) / 1.5, math.log(1e-2) / 0.3, GROUP_W))
    env = np.exp(-tn[:, None] * deltas[None, :]).astype(np.float32)
    w1p = jnp.zeros((LANES, FILTER_HIDDEN), F32).at[0:POS_DIM].set(w1)
    n_h = HYENA_ORDER * 2 * GROUP_W
    full = lambda shape: pl.BlockSpec(shape, lambda: (0,) * len(shape))
    return pl.pallas_call(
        functools.partial(_filter_kernel, seq_len=seq_len),
        in_specs=[full((seq_len, LANES)), full((seq_len, GROUP_W)), full((LANES, FILTER_HIDDEN)),
                  full((1, FILTER_HIDDEN)), full((FILTER_HIDDEN, FILTER_HIDDEN)), full((1, FILTER_HIDDEN)),
                  full((FILTER_HIDDEN, n_h)), full((1, FILTER_HIDDEN)),
                  full((2 * seq_len, seq_len)), full((2 * seq_len, seq_len))],
        out_specs=[full((HYENA_ORDER, seq_len, GROUP_W)), full((HYENA_ORDER, seq_len, GROUP_W))],
        out_shape=[jax.ShapeDtypeStruct((HYENA_ORDER, seq_len, GROUP_W), F32)] * 2,
        compiler_params=pltpu.CompilerParams(vmem_limit_bytes=VMEM_LIMIT),
        name="hyena_filter",
    )(jnp.asarray(feats), jnp.asarray(env), w1p, b1[None, :], w2, b2[None, :], w3, freq[None, :],
      tables[0], tables[1])


def _hyena_kernel(p_ref, conv_ref, hr_ref, hs_ref, db_ref, fh_ref, fl_ref, ih_ref, il_ref, y_ref, pad_s,
                  *, seq_len):
    u = _short_conv(pad_s, p_ref[0], conv_ref, HYENA_CONV, seq_len)
    first_row = _iota((seq_len, GROUP_W), 0) == 0

    def longconv(z, o):
        zf = _dft_apply(fh_ref[...], fl_ref[...], z)
        zr = zf[0:seq_len]
        zs = zf[seq_len:2 * seq_len]
        hr = hr_ref[o]
        hs = hs_ref[o]
        yr = zr * hr - jnp.where(first_row, 0.0, zs * hs)
        ys = jnp.where(first_row, zs * hs, zr * hs + zs * hr)
        y = _dft_apply(ih_ref[...], il_ref[...], jnp.concatenate([yr, ys], axis=0))
        return y + db_ref[o:o + 1, :] * z

    v = u[:, 0:GROUP_W]
    x1 = u[:, GROUP_W:2 * GROUP_W]
    x2 = u[:, 2 * GROUP_W:3 * GROUP_W]
    y_ref[0] = (x2 * longconv(x1 * longconv(v, 0), 1)).astype(BF16)


def _hyena(slab, conv_w, hr, hs, dbias, tables, seq_len):
    bsz = slab.shape[0]
    once = lambda shape: pl.BlockSpec(shape, lambda b: (0,) * len(shape), pipeline_mode=pl.Buffered(1))
    return pl.pallas_call(
        functools.partial(_hyena_kernel, seq_len=seq_len),
        grid=(bsz,),
        in_specs=[_seq_spec(seq_len, W_SLAB_D), _full_spec((HYENA_CONV, 3 * GROUP_W)),
                  once((HYENA_ORDER, seq_len, GROUP_W)), once((HYENA_ORDER, seq_len, GROUP_W)),
                  _full_spec((HYENA_ORDER, GROUP_W)),
                  once((2 * seq_len, seq_len)), once((2 * seq_len, seq_len)),
                  once((seq_len, 2 * seq_len)), once((seq_len, 2 * seq_len))],
        out_specs=_seq_spec(seq_len, GROUP_W),
        out_shape=jax.ShapeDtypeStruct((bsz, seq_len, GROUP_W), BF16),
        scratch_shapes=[pltpu.VMEM((seq_len + 2 * CONV_PAD, 3 * GROUP_W), F32)],
        compiler_params=pltpu.CompilerParams(dimension_semantics=("parallel",), vmem_limit_bytes=VMEM_LIMIT),
        name="hyena",
    )(slab, conv_w, hr, hs, dbias, *tables)


def _outproj_kernel(x_ref, ya_ref, yb_ref, yc_ref, yd_ref, wo_ref, mod_ref, g_ref, rwh_ref, rwl_ref, rb_ref,
                    x1_ref, h2_ref, gates_ref):
    y = (_dot(ya_ref[...], wo_ref[0:GROUP_W, :]) + _dot(yb_ref[...], wo_ref[GROUP_W:2 * GROUP_W, :])
         + _dot(yc_ref[...], wo_ref[2 * GROUP_W:3 * GROUP_W, :]) + _dot(yd_ref[...], wo_ref[3 * GROUP_W:4 * GROUP_W, :]))
    x1 = x_ref[...] + mod_ref[0, 2:3, :] * y
    x1_ref[...] = x1
    h2 = _modulated_norm(x1, g_ref[...], mod_ref[0, 3:4, :], mod_ref[0, 4:5, :])
    hh, hl = _split2(h2)
    h2_ref[...] = hh
    logits = _dot(hh, rwh_ref[...]) + _dot(hl, rwh_ref[...]) + _dot(hh, rwl_ref[...])
    scores = _sigmoid(logits)
    vals = scores + rb_ref[...]
    lane = _iota(vals.shape, 1)
    picked = jnp.zeros(vals.shape, F32)
    for _ in range(TOP_K):
        best = jnp.max(vals, axis=-1, keepdims=True)
        first = jnp.min(jnp.where(vals == best, lane, LANES), axis=-1, keepdims=True)
        hit = lane == first
        picked = jnp.where(hit, scores, picked)
        vals = jnp.where(hit, -jnp.inf, vals)
    gates_ref[...] = picked / jnp.sum(picked, axis=-1, keepdims=True) * ROUTED_SCALE


def _outproj(x, ys, w_out, mod, g, router_w, router_bias, seq_len, tm):
    t = x.shape[0]
    per_seq = seq_len // tm if mod.shape[0] > 1 else None

    def mod_idx(i):
        return (i // per_seq if per_seq else 0, 0, 0)

    rw = jnp.zeros((D_MODEL, LANES), F32).at[:, 0:N_EXPERTS].set(router_w)
    rwh = rw.astype(BF16)
    rwl = (rw - rwh.astype(F32)).astype(BF16)
    rb = jnp.full((1, LANES), -jnp.inf, F32).at[0, 0:N_EXPERTS].set(router_bias.astype(F32))
    row = lambda w: pl.BlockSpec((tm, w), lambda i: (i, 0))
    fix = lambda shape: pl.BlockSpec(shape, lambda i: (0,) * len(shape))
    return pl.pallas_call(
        _outproj_kernel,
        grid=(t // tm,),
        in_specs=[row(D_MODEL), row(GROUP_W), row(GROUP_W), row(GROUP_W), row(GROUP_W),
                  fix((D_MODEL, D_MODEL)), pl.BlockSpec((1, 6, D_MODEL), mod_idx), fix((1, D_MODEL)),
                  fix((D_MODEL, LANES)), fix((D_MODEL, LANES)), fix((1, LANES))],
        out_specs=[row(D_MODEL), row(D_MODEL), row(LANES)],
        out_shape=[jax.ShapeDtypeStruct((t, D_MODEL), F32), jax.ShapeDtypeStruct((t, D_MODEL), BF16),
                   jax.ShapeDtypeStruct((t, LANES), F32)],
        compiler_params=pltpu.CompilerParams(dimension_semantics=("parallel",), vmem_limit_bytes=VMEM_LIMIT),
        name="outproj",
    )(x, *ys, w_out.astype(BF16), mod, g, rwh, rwl, rb)


def _moe_kernel(h_ref, gates_ref, x1_ref, mod_ref, wg_ref, wu_ref, wd_ref, sg_ref, su_ref, sd_ref, fg_ref,
                o_ref, acc_ref, *, final_norm):
    j = pl.program_id(1)
    h = h_ref[...]
    width = EXPERTS_PER_STEP * D_EXPERT

    @pl.when(j == 0)
    def _():
        sh = _silu(_dot(h, sg_ref[...].astype(BF16))) * _dot(h, su_ref[...].astype(BF16))
        acc_ref[...] = _dot(sh.astype(BF16), sd_ref[...].astype(BF16))

    w_gu = jnp.concatenate([wg_ref[e] for e in range(EXPERTS_PER_STEP)]
                           + [wu_ref[e] for e in range(EXPERTS_PER_STEP)], axis=1).astype(BF16)
    gu = _dot(h, w_gu)
    sel = (_iota((LANES, width), 0) == EXPERTS_PER_STEP * j + _iota((LANES, width), 1) // D_EXPERT)
    sel = jnp.where(sel, 1.0, 0.0).astype(BF16)
    gate_w = _dot_sel(gates_ref[...], sel)
    hid = (_silu(gu[:, 0:width]) * gu[:, width:2 * width]) * gate_w
    w_dn = jnp.concatenate([wd_ref[e] for e in range(EXPERTS_PER_STEP)], axis=0).astype(BF16)
    acc_ref[...] += _dot(hid.astype(BF16), w_dn)

    @pl.when(j == pl.num_programs(1) - 1)
    def _():
        out = x1_ref[...] + mod_ref[0, 5:6, :] * acc_ref[...]
        if final_norm:
            ms = jnp.mean(out * out, axis=-1, keepdims=True)
            out = out * lax.rsqrt(ms + EPS) * fg_ref[...]
        o_ref[...] = out


def _moe(h2, gates, x1, mod, w_gate, w_up, w_down, sh_gate, sh_up, sh_down, final_g, seq_len, tm, final_norm):
    t = h2.shape[0]
    per_seq = max(seq_len // tm, 1) if mod.shape[0] > 1 else None
    seqs_per_tile = max(tm // seq_len, 1)
    assert mod.shape[0] == 1 or seqs_per_tile == 1

    def mod_idx(i, j):
        return (i // per_seq if per_seq else 0, 0, 0)

    row = lambda w: pl.BlockSpec((tm, w), lambda i, j: (i, 0))
    fix = lambda shape: pl.BlockSpec(shape, lambda i, j: (0,) * len(shape))
    eps_ = EXPERTS_PER_STEP
    return pl.pallas_call(
        functools.partial(_moe_kernel, final_norm=final_norm),
        grid=(t // tm, N_EXPERTS // eps_),
        in_specs=[row(D_MODEL), row(LANES), row(D_MODEL), pl.BlockSpec((1, 6, D_MODEL), mod_idx),
                  pl.BlockSpec((eps_, D_MODEL, D_EXPERT), lambda i, j: (j, 0, 0)),
                  pl.BlockSpec((eps_, D_MODEL, D_EXPERT), lambda i, j: (j, 0, 0)),
                  pl.BlockSpec((eps_, D_EXPERT, D_MODEL), lambda i, j: (j, 0, 0)),
                  fix((D_MODEL, D_EXPERT)), fix((D_MODEL, D_EXPERT)), fix((D_EXPERT, D_MODEL)),
                  fix((1, D_MODEL))],
        out_specs=row(D_MODEL),
        out_shape=jax.ShapeDtypeStruct((t, D_MODEL), F32),
        scratch_shapes=[pltpu.VMEM((tm, D_MODEL), F32)],
        compiler_params=pltpu.CompilerParams(dimension_semantics=("parallel", "arbitrary"),
                                             vmem_limit_bytes=VMEM_LIMIT),
        name="moe",
    )(h2, gates, x1, mod, w_gate, w_up, w_down, sh_gate, sh_up, sh_down, final_g)


def _in_weight_slabs(w_in):
    ga = 4 * GROUP_W
    gb = ga + 4 * N_HEADS
    hb = gb + 5 * GROUP_W
    mc = hb + 4 * GROUP_W
    md = mc + 4 * N_HEADS
    pad = jnp.zeros((D_MODEL, LANES - 4 * N_HEADS), w_in.dtype)
    wa = jnp.concatenate([w_in[:, 0:gb], pad], axis=1)
    wb = w_in[:, gb:hb]
    wc = jnp.concatenate([w_in[:, hb:md], pad], axis=1)
    wd = w_in[:, md:]
    return [w.astype(BF16) for w in (wa, wb, wc, wd)]


def _to_colmajor(t, rows):
    b, l, ch = t.shape
    return t.reshape(b, rows, GRID_W, ch).transpose(0, 2, 1, 3).reshape(b, l, ch)


def _from_colmajor(t, rows):
    b, l, ch = t.shape
    return t.reshape(b, GRID_W, rows, ch).transpose(0, 2, 1, 3).reshape(b, l, ch)


def _layer(x, mod, states, prm, lb, seq_len, grid_rows, tables, final_g, final_norm):
    bsz = x.shape[0] // seq_len
    tm = 512
    s_gdn, s_hgrn, s_c, s_n, s_m = states
    pa, pb, pc, pd = _inproj(x, mod, prm['norm1_g'][None, :], _in_weight_slabs(prm['w_in']), seq_len, tm)
    pa = pa.reshape(bsz, seq_len, W_SLAB_A)
    pb = pb.reshape(bsz, seq_len, W_SLAB_B)
    pc = pc.reshape(bsz, seq_len, W_SLAB_C)
    pd = pd.reshape(bsz, seq_len, W_SLAB_D)

    ya, st_a = _gdn(pa, prm['gdn_conv'], prm['gdn_a_log'], prm['gdn_dt_bias'], prm['gdn_norm_g'],
                    s_gdn.reshape(bsz, 2, GROUP_W, HEAD_DIM), seq_len)
    if grid_rows is not None:
        pb = _to_colmajor(pb, grid_rows)
    yb, st_b = _hgrn(pb, lb, prm['hgrn_norm_g'], s_hgrn.reshape(bsz, 2, GROUP_W, HEAD_DIM), seq_len)
    if grid_rows is not None:
        yb = _from_colmajor(yb, grid_rows)
    yc, st_c, st_n, st_m = _mlstm(pc, prm['mlstm_f_bias'], prm['mlstm_norm_g'],
                                  s_c.reshape(bsz, 2, GROUP_W, HEAD_DIM), s_n.reshape(bsz, 2, 1, GROUP_W),
                                  jnp.repeat(s_m, HEAD_DIM, axis=-1).reshape(bsz, 2, 1, GROUP_W), seq_len)
    hr, hs = _hyena_filters(seq_len, prm['filt_w1'], prm['filt_b1'], prm['filt_w2'], prm['filt_b2'],
                            prm['filt_w3'], prm['filt_freq'], tables)
    yd = _hyena(pd, prm['hyena_conv'], hr, hs, prm['hyena_d'], tables, seq_len)

    t = x.shape[0]
    ys = [y.reshape(t, GROUP_W) for y in (ya, yb, yc, yd)]
    x1, h2, gates = _outproj(x, ys, prm['w_out'], mod, prm['norm2_g'][None, :], prm['router_w'],
                             prm['router_bias'], seq_len, tm)
    tm_moe = 1024
    out = _moe(h2, gates, x1, mod, prm['exp_w_gate'], prm['exp_w_up'], prm['exp_w_down'], prm['sh_w_gate'],
               prm['sh_w_up'], prm['sh_w_down'], final_g[None, :], seq_len, tm_moe, final_norm)
    new_states = (st_a.reshape(bsz, 2, N_HEADS, HEAD_DIM, HEAD_DIM),
                  st_b.reshape(bsz, 2, N_HEADS, HEAD_DIM, HEAD_DIM),
                  st_c.reshape(bsz, 2, N_HEADS, HEAD_DIM, HEAD_DIM),
                  st_n.reshape(bsz, 2, N_HEADS, HEAD_DIM),
                  st_m.reshape(bsz, 2, N_HEADS, HEAD_DIM)[..., 0])
    return out, new_states


def kernel(x_prompt, x_sample, c, state_gdn, state_hgrn, state_mlstm_c, state_mlstm_n, state_mlstm_m, c_ctx, norm1_g, norm2_g, w_mod, b_mod, w_in, gdn_conv, gdn_a_log, gdn_dt_bias, gdn_norm_g, hgrn_lb_logits, hgrn_norm_g, mlstm_f_bias, mlstm_norm_g, hyena_conv, filt_w1, filt_b1, filt_w2, filt_b2, filt_w3, filt_freq, hyena_d, w_out, router_w, router_bias, exp_w_gate, exp_w_up, exp_w_down, sh_w_gate, sh_w_up, sh_w_down, final_g):
    depth = w_in.shape[0]
    bsz, seq, _ = x_prompt.shape
    dbsz, dseq, _ = x_sample.shape
    p = jax.nn.softmax(hgrn_lb_logits.astype(F32), axis=0)
    lower = jnp.cumsum(p, axis=0) - p[0:1]

    per_layer = dict(norm1_g=norm1_g, norm2_g=norm2_g, w_in=w_in, gdn_conv=gdn_conv, gdn_a_log=gdn_a_log,
                     gdn_dt_bias=gdn_dt_bias, gdn_norm_g=gdn_norm_g, hgrn_norm_g=hgrn_norm_g,
                     mlstm_f_bias=mlstm_f_bias, mlstm_norm_g=mlstm_norm_g, hyena_conv=hyena_conv,
                     filt_w1=filt_w1, filt_b1=filt_b1, filt_w2=filt_w2, filt_b2=filt_b2, filt_w3=filt_w3,
                     filt_freq=filt_freq, hyena_d=hyena_d, w_out=w_out, router_w=router_w,
                     router_bias=router_bias, exp_w_gate=exp_w_gate, exp_w_up=exp_w_up, exp_w_down=exp_w_down,
                     sh_w_gate=sh_w_gate, sh_w_up=sh_w_up, sh_w_down=sh_w_down)

    cond = jnp.zeros((N_COND_ROWS, D_MODEL), F32).at[0].set(c_ctx).at[1:1 + dbsz].set(c)
    tables_p = _dft_tables(seq)
    tables_s = _dft_tables(dseq)

    zero_states = (jnp.zeros((bsz, 2, N_HEADS, HEAD_DIM, HEAD_DIM), F32),
                   jnp.zeros((bsz, 2, N_HEADS, HEAD_DIM, HEAD_DIM), F32),
                   jnp.zeros((bsz, 2, N_HEADS, HEAD_DIM, HEAD_DIM), F32),
                   jnp.zeros((bsz, 2, N_HEADS, HEAD_DIM), F32),
                   jnp.zeros((bsz, 2, N_HEADS), F32))
    h = x_prompt.reshape(bsz * seq, D_MODEL)
    z = x_sample.reshape(dbsz * dseq, D_MODEL)
    new_states = [[] for _ in range(5)]
    for l in range(depth):
        prm = {name: val[l] for name, val in per_layer.items()}
        mod = _adaln(cond, w_mod[l], b_mod[l]).reshape(N_COND_ROWS, 6, D_MODEL)
        last = l == depth - 1
        h, st = _layer(h, mod[0:1], zero_states, prm, lower[l], seq, None, tables_p, final_g, last)
        for acc, s in zip(new_states, st):
            acc.append(s.astype(x_prompt.dtype))
        cached = (state_gdn[:, l].astype(F32), state_hgrn[:, l].astype(F32), state_mlstm_c[:, l].astype(F32),
                  state_mlstm_n[:, l].astype(F32), state_mlstm_m[:, l].astype(F32))
        z, _ = _layer(z, mod[1:1 + dbsz], cached, prm, lower[l], dseq, dseq // GRID_W, tables_s, final_g, last)

    return (h.reshape(bsz, seq, D_MODEL), z.reshape(dbsz, dseq, D_MODEL),
            *[jnp.stack(acc, axis=1) for acc in new_states])
```

```python
import functools
import math

import numpy as np
import jax
import jax.numpy as jnp
from jax import lax
from jax.experimental import pallas as pl
from jax.experimental.pallas import tpu as pltpu

F32 = jnp.float32
BF16 = jnp.bfloat16

D_MODEL = 1024
N_HEADS = 4
HEAD_DIM = 64
GROUP_W = N_HEADS * HEAD_DIM
GRID_W = 64
CHUNK = 64
HGRN_CHUNK = 16
GDN_CONV = 5
HYENA_CONV = 3
HYENA_ORDER = 2
POS_BANDS = 16
POS_DIM = 1 + 2 * POS_BANDS
FILTER_HIDDEN = 64
N_EXPERTS = 64
TOP_K = 8
D_EXPERT = 128
ROUTED_SCALE = 2.5
EPS = 1e-6
LOG2_E = 1.4426950408889634
LANES = 128
CONV_PAD = 8
EXPERTS_PER_STEP = 2
GATE_PACK_W = 2 * LANES
N_COND_ROWS = 16
VMEM_LIMIT = 56 * 1024 * 1024

W_SLAB_A = 4 * GROUP_W + LANES
W_SLAB_B = 5 * GROUP_W
W_SLAB_C = 4 * GROUP_W + LANES
W_SLAB_D = 3 * GROUP_W


def _split2(x):
    hi = x.astype(BF16)
    lo = (x - hi.astype(F32)).astype(BF16)
    return hi, lo


def _split3(x):
    p1 = x.astype(BF16)
    r = x - p1.astype(F32)
    p2 = r.astype(BF16)
    p3 = (r - p2.astype(F32)).astype(BF16)
    return p1, p2, p3


def _dot(a, b):
    return jnp.dot(a, b, preferred_element_type=F32)


def _dot_nt(a, b):
    return lax.dot_general(a, b, (((1,), (1,)), ((), ())), preferred_element_type=F32)


def _dot_tn(a, b):
    return lax.dot_general(a, b, (((0,), (0,)), ((), ())), preferred_element_type=F32)


def _dot3(a, b):
    ah, al = _split2(a)
    bh, bl = _split2(b)
    return _dot(ah, bh) + _dot(al, bh) + _dot(ah, bl)


def _dot_sel(x, sel):
    p1, p2, p3 = _split3(x)
    return _dot(p1, sel) + _dot(p2, sel) + _dot(p3, sel)


def _sel_dot(sel, x):
    p1, p2, p3 = _split3(x)
    return _dot(sel, p1) + _dot(sel, p2) + _dot(sel, p3)


def _iota(shape, dim):
    return lax.broadcasted_iota(jnp.int32, shape, dim)


def _head_ones():
    r = _iota((GROUP_W, GROUP_W), 0) // HEAD_DIM
    c = _iota((GROUP_W, GROUP_W), 1) // HEAD_DIM
    return r == c


def _seg_sum(x, ones_bd):
    hi, lo = _split2(x)
    return _dot(hi, ones_bd) + _dot(lo, ones_bd)


def _blockdiag(x, bd_mask):
    return jnp.where(bd_mask, jnp.concatenate([x] * N_HEADS, axis=0), jnp.zeros((), x.dtype))


def _sigmoid(x):
    return 1.0 / (1.0 + jnp.exp(-x))


def _silu(x):
    return x * _sigmoid(x)


def _softplus(x):
    return jnp.maximum(x, 0.0) + jnp.log(1.0 + jnp.exp(-jnp.abs(x)))


def _seg_cumsum(x, chunk, reverse):
    n = x.shape[0]
    pos = _iota(x.shape, 0) % chunk
    s = 1
    while s < chunk:
        if reverse:
            x = x + jnp.where(pos < chunk - s, pltpu.roll(x, n - s, axis=0), 0.0)
        else:
            x = x + jnp.where(pos >= s, pltpu.roll(x, s, axis=0), 0.0)
        s *= 2
    return x


def _short_conv(pad_ref, x, w_ref, width, n):
    ch = x.shape[1]
    pad_ref[0:CONV_PAD, :] = jnp.zeros((CONV_PAD, ch), F32)
    pad_ref[CONV_PAD + n:2 * CONV_PAD + n, :] = jnp.zeros((CONV_PAD, ch), F32)
    pad_ref[CONV_PAD:CONV_PAD + n, :] = x
    acc = None
    for j in range(width):
        off = CONV_PAD - width // 2 + j
        term = pad_ref[off:off + n, :] * w_ref[j:j + 1, :]
        acc = term if acc is None else acc + term
    return acc


def _chunk_masks(reverse):
    r = _iota((CHUNK, GROUP_W), 0)
    j = _iota((CHUNK, GROUP_W), 1) % HEAD_DIM
    if reverse:
        return j >= r, j > r, j == r
    return j <= r, j < r, j == r


def _adaln_kernel(c_ref, w_ref, b_ref, o_ref):
    c = c_ref[...]
    o_ref[...] = _dot3(_silu(c), w_ref[...]) + b_ref[...]


def _adaln(cond, w_mod, b_mod):
    n_out = w_mod.shape[1]
    tn = 1536
    return pl.pallas_call(
        _adaln_kernel,
        grid=(n_out // tn,),
        in_specs=[pl.BlockSpec((N_COND_ROWS, D_MODEL), lambda j: (0, 0)),
                  pl.BlockSpec((D_MODEL, tn), lambda j: (0, j)),
                  pl.BlockSpec((1, tn), lambda j: (0, j))],
        out_specs=pl.BlockSpec((N_COND_ROWS, tn), lambda j: (0, j)),
        out_shape=jax.ShapeDtypeStruct((N_COND_ROWS, n_out), F32),
        compiler_params=pltpu.CompilerParams(dimension_semantics=("arbitrary",), vmem_limit_bytes=VMEM_LIMIT),
        name="adaln",
    )(cond, w_mod, b_mod.reshape(1, n_out))


def _modulated_norm(x, g, shift, scale):
    ms = jnp.mean(x * x, axis=-1, keepdims=True)
    return (x * lax.rsqrt(ms + EPS) * g) * (1.0 + scale) + shift


def _inproj_kernel(x_ref, mod_ref, g_ref, wa_ref, wb_ref, wc_ref, wd_ref, oa_ref, ob_ref, oc_ref, od_ref):
    h = _modulated_norm(x_ref[...], g_ref[...], mod_ref[0, 0:1, :], mod_ref[0, 1:2, :]).astype(BF16)
    oa_ref[...] = _dot(h, wa_ref[...])
    ob_ref[...] = _dot(h, wb_ref[...])
    oc_ref[...] = _dot(h, wc_ref[...])
    od_ref[...] = _dot(h, wd_ref[...])


def _inproj(x, mod, g, w_slabs, seq_len, tm):
    t = x.shape[0]
    per_seq = seq_len // tm if mod.shape[0] > 1 else None

    def mod_idx(i):
        return (i // per_seq if per_seq else 0, 0, 0)

    widths = (W_SLAB_A, W_SLAB_B, W_SLAB_C, W_SLAB_D)
    return pl.pallas_call(
        _inproj_kernel,
        grid=(t // tm,),
        in_specs=[pl.BlockSpec((tm, D_MODEL), lambda i: (i, 0)),
                  pl.BlockSpec((1, 6, D_MODEL), mod_idx),
                  pl.BlockSpec((1, D_MODEL), lambda i: (0, 0))]
                 + [pl.BlockSpec((D_MODEL, w), lambda i: (0, 0)) for w in widths],
        out_specs=[pl.BlockSpec((tm, w), lambda i: (i, 0)) for w in widths],
        out_shape=[jax.ShapeDtypeStruct((t, w), F32) for w in widths],
        compiler_params=pltpu.CompilerParams(dimension_semantics=("parallel",), vmem_limit_bytes=VMEM_LIMIT),
        name="inproj",
    )(x, mod, g, *w_slabs)


def _chunk_masks_f32(reverse):
    return tuple(jnp.where(m, 1.0, 0.0) for m in _chunk_masks(reverse))


def _blockdiag_mul(x, bd_factor):
    return jnp.concatenate([x] * N_HEADS, axis=0) * bd_factor


def _tri_inverse(lms, eyes, same16, same32, bd_b):
    def prod(xs, ys):
        staged = []
        for a, b in zip(xs, ys):
            bh, bl = _split2(b)
            ah, al = _split2(a)
            staged.append((jnp.concatenate([ah, al], axis=0), ah,
                           _blockdiag_mul(bh, bd_b), _blockdiag_mul(bl, bd_b)))
        outs = []
        for a_hl, ah, bdh, bdl in staged:
            n = ah.shape[0]
            hl = _dot(a_hl, bdh)
            outs.append(hl[0:n] + hl[n:2 * n] + _dot(ah, bdl))
        return outs

    ms = [-(lm * same16) for lm in lms]
    ps = [eye + m for eye, m in zip(eyes, ms)]
    ms = prod(ms, ms)
    for _ in range(2):
        pms = prod([jnp.concatenate([p, m], axis=0) for p, m in zip(ps, ms)], ms)
        ps = [p + pm[0:CHUNK] for p, pm in zip(ps, pms)]
        ms = [pm[CHUNK:2 * CHUNK] for pm in pms]
    ps = [p + pm for p, pm in zip(ps, prod(ps, ms))]
    for keep in (same32 - same16, 1.0 - same32):
        offs = [lm * keep for lm in lms]
        ps = [p - c for p, c in zip(ps, prod(ps, prod(offs, ps)))]
    return ps


def _gdn_kernel(p_ref, conv_ref, alog_ref, dtb_ref, ng_ref, ea_ref, t4_ref, t4t_ref, s0_ref,
                y_ref, sout_ref,
                q_s, k_s, v_s, gc_s, beta_s, u_s, o_s, w_s, qe_s, attn_s, kdec_s, st_s, pad_s, *, seq_len):
    n_chunks = seq_len // CHUNK
    bd_f = jnp.where(_head_ones(), 1.0, 0.0)
    bd_b = bd_f.astype(BF16)
    ones_bd = bd_b

    qkv = _silu(_short_conv(pad_s, p_ref[0, :, 0:3 * GROUP_W], conv_ref, GDN_CONV, seq_len))
    q = qkv[:, 0:GROUP_W]
    k = qkv[:, GROUP_W:2 * GROUP_W]
    q_s[...] = q * lax.rsqrt(_seg_sum(q * q, ones_bd) + EPS) * (HEAD_DIM ** -0.5)
    k_s[...] = k * lax.rsqrt(_seg_sum(k * k, ones_bd) + EPS)
    v_s[...] = qkv[:, 2 * GROUP_W:3 * GROUP_W]

    ab = _expand_gates(p_ref[0, :, 4 * GROUP_W:4 * GROUP_W + LANES], ea_ref[...])
    for d in range(2):
        a = ab[:, d * GROUP_W:(d + 1) * GROUP_W]
        b = ab[:, (2 + d) * GROUP_W:(3 + d) * GROUP_W]
        g = -jnp.exp(alog_ref[d:d + 1, :]) * _softplus(a + dtb_ref[d:d + 1, :])
        gc_s[d] = _seg_cumsum(g, CHUNK, reverse=(d == 1))
        beta_s[d] = _sigmoid(b)
        st_s[d] = bd_f * _dot_sel(s0_ref[0, d], t4_ref[...])

    masks = [_chunk_masks_f32(False), _chunk_masks_f32(True)]
    r = _iota((CHUNK, GROUP_W), 0)
    j = _iota((CHUNK, GROUP_W), 1) % HEAD_DIM
    same16 = jnp.where((r // 16) == (j // 16), 1.0, 0.0)
    same32 = jnp.where((r // 32) == (j // 32), 1.0, 0.0)
    ones_cc = jnp.ones((CHUNK, CHUNK), BF16)

    def prepare_body(n, carry):
        ds = [d for _ in range(2) for d in range(2)]
        rows = [pl.ds(pl.multiple_of((2 * n + c) * CHUNK, CHUNK), CHUNK) for c in range(2) for _ in range(2)]
        q_c = [q_s[r, :] for r in rows]
        k_c = [k_s[r, :] for r in rows]
        gc = [gc_s[d, r, :] for d, r in zip(ds, rows)]
        beta = [beta_s[d, r, :] for d, r in zip(ds, rows)]
        gc_row = [_sel_dot(ones_cc, g * masks[d][2]) for d, g in zip(ds, gc)]
        decay = [masks[d][0] * jnp.exp(jnp.minimum(g - gr, 0.0)) for d, g, gr in zip(ds, gc, gc_row)]
        kb = [k * b for k, b in zip(k_c, beta)]
        kq = [_dot_nt(jnp.concatenate([kbi, qi], axis=0).astype(BF16), _blockdiag_mul(ki.astype(BF16), bd_b))
              for kbi, qi, ki in zip(kb, q_c, k_c)]
        tmat = _tri_inverse([kqi[0:CHUNK] * (dec * masks[d][1]) for d, kqi, dec in zip(ds, kq, decay)],
                            [masks[d][2] for d in ds], same16, same32, bd_b)
        egc = [jnp.exp(g) for g in gc]
        rhs = [jnp.concatenate([_blockdiag_mul((v_s[r, :] * b).astype(BF16), bd_b),
                                _blockdiag_mul((kbi * e).astype(BF16), bd_b)], axis=1)
               for r, b, kbi, e in zip(rows, beta, kb, egc)]
        uw = [_dot(t.astype(BF16), x) for t, x in zip(tmat, rhs)]
        for i, (d, r) in enumerate(zip(ds, rows)):
            last = 0 if d == 1 else CHUNK - 1
            u_s[d, r, :] = uw[i][:, 0:GROUP_W]
            w_s[d, r, :] = uw[i][:, GROUP_W:2 * GROUP_W].astype(BF16)
            qe_s[d, r, :] = (q_c[i] * egc[i]).astype(BF16)
            attn_s[d, r, :] = (kq[i][CHUNK:2 * CHUNK] * decay[i]).astype(BF16)
            kdec_s[d, r, :] = (k_c[i] * jnp.exp(gc[i][last:last + 1, :] - gc[i])).astype(BF16)
        return carry

    lax.fori_loop(0, n_chunks // 2, prepare_body, 0)

    def scan_body(n, carry):
        r0 = [pl.multiple_of(n * CHUNK, CHUNK), pl.multiple_of((n_chunks - 1 - n) * CHUNK, CHUNK)]
        rows = [pl.ds(r, CHUNK) for r in r0]
        last = [CHUNK - 1, 0]
        s_bd = [st_s[d] for d in range(2)]
        ws_qs = [_dot(jnp.concatenate([w_s[d, rows[d], :], qe_s[d, rows[d], :]], axis=0), s_bd[d].astype(BF16))
                 for d in range(2)]
        v_new = [(u_s[d, rows[d], :] - ws_qs[d][0:CHUNK]).astype(BF16) for d in range(2)]
        intra = [_dot(attn_s[d, rows[d], :], _blockdiag_mul(v_new[d], bd_b)) for d in range(2)]
        outer = [_dot_tn(kdec_s[d, rows[d], :], v_new[d]) for d in range(2)]
        for d in range(2):
            o_s[d, rows[d], :] = ws_qs[d][CHUNK:2 * CHUNK] + intra[d]
            g_last = jnp.exp(gc_s[d, pl.ds(r0[d] + last[d], 1), :])
            st_s[d] = s_bd[d] * g_last + bd_f * outer[d]
        return carry

    lax.fori_loop(0, n_chunks, scan_body, 0)

    o = o_s[0] + o_s[1]
    o = o * lax.rsqrt(_seg_sum(o * o, ones_bd) * (1.0 / HEAD_DIM) + EPS) * ng_ref[...]
    y_ref[0] = (o * _silu(p_ref[0, :, 3 * GROUP_W:4 * GROUP_W])).astype(BF16)
    for d in range(2):
        sout_ref[0, d] = _dot_sel(st_s[d], t4t_ref[...])


def _expand_rows(m):
    return jnp.repeat(m.astype(F32), HEAD_DIM, axis=1)


N_GATE_LANES = 4 * N_HEADS


def _gate_expander():
    sel = np.zeros((LANES, 4 * GROUP_W), np.float32)
    for piece in range(3):
        for kind in range(2):
            for d in range(2):
                for h in range(N_HEADS):
                    c0 = (kind * 2 + d) * GROUP_W + h * HEAD_DIM
                    sel[piece * N_GATE_LANES + kind * 2 * N_HEADS + d * N_HEADS + h, c0:c0 + HEAD_DIM] = 1.0
    return jnp.asarray(sel, BF16)


def _expand_gates(g, sel):
    p1 = g.astype(BF16).astype(F32)
    r1 = g - p1
    p2 = r1.astype(BF16).astype(F32)
    packed = p1 + pltpu.roll(p2, N_GATE_LANES, axis=1) + pltpu.roll(r1 - p2, 2 * N_GATE_LANES, axis=1)
    return _dot(packed.astype(BF16), sel)


def _tile4():
    t = np.concatenate([np.eye(HEAD_DIM, dtype=np.float32)] * N_HEADS, axis=1)
    return jnp.asarray(t, BF16), jnp.asarray(t.T, BF16)


def _seq_spec(seq_len, width):
    return pl.BlockSpec((1, seq_len, width), lambda b: (b, 0, 0))


def _full_spec(shape):
    nd = len(shape)
    return pl.BlockSpec(shape, lambda b: (0,) * nd)


def _state_spec():
    return pl.BlockSpec((1, 2, GROUP_W, HEAD_DIM), lambda b: (b, 0, 0, 0))


def _gdn(slab, conv_w, a_log, dt_bias, norm_g, s0, seq_len):
    bsz = slab.shape[0]
    t4, t4t = _tile4()
    return pl.pallas_call(
        functools.partial(_gdn_kernel, seq_len=seq_len),
        grid=(bsz,),
        in_specs=[_seq_spec(seq_len, W_SLAB_A), _full_spec((GDN_CONV, 3 * GROUP_W)), _full_spec((2, GROUP_W)),
                  _full_spec((2, GROUP_W)), _full_spec((1, GROUP_W)), _full_spec((LANES, 4 * GROUP_W)),
                  _full_spec((HEAD_DIM, GROUP_W)), _full_spec((GROUP_W, HEAD_DIM)), _state_spec()],
        out_specs=[_seq_spec(seq_len, GROUP_W), _state_spec()],
        out_shape=[jax.ShapeDtypeStruct((bsz, seq_len, GROUP_W), BF16),
                   jax.ShapeDtypeStruct((bsz, 2, GROUP_W, HEAD_DIM), F32)],
        scratch_shapes=[pltpu.VMEM((seq_len, GROUP_W), F32)] * 3
                       + [pltpu.VMEM((2, seq_len, GROUP_W), F32)] * 4
                       + [pltpu.VMEM((2, seq_len, GROUP_W), BF16)] * 4
                       + [pltpu.VMEM((2, GROUP_W, GROUP_W), F32),
                          pltpu.VMEM((seq_len + 2 * CONV_PAD, 3 * GROUP_W), F32)],
        compiler_params=pltpu.CompilerParams(dimension_semantics=("parallel",), vmem_limit_bytes=VMEM_LIMIT),
        name="gdn",
    )(slab, conv_w, _expand_rows(a_log), _expand_rows(dt_bias), jnp.tile(norm_g, N_HEADS)[None, :],
      _gate_expander(), t4, t4t, s0)


def _mlstm_kernel(p_ref, fb_ref, ng_ref, eg_ref, t4_ref, t4t_ref, c0_ref, n0_ref, m0_ref,
                  y_ref, cout_ref, nout_ref, mout_ref,
                  k_s, bc_s, ig_s, h_s, c_s, n_s, m_s, *, seq_len):
    n_chunks = seq_len // CHUNK
    bd_mask = _head_ones()
    bd_f = jnp.where(bd_mask, 1.0, 0.0)
    bd_b = bd_f.astype(BF16)
    ones_bd = bd_b
    lane_head = _iota((CHUNK, GROUP_W), 1) // HEAD_DIM

    k_s[...] = p_ref[0, :, GROUP_W:2 * GROUP_W] * (HEAD_DIM ** -0.5)
    gates = _expand_gates(p_ref[0, :, 4 * GROUP_W:4 * GROUP_W + LANES], eg_ref[...])
    for d in range(2):
        ig_s[d] = gates[:, d * GROUP_W:(d + 1) * GROUP_W]
        logf = -_softplus(-(gates[:, (2 + d) * GROUP_W:(3 + d) * GROUP_W] + fb_ref[d:d + 1, :]))
        bc_s[d] = _seg_cumsum(logf, CHUNK, reverse=(d == 1))
        c_s[d] = jnp.where(bd_mask, _dot_sel(c0_ref[0, d], t4_ref[...]), 0.0)
        n_s[d] = n0_ref[0, d]
        m_s[d] = m0_ref[0, d]

    masks = [_chunk_masks(False), _chunk_masks(True)]
    ones_cc = jnp.ones((CHUNK, CHUNK), BF16)

    def seg_max(dm):
        dmax = jnp.zeros((CHUNK, GROUP_W), F32)
        for h in range(N_HEADS):
            in_head = lane_head == h
            mh = jnp.max(jnp.where(in_head, dm, -jnp.inf), axis=1, keepdims=True)
            dmax = jnp.where(in_head, mh, dmax)
        return dmax

    def body(n, carry):
        dirs = range(2)
        last = [CHUNK - 1, 0]
        rows = [pl.ds(pl.multiple_of(n * CHUNK, CHUNK), CHUNK),
                pl.ds(pl.multiple_of((n_chunks - 1 - n) * CHUNK, CHUNK), CHUNK)]
        q_c = [p_ref[0, r, 0:GROUP_W] for r in rows]
        k_c = [k_s[r, :] for r in rows]
        v_b = [p_ref[0, r, 2 * GROUP_W:3 * GROUP_W].astype(BF16) for r in rows]
        bc = [bc_s[d, rows[d], :] for d in dirs]
        x = [ig_s[d, rows[d], :] - bc[d] for d in dirs]
        x_row = [_sel_dot(ones_cc, jnp.where(masks[d][2], x[d], 0.0)) for d in dirs]
        dm = [jnp.where(masks[d][0], bc[d] + x_row[d], -jnp.inf) for d in dirs]
        dmax = [seg_max(dm[d]) for d in dirs]
        a = [bc[d] + m_s[d] for d in dirs]
        m_t = [jnp.maximum(a[d], dmax[d]) for d in dirs]
        inter = [jnp.exp(a[d] - m_t[d]) for d in dirs]
        q_b = [q.astype(BF16) for q in q_c]
        qk = [_dot_nt(q_b[d], _blockdiag_mul(k_c[d].astype(BF16), bd_b)) for d in dirs]
        w = [(jnp.exp(dm[d] - m_t[d]) * qk[d]).astype(BF16) for d in dirs]
        c_bd = [c_s[d] for d in dirs]
        n_row = [n_s[d] for d in dirs]
        q_cm = [_dot(q_b[d], c_bd[d].astype(BF16)) for d in dirs]
        q_n = [_seg_sum(q_c[d] * n_row[d], ones_bd) for d in dirs]
        wv = [_dot(w[d], jnp.concatenate([_blockdiag_mul(v_b[d], bd_b), ones_bd], axis=1)) for d in dirs]
        m_new = [m_t[d][last[d]:last[d] + 1, :] for d in dirs]
        carry_w = [jnp.exp(a[d][last[d]:last[d] + 1, :] - m_new[d]) for d in dirs]
        kw = [k_c[d] * jnp.exp(bc[d][last[d]:last[d] + 1, :] + x[d] - m_new[d]) for d in dirs]
        outer = [_dot_tn(kw[d].astype(BF16), v_b[d]) for d in dirs]
        for d in dirs:
            num = inter[d] * q_cm[d] + wv[d][:, 0:GROUP_W]
            den = inter[d] * q_n[d] + wv[d][:, GROUP_W:2 * GROUP_W]
            h_s[d, rows[d], :] = num / jnp.maximum(jnp.abs(den), jnp.exp(-m_t[d]))
            c_s[d] = carry_w[d] * c_bd[d] + bd_f * outer[d]
            n_s[d] = carry_w[d] * n_row[d] + jnp.sum(kw[d], axis=0, keepdims=True)
            m_s[d] = m_new[d]
        return carry

    lax.fori_loop(0, n_chunks, body, 0)

    h = h_s[0] + h_s[1]
    h = h * lax.rsqrt(_seg_sum(h * h, ones_bd) * (1.0 / HEAD_DIM) + EPS) * ng_ref[...]
    y_ref[0] = (h * _sigmoid(p_ref[0, :, 3 * GROUP_W:4 * GROUP_W])).astype(BF16)
    for d in range(2):
        cout_ref[0, d] = _dot_sel(c_s[d], t4t_ref[...])
        nout_ref[0, d] = n_s[d]
        mout_ref[0, d] = m_s[d]


def _row_state_spec():
    return pl.BlockSpec((1, 2, 1, GROUP_W), lambda b: (b, 0, 0, 0))


def _mlstm(slab, f_bias, norm_g, c0, n0, m0, seq_len):
    bsz = slab.shape[0]
    t4, t4t = _tile4()
    return pl.pallas_call(
        functools.partial(_mlstm_kernel, seq_len=seq_len),
        grid=(bsz,),
        in_specs=[_seq_spec(seq_len, W_SLAB_C), _full_spec((2, GROUP_W)), _full_spec((1, GROUP_W)),
                  _full_spec((LANES, 4 * GROUP_W)), _full_spec((HEAD_DIM, GROUP_W)),
                  _full_spec((GROUP_W, HEAD_DIM)), _state_spec(), _row_state_spec(), _row_state_spec()],
        out_specs=[_seq_spec(seq_len, GROUP_W), _state_spec(), _row_state_spec(), _row_state_spec()],
        out_shape=[jax.ShapeDtypeStruct((bsz, seq_len, GROUP_W), BF16),
                   jax.ShapeDtypeStruct((bsz, 2, GROUP_W, HEAD_DIM), F32),
                   jax.ShapeDtypeStruct((bsz, 2, 1, GROUP_W), F32),
                   jax.ShapeDtypeStruct((bsz, 2, 1, GROUP_W), F32)],
        scratch_shapes=[pltpu.VMEM((seq_len, GROUP_W), F32), pltpu.VMEM((2, seq_len, GROUP_W), F32),
                        pltpu.VMEM((2, seq_len, GROUP_W), F32), pltpu.VMEM((2, seq_len, GROUP_W), F32),
                        pltpu.VMEM((2, GROUP_W, GROUP_W), F32), pltpu.VMEM((2, 1, GROUP_W), F32),
                        pltpu.VMEM((2, 1, GROUP_W), F32)],
        compiler_params=pltpu.CompilerParams(dimension_semantics=("parallel",), vmem_limit_bytes=VMEM_LIMIT),
        name="mlstm",
    )(slab, _expand_rows(f_bias), jnp.tile(norm_g, N_HEADS)[None, :], _gate_expander(), t4, t4t, c0, n0, m0)


def _hgrn_kernel(p_ref, lb_ref, ng_ref, t4_ref, t4t_ref, s0_ref, y_ref, sout_ref,
                 k_s, gc_s, o_s, st_s, *, seq_len):
    c = HGRN_CHUNK
    n_chunks = seq_len // c
    bd_mask = _head_ones()
    bd_f = jnp.where(bd_mask, 1.0, 0.0)
    ones_bd = bd_f.astype(BF16)
    sub = _iota((c, GROUP_W), 0)
    row_sum = jnp.where(_iota((c, c * c), 1) // c == _iota((c, c * c), 0), 1.0, 0.0).astype(BF16)

    for d in range(2):
        lb = lb_ref[d:d + 1, :]
        forget = lb + (1.0 - lb) * _sigmoid(p_ref[0, :, (3 + d) * GROUP_W:(4 + d) * GROUP_W])
        k_s[d] = 1.0 - forget
        gc_s[d] = _seg_cumsum(jnp.log(forget), c, reverse=(d == 1))
        s_bd = jnp.where(bd_mask, _dot_sel(s0_ref[0, d], t4_ref[...]), 0.0)
        st_s[d] = s_bd.T

    def pair_weights(d, q_c, k_c, gc):
        gc2 = gc * LOG2_E
        pieces = []
        for i in range(c):
            keep = (sub >= i) if d == 1 else (sub <= i)
            e_i = jnp.where(keep, jnp.exp2(gc2[i:i + 1, :] - gc2), 0.0)
            pieces.append(e_i * q_c[i:i + 1, :] * k_c)
        return jnp.concatenate(pieces, axis=0).astype(BF16)

    def body(n, carry):
        ds = [0, 1, 0, 1]
        starts = [2 * n, n_chunks - 1 - 2 * n, 2 * n + 1, n_chunks - 2 - 2 * n]
        rows = [pl.ds(pl.multiple_of(s * c, c), c) for s in starts]
        q_c = [p_ref[0, r, 0:GROUP_W] for r in rows]
        v_c = [p_ref[0, r, GROUP_W:2 * GROUP_W] for r in rows]
        k_c = [k_s[d, r, :] for d, r in zip(ds, rows)]
        gc = [gc_s[d, r, :] for d, r in zip(ds, rows)]
        pairs = [pair_weights(d, q, k, g) for d, q, k, g in zip(ds, q_c, k_c, gc)]
        attn = [_dot(p, ones_bd) for p in pairs]
        weighted = [(a * jnp.concatenate([v] * c, axis=0)).astype(BF16) for a, v in zip(attn, v_c)]
        intra = [_dot(row_sum, w) for w in weighted]
        q_dec = [(q * jnp.exp(g)).astype(BF16) for q, g in zip(q_c, gc)]
        gc_last = [g[(0 if d == 1 else c - 1):(1 if d == 1 else c), :] for d, g in zip(ds, gc)]
        outer = [_dot_tn(v.astype(BF16), (k * jnp.exp(gl - g)).astype(BF16))
                 for v, k, gl, g in zip(v_c, k_c, gc_last, gc)]
        for i in range(4):
            d = ds[i]
            st = st_s[d]
            o_s[d, rows[i], :] = intra[i] + _dot_nt(q_dec[i], st.astype(BF16))
            st_s[d] = st * jnp.exp(gc_last[i]) + bd_f * outer[i]
        return carry

    lax.fori_loop(0, n_chunks // 2, body, 0)

    o = o_s[0] + o_s[1]
    o = o * lax.rsqrt(_seg_sum(o * o, ones_bd) * (1.0 / HEAD_DIM) + EPS) * ng_ref[...]
    y_ref[0] = (o * _sigmoid(p_ref[0, :, 2 * GROUP_W:3 * GROUP_W])).astype(BF16)
    for d in range(2):
        sout_ref[0, d] = _dot_sel(st_s[d].T, t4t_ref[...])


def _hgrn(slab, lb, norm_g, s0, seq_len):
    bsz = slab.shape[0]
    t4, t4t = _tile4()
    return pl.pallas_call(
        functools.partial(_hgrn_kernel, seq_len=seq_len),
        grid=(bsz,),
        in_specs=[_seq_spec(seq_len, W_SLAB_B), _full_spec((2, GROUP_W)), _full_spec((1, GROUP_W)),
                  _full_spec((HEAD_DIM, GROUP_W)), _full_spec((GROUP_W, HEAD_DIM)), _state_spec()],
        out_specs=[_seq_spec(seq_len, GROUP_W), _state_spec()],
        out_shape=[jax.ShapeDtypeStruct((bsz, seq_len, GROUP_W), BF16),
                   jax.ShapeDtypeStruct((bsz, 2, GROUP_W, HEAD_DIM), F32)],
        scratch_shapes=[pltpu.VMEM((2, seq_len, GROUP_W), F32), pltpu.VMEM((2, seq_len, GROUP_W), F32),
                        pltpu.VMEM((2, seq_len, GROUP_W), F32), pltpu.VMEM((2, GROUP_W, GROUP_W), F32)],
        compiler_params=pltpu.CompilerParams(dimension_semantics=("parallel",), vmem_limit_bytes=VMEM_LIMIT),
        name="hgrn",
    )(slab, lb.astype(F32), jnp.tile(norm_g, N_HEADS)[None, :], t4, t4t, s0)


def _dft_tables(n):
    k = np.arange(n, dtype=np.float64)[:, None]
    s = np.arange(n, dtype=np.float64)[None, :]
    ang = np.pi * k * s / n
    cos_m = np.cos(ang)
    sin_m = np.sin(ang)
    alt = (-1.0) ** np.arange(n)
    sin_m[0, :] = alt
    fwd = np.concatenate([cos_m, sin_m], axis=0)
    wgt = np.full((n,), 2.0 / (2 * n))
    wgt[0] = 1.0 / (2 * n)
    inv = np.concatenate([cos_m.T * wgt[None, :], sin_m.T * wgt[None, :]], axis=1)
    out = []
    for m in (fwd, inv):
        m32 = m.astype(np.float32)
        hi = m32.astype(BF16)
        lo = (m32 - hi.astype(np.float32)).astype(BF16)
        out += [jnp.asarray(hi), jnp.asarray(lo)]
    return out


def _dft_apply(m_hi, m_lo, x):
    xh, xl = _split2(x)
    return _dot(m_hi, xh) + _dot(m_lo, xh) + _dot(m_hi, xl)


def _filter_kernel(feat_ref, env_ref, w1_ref, b1_ref, w2_ref, b2_ref, w3_ref, freq_ref, fh_ref, fl_ref,
                   hr_ref, hs_ref, *, seq_len):
    freq = freq_ref[...]
    hid = jnp.sin(freq * (_dot3(feat_ref[...], w1_ref[...]) + b1_ref[...]))
    hid = jnp.sin(freq * (_dot3(hid, w2_ref[...]) + b2_ref[...]))
    h = _dot3(hid, w3_ref[...])
    env = env_ref[...]
    not_first = _iota((seq_len, GROUP_W), 0) > 0
    first_row = _iota((seq_len, GROUP_W), 0) == 0
    for o in range(HYENA_ORDER):
        pos = h[:, (2 * o) * GROUP_W:(2 * o + 1) * GROUP_W] * env
        neg = jnp.where(not_first, h[:, (2 * o + 1) * GROUP_W:(2 * o + 2) * GROUP_W] * env, 0.0)
        ss = jnp.sum(pos * pos + neg * neg, axis=0, keepdims=True)
        scale = lax.rsqrt(ss + EPS)
        fa = _dft_apply(fh_ref[...], fl_ref[...], (pos + neg) * scale)
        fb = _dft_apply(fh_ref[...], fl_ref[...], (pos - neg) * scale)
        hr_ref[o] = fa[0:seq_len]
        hs_ref[o] = jnp.where(first_row, fa[seq_len:2 * seq_len], fb[seq_len:2 * seq_len])


def _hyena_filters(seq_len, w1, b1, w2, b2, w3, freq, tables):
    t = np.arange(seq_len, dtype=np.float64)
    tn = np.linspace(0.0, 1.0, seq_len)
    bands = np.linspace(1e-4, POS_BANDS - 1, POS_BANDS)
    ang = (2.0 * math.pi / seq_len) * t[:, None] * bands[None, :]
    feats = np.zeros((seq_len, LANES), np.float32)
    feats[:, 0:POS_DIM] = np.concatenate([tn[:, None], np.cos(ang), -np.sin(ang)], axis=-1)
    deltas = np.abs(np.linspace(math.log(1e-2) / 1.5, math.log(1e-2) / 0.3, GROUP_W))
    env = np.exp(-tn[:, None] * deltas[None, :]).astype(np.float32)
    w1p = jnp.zeros((LANES, FILTER_HIDDEN), F32).at[0:POS_DIM].set(w1)
    n_h = HYENA_ORDER * 2 * GROUP_W
    full = lambda shape: pl.BlockSpec(shape, lambda: (0,) * len(shape))
    return pl.pallas_call(
        functools.partial(_filter_kernel, seq_len=seq_len),
        in_specs=[full((seq_len, LANES)), full((seq_len, GROUP_W)), full((LANES, FILTER_HIDDEN)),
                  full((1, FILTER_HIDDEN)), full((FILTER_HIDDEN, FILTER_HIDDEN)), full((1, FILTER_HIDDEN)),
                  full((FILTER_HIDDEN, n_h)), full((1, FILTER_HIDDEN)),
                  full((2 * seq_len, seq_len)), full((2 * seq_len, seq_len))],
        out_specs=[full((HYENA_ORDER, seq_len, GROUP_W)), full((HYENA_ORDER, seq_len, GROUP_W))],
        out_shape=[jax.ShapeDtypeStruct((HYENA_ORDER, seq_len, GROUP_W), F32)] * 2,
        compiler_params=pltpu.CompilerParams(vmem_limit_bytes=VMEM_LIMIT),
        name="hyena_filter",
    )(jnp.asarray(feats), jnp.asarray(env), w1p, b1[None, :], w2, b2[None, :], w3, freq[None, :],
      tables[0], tables[1])


def _hyena_kernel(p_ref, conv_ref, hr_ref, hs_ref, db_ref, fh_ref, fl_ref, ih_ref, il_ref, y_ref, pad_s,
                  *, seq_len):
    u = _short_conv(pad_s, p_ref[0], conv_ref, HYENA_CONV, seq_len)
    first_row = _iota((seq_len, GROUP_W), 0) == 0

    def longconv(z, o):
        zf = _dft_apply(fh_ref[...], fl_ref[...], z)
        zr = zf[0:seq_len]
        zs = zf[seq_len:2 * seq_len]
        hr = hr_ref[o]
        hs = hs_ref[o]
        yr = zr * hr - jnp.where(first_row, 0.0, zs * hs)
        ys = jnp.where(first_row, zs * hs, zr * hs + zs * hr)
        y = _dft_apply(ih_ref[...], il_ref[...], jnp.concatenate([yr, ys], axis=0))
        return y + db_ref[o:o + 1, :] * z

    v = u[:, 0:GROUP_W]
    x1 = u[:, GROUP_W:2 * GROUP_W]
    x2 = u[:, 2 * GROUP_W:3 * GROUP_W]
    y_ref[0] = (x2 * longconv(x1 * longconv(v, 0), 1)).astype(BF16)


def _hyena(slab, conv_w, hr, hs, dbias, tables, seq_len):
    bsz = slab.shape[0]
    once = lambda shape: pl.BlockSpec(shape, lambda b: (0,) * len(shape), pipeline_mode=pl.Buffered(1))
    return pl.pallas_call(
        functools.partial(_hyena_kernel, seq_len=seq_len),
        grid=(bsz,),
        in_specs=[_seq_spec(seq_len, W_SLAB_D), _full_spec((HYENA_CONV, 3 * GROUP_W)),
                  once((HYENA_ORDER, seq_len, GROUP_W)), once((HYENA_ORDER, seq_len, GROUP_W)),
                  _full_spec((HYENA_ORDER, GROUP_W)),
                  once((2 * seq_len, seq_len)), once((2 * seq_len, seq_len)),
                  once((seq_len, 2 * seq_len)), once((seq_len, 2 * seq_len))],
        out_specs=_seq_spec(seq_len, GROUP_W),
        out_shape=jax.ShapeDtypeStruct((bsz, seq_len, GROUP_W), BF16),
        scratch_shapes=[pltpu.VMEM((seq_len + 2 * CONV_PAD, 3 * GROUP_W), F32)],
        compiler_params=pltpu.CompilerParams(dimension_semantics=("parallel",), vmem_limit_bytes=VMEM_LIMIT),
        name="hyena",
    )(slab, conv_w, hr, hs, dbias, *tables)


def _outproj_kernel(x_ref, ya_ref, yb_ref, yc_ref, yd_ref, wo_ref, mod_ref, g_ref, rwh_ref, rwl_ref, rb_ref,
                    x1_ref, h2_ref, gates_ref):
    y = (_dot(ya_ref[...], wo_ref[0:GROUP_W, :]) + _dot(yb_ref[...], wo_ref[GROUP_W:2 * GROUP_W, :])
         + _dot(yc_ref[...], wo_ref[2 * GROUP_W:3 * GROUP_W, :]) + _dot(yd_ref[...], wo_ref[3 * GROUP_W:4 * GROUP_W, :]))
    x1 = x_ref[...] + mod_ref[0, 2:3, :] * y
    x1_ref[...] = x1
    h2 = _modulated_norm(x1, g_ref[...], mod_ref[0, 3:4, :], mod_ref[0, 4:5, :])
    hh, hl = _split2(h2)
    h2_ref[...] = hh
    logits = _dot(hh, rwh_ref[...]) + _dot(hl, rwh_ref[...]) + _dot(hh, rwl_ref[...])
    scores = _sigmoid(logits)
    vals = scores + rb_ref[...]
    lane = _iota(vals.shape, 1)
    picked = jnp.zeros(vals.shape, F32)
    for _ in range(TOP_K):
        best = jnp.max(vals, axis=-1, keepdims=True)
        first = jnp.min(jnp.where(vals == best, lane, LANES), axis=-1, keepdims=True)
        hit = lane == first
        picked = jnp.where(hit, scores, picked)
        vals = jnp.where(hit, -jnp.inf, vals)
    gates = picked / jnp.sum(picked, axis=-1, keepdims=True) * ROUTED_SCALE
    p1 = gates.astype(BF16).astype(F32)
    r1 = gates - p1
    p2 = r1.astype(BF16).astype(F32)
    gates_ref[:, 0:LANES] = (p1 + pltpu.roll(p2, N_EXPERTS, axis=1)).astype(BF16)
    gates_ref[:, LANES:2 * LANES] = (r1 - p2).astype(BF16)


def _outproj(x, ys, w_out, mod, g, router_w, router_bias, seq_len, tm):
    t = x.shape[0]
    per_seq = seq_len // tm if mod.shape[0] > 1 else None

    def mod_idx(i):
        return (i // per_seq if per_seq else 0, 0, 0)

    rw = jnp.zeros((D_MODEL, LANES), F32).at[:, 0:N_EXPERTS].set(router_w)
    rwh = rw.astype(BF16)
    rwl = (rw - rwh.astype(F32)).astype(BF16)
    rb = jnp.full((1, LANES), -jnp.inf, F32).at[0, 0:N_EXPERTS].set(router_bias.astype(F32))
    row = lambda w: pl.BlockSpec((tm, w), lambda i: (i, 0))
    fix = lambda shape: pl.BlockSpec(shape, lambda i: (0,) * len(shape))
    return pl.pallas_call(
        _outproj_kernel,
        grid=(t // tm,),
        in_specs=[row(D_MODEL), row(GROUP_W), row(GROUP_W), row(GROUP_W), row(GROUP_W),
                  fix((D_MODEL, D_MODEL)), pl.BlockSpec((1, 6, D_MODEL), mod_idx), fix((1, D_MODEL)),
                  fix((D_MODEL, LANES)), fix((D_MODEL, LANES)), fix((1, LANES))],
        out_specs=[row(D_MODEL), row(D_MODEL), row(GATE_PACK_W)],
        out_shape=[jax.ShapeDtypeStruct((t, D_MODEL), F32), jax.ShapeDtypeStruct((t, D_MODEL), BF16),
                   jax.ShapeDtypeStruct((t, GATE_PACK_W), BF16)],
        compiler_params=pltpu.CompilerParams(dimension_semantics=("parallel",), vmem_limit_bytes=VMEM_LIMIT),
        name="outproj",
    )(x, *ys, w_out.astype(BF16), mod, g, rwh, rwl, rb)


def _moe_kernel(h_ref, gates_ref, x1_ref, mod_ref, wg_ref, wu_ref, wd_ref, sg_ref, su_ref, sd_ref, fg_ref,
                o_ref, acc_ref, *, final_norm):
    j = pl.program_id(1)
    h = h_ref[...]
    width = EXPERTS_PER_STEP * D_EXPERT

    @pl.when(j == 0)
    def _():
        sh = _silu(_dot(h, sg_ref[...].astype(BF16))) * _dot(h, su_ref[...].astype(BF16))
        acc_ref[...] = _dot(sh.astype(BF16), sd_ref[...].astype(BF16))

    w_gu = jnp.concatenate([wg_ref[e] for e in range(EXPERTS_PER_STEP)]
                           + [wu_ref[e] for e in range(EXPERTS_PER_STEP)], axis=1).astype(BF16)
    gu = _dot(h, w_gu)
    piece_row = _iota((GATE_PACK_W, width), 0)
    sel = ((piece_row % N_EXPERTS == EXPERTS_PER_STEP * j + _iota((GATE_PACK_W, width), 1) // D_EXPERT)
           & (piece_row < 3 * N_EXPERTS))
    gate_w = _dot(gates_ref[...], jnp.where(sel, 1.0, 0.0).astype(BF16))
    hid = (_silu(gu[:, 0:width]) * gu[:, width:2 * width]) * gate_w
    w_dn = jnp.concatenate([wd_ref[e] for e in range(EXPERTS_PER_STEP)], axis=0).astype(BF16)
    acc_ref[...] += _dot(hid.astype(BF16), w_dn)

    @pl.when(j == pl.num_programs(1) - 1)
    def _():
        out = x1_ref[...] + mod_ref[0, 5:6, :] * acc_ref[...]
        if final_norm:
            ms = jnp.mean(out * out, axis=-1, keepdims=True)
            out = out * lax.rsqrt(ms + EPS) * fg_ref[...]
        o_ref[...] = out


def _moe(h2, gates, x1, mod, w_gate, w_up, w_down, sh_gate, sh_up, sh_down, final_g, seq_len, tm, final_norm):
    t = h2.shape[0]
    per_seq = max(seq_len // tm, 1) if mod.shape[0] > 1 else None
    seqs_per_tile = max(tm // seq_len, 1)
    assert mod.shape[0] == 1 or seqs_per_tile == 1

    def mod_idx(i, j):
        return (i // per_seq if per_seq else 0, 0, 0)

    row = lambda w: pl.BlockSpec((tm, w), lambda i, j: (i, 0))
    fix = lambda shape: pl.BlockSpec(shape, lambda i, j: (0,) * len(shape))
    eps_ = EXPERTS_PER_STEP
    return pl.pallas_call(
        functools.partial(_moe_kernel, final_norm=final_norm),
        grid=(t // tm, N_EXPERTS // eps_),
        in_specs=[row(D_MODEL), row(GATE_PACK_W), row(D_MODEL), pl.BlockSpec((1, 6, D_MODEL), mod_idx),
                  pl.BlockSpec((eps_, D_MODEL, D_EXPERT), lambda i, j: (j, 0, 0)),
                  pl.BlockSpec((eps_, D_MODEL, D_EXPERT), lambda i, j: (j, 0, 0)),
                  pl.BlockSpec((eps_, D_EXPERT, D_MODEL), lambda i, j: (j, 0, 0)),
                  fix((D_MODEL, D_EXPERT)), fix((D_MODEL, D_EXPERT)), fix((D_EXPERT, D_MODEL)),
                  fix((1, D_MODEL))],
        out_specs=row(D_MODEL),
        out_shape=jax.ShapeDtypeStruct((t, D_MODEL), F32),
        scratch_shapes=[pltpu.VMEM((tm, D_MODEL), F32)],
        compiler_params=pltpu.CompilerParams(dimension_semantics=("parallel", "arbitrary"),
                                             vmem_limit_bytes=VMEM_LIMIT),
        name="moe",
    )(h2, gates, x1, mod, w_gate, w_up, w_down, sh_gate, sh_up, sh_down, final_g)


def _in_weight_slabs(w_in):
    ga = 4 * GROUP_W
    gb = ga + 4 * N_HEADS
    hb = gb + 5 * GROUP_W
    mc = hb + 4 * GROUP_W
    md = mc + 4 * N_HEADS
    pad = jnp.zeros((D_MODEL, LANES - 4 * N_HEADS), w_in.dtype)
    wa = jnp.concatenate([w_in[:, 0:gb], pad], axis=1)
    wb = w_in[:, gb:hb]
    wc = jnp.concatenate([w_in[:, hb:md], pad], axis=1)
    wd = w_in[:, md:]
    return [w.astype(BF16) for w in (wa, wb, wc, wd)]


def _to_colmajor(t, rows):
    b, l, ch = t.shape
    return t.reshape(b, rows, GRID_W, ch).transpose(0, 2, 1, 3).reshape(b, l, ch)


def _from_colmajor(t, rows):
    b, l, ch = t.shape
    return t.reshape(b, GRID_W, rows, ch).transpose(0, 2, 1, 3).reshape(b, l, ch)


def _layer(x, mod, states, prm, lb, seq_len, grid_rows, tables, final_g, final_norm):
    bsz = x.shape[0] // seq_len
    tm = 512
    s_gdn, s_hgrn, s_c, s_n, s_m = states
    pa, pb, pc, pd = _inproj(x, mod, prm['norm1_g'][None, :], _in_weight_slabs(prm['w_in']), seq_len, tm)
    pa = pa.reshape(bsz, seq_len, W_SLAB_A)
    pb = pb.reshape(bsz, seq_len, W_SLAB_B)
    pc = pc.reshape(bsz, seq_len, W_SLAB_C)
    pd = pd.reshape(bsz, seq_len, W_SLAB_D)

    ya, st_a = _gdn(pa, prm['gdn_conv'], prm['gdn_a_log'], prm['gdn_dt_bias'], prm['gdn_norm_g'],
                    s_gdn.reshape(bsz, 2, GROUP_W, HEAD_DIM), seq_len)
    if grid_rows is not None:
        pb = _to_colmajor(pb, grid_rows)
    yb, st_b = _hgrn(pb, lb, prm['hgrn_norm_g'], s_hgrn.reshape(bsz, 2, GROUP_W, HEAD_DIM), seq_len)
    if grid_rows is not None:
        yb = _from_colmajor(yb, grid_rows)
    yc, st_c, st_n, st_m = _mlstm(pc, prm['mlstm_f_bias'], prm['mlstm_norm_g'],
                                  s_c.reshape(bsz, 2, GROUP_W, HEAD_DIM), s_n.reshape(bsz, 2, 1, GROUP_W),
                                  jnp.repeat(s_m, HEAD_DIM, axis=-1).reshape(bsz, 2, 1, GROUP_W), seq_len)
    hr, hs = _hyena_filters(seq_len, prm['filt_w1'], prm['filt_b1'], prm['filt_w2'], prm['filt_b2'],
                            prm['filt_w3'], prm['filt_freq'], tables)
    yd = _hyena(pd, prm['hyena_conv'], hr, hs, prm['hyena_d'], tables, seq_len)

    t = x.shape[0]
    ys = [y.reshape(t, GROUP_W) for y in (ya, yb, yc, yd)]
    x1, h2, gates = _outproj(x, ys, prm['w_out'], mod, prm['norm2_g'][None, :], prm['router_w'],
                             prm['router_bias'], seq_len, tm)
    tm_moe = 1024
    out = _moe(h2, gates, x1, mod, prm['exp_w_gate'], prm['exp_w_up'], prm['exp_w_down'], prm['sh_w_gate'],
               prm['sh_w_up'], prm['sh_w_down'], final_g[None, :], seq_len, tm_moe, final_norm)
    new_states = (st_a.reshape(bsz, 2, N_HEADS, HEAD_DIM, HEAD_DIM),
                  st_b.reshape(bsz, 2, N_HEADS, HEAD_DIM, HEAD_DIM),
                  st_c.reshape(bsz, 2, N_HEADS, HEAD_DIM, HEAD_DIM),
                  st_n.reshape(bsz, 2, N_HEADS, HEAD_DIM),
                  st_m.reshape(bsz, 2, N_HEADS, HEAD_DIM)[..., 0])
    return out, new_states


def kernel(x_prompt, x_sample, c, state_gdn, state_hgrn, state_mlstm_c, state_mlstm_n, state_mlstm_m, c_ctx, norm1_g, norm2_g, w_mod, b_mod, w_in, gdn_conv, gdn_a_log, gdn_dt_bias, gdn_norm_g, hgrn_lb_logits, hgrn_norm_g, mlstm_f_bias, mlstm_norm_g, hyena_conv, filt_w1, filt_b1, filt_w2, filt_b2, filt_w3, filt_freq, hyena_d, w_out, router_w, router_bias, exp_w_gate, exp_w_up, exp_w_down, sh_w_gate, sh_w_up, sh_w_down, final_g):
    depth = w_in.shape[0]
    bsz, seq, _ = x_prompt.shape
    dbsz, dseq, _ = x_sample.shape
    p = jax.nn.softmax(hgrn_lb_logits.astype(F32), axis=0)
    lower = jnp.cumsum(p, axis=0) - p[0:1]

    per_layer = dict(norm1_g=norm1_g, norm2_g=norm2_g, w_in=w_in, gdn_conv=gdn_conv, gdn_a_log=gdn_a_log,
                     gdn_dt_bias=gdn_dt_bias, gdn_norm_g=gdn_norm_g, hgrn_norm_g=hgrn_norm_g,
                     mlstm_f_bias=mlstm_f_bias, mlstm_norm_g=mlstm_norm_g, hyena_conv=hyena_conv,
                     filt_w1=filt_w1, filt_b1=filt_b1, filt_w2=filt_w2, filt_b2=filt_b2, filt_w3=filt_w3,
                     filt_freq=filt_freq, hyena_d=hyena_d, w_out=w_out, router_w=router_w,
                     router_bias=router_bias, exp_w_gate=exp_w_gate, exp_w_up=exp_w_up, exp_w_down=exp_w_down,
                     sh_w_gate=sh_w_gate, sh_w_up=sh_w_up, sh_w_down=sh_w_down)

    cond = jnp.zeros((N_COND_ROWS, D_MODEL), F32).at[0].set(c_ctx).at[1:1 + dbsz].set(c)
    tables_p = _dft_tables(seq)
    tables_s = _dft_tables(dseq)

    zero_states = (jnp.zeros((bsz, 2, N_HEADS, HEAD_DIM, HEAD_DIM), F32),
                   jnp.zeros((bsz, 2, N_HEADS, HEAD_DIM, HEAD_DIM), F32),
                   jnp.zeros((bsz, 2, N_HEADS, HEAD_DIM, HEAD_DIM), F32),
                   jnp.zeros((bsz, 2, N_HEADS, HEAD_DIM), F32),
                   jnp.zeros((bsz, 2, N_HEADS), F32))
    h = x_prompt.reshape(bsz * seq, D_MODEL)
    z = x_sample.reshape(dbsz * dseq, D_MODEL)
    new_states = [[] for _ in range(5)]
    for l in range(depth):
        prm = {name: val[l] for name, val in per_layer.items()}
        mod = _adaln(cond, w_mod[l], b_mod[l]).reshape(N_COND_ROWS, 6, D_MODEL)
        last = l == depth - 1
        h, st = _layer(h, mod[0:1], zero_states, prm, lower[l], seq, None, tables_p, final_g, last)
        for acc, s in zip(new_states, st):
            acc.append(s.astype(x_prompt.dtype))
        cached = (state_gdn[:, l].astype(F32), state_hgrn[:, l].astype(F32), state_mlstm_c[:, l].astype(F32),
                  state_mlstm_n[:, l].astype(F32), state_mlstm_m[:, l].astype(F32))
        z, _ = _layer(z, mod[1:1 + dbsz], cached, prm, lower[l], dseq, dseq // GRID_W, tables_s, final_g, last)

    return (h.reshape(bsz, seq, D_MODEL), z.reshape(dbsz, dseq, D_MODEL),
            *[jnp.stack(acc, axis=1) for acc in new_states])
```

```python
import functools
import math

import numpy as np
import jax
import jax.numpy as jnp
from jax import lax
from jax.experimental import pallas as pl
from jax.experimental.pallas import tpu as pltpu

F32 = jnp.float32
BF16 = jnp.bfloat16

D_MODEL = 1024
N_HEADS = 4
HEAD_DIM = 64
GROUP_W = N_HEADS * HEAD_DIM
GRID_W = 64
CHUNK = 64
HGRN_CHUNK = 16
HYENA_SEQS = 2
GDN_GROUP = 4
GDN_CONV = 5
HYENA_CONV = 3
HYENA_ORDER = 2
POS_BANDS = 16
POS_DIM = 1 + 2 * POS_BANDS
FILTER_HIDDEN = 64
N_EXPERTS = 64
TOP_K = 8
D_EXPERT = 128
ROUTED_SCALE = 2.5
EPS = 1e-6
LOG2_E = 1.4426950408889634
LANES = 128
CONV_PAD = 8
EXPERTS_PER_STEP = 4
GATE_PACK_W = 2 * LANES
N_COND_ROWS = 16
VMEM_LIMIT = 56 * 1024 * 1024

W_SLAB_A = 4 * GROUP_W + LANES
W_SLAB_B = 5 * GROUP_W
W_SLAB_C = 4 * GROUP_W + LANES
W_SLAB_D = 3 * GROUP_W


def _split2(x):
    hi = x.astype(BF16)
    lo = (x - hi.astype(F32)).astype(BF16)
    return hi, lo


def _split3(x):
    p1 = x.astype(BF16)
    r = x - p1.astype(F32)
    p2 = r.astype(BF16)
    p3 = (r - p2.astype(F32)).astype(BF16)
    return p1, p2, p3


def _dot(a, b):
    return jnp.dot(a, b, preferred_element_type=F32)


def _dot_nt(a, b):
    return lax.dot_general(a, b, (((1,), (1,)), ((), ())), preferred_element_type=F32)


def _dot_tn(a, b):
    return lax.dot_general(a, b, (((0,), (0,)), ((), ())), preferred_element_type=F32)


def _dot3(a, b):
    ah, al = _split2(a)
    bh, bl = _split2(b)
    return _dot(ah, bh) + _dot(al, bh) + _dot(ah, bl)


def _dot_sel(x, sel):
    p1, p2, p3 = _split3(x)
    return _dot(p1, sel) + _dot(p2, sel) + _dot(p3, sel)


def _sel_dot(sel, x):
    p1, p2, p3 = _split3(x)
    return _dot(sel, p1) + _dot(sel, p2) + _dot(sel, p3)


def _iota(shape, dim):
    return lax.broadcasted_iota(jnp.int32, shape, dim)


def _head_ones():
    r = _iota((GROUP_W, GROUP_W), 0) // HEAD_DIM
    c = _iota((GROUP_W, GROUP_W), 1) // HEAD_DIM
    return r == c


def _seg_sum(x, ones_bd):
    hi, lo = _split2(x)
    return _dot(hi, ones_bd) + _dot(lo, ones_bd)


def _blockdiag(x, bd_mask):
    return jnp.where(bd_mask, jnp.concatenate([x] * N_HEADS, axis=0), jnp.zeros((), x.dtype))


def _sigmoid(x):
    return 1.0 / (1.0 + jnp.exp(-x))


def _silu(x):
    return x * _sigmoid(x)


def _softplus(x):
    return jnp.maximum(x, 0.0) + jnp.log(1.0 + jnp.exp(-jnp.abs(x)))


def _seg_cumsum(x, chunk, reverse):
    n = x.shape[0]
    pos = _iota(x.shape, 0) % chunk
    s = 1
    while s < chunk:
        if reverse:
            x = x + jnp.where(pos < chunk - s, pltpu.roll(x, n - s, axis=0), 0.0)
        else:
            x = x + jnp.where(pos >= s, pltpu.roll(x, s, axis=0), 0.0)
        s *= 2
    return x


def _swap_row_groups(x, a, b):
    n, ch = x.shape
    return jnp.swapaxes(x.reshape(a, b, ch), 0, 1).reshape(n, ch)


def _dir_cumsum(x, chunk, lane):
    backward = (lane // N_HEADS) % 2 == 1
    return jnp.where(backward, _seg_cumsum(x, chunk, True), _seg_cumsum(x, chunk, False))


def _short_conv(pad_ref, x, w_ref, width, n):
    ch = x.shape[1]
    pad_ref[0:CONV_PAD, :] = jnp.zeros((CONV_PAD, ch), F32)
    pad_ref[CONV_PAD + n:2 * CONV_PAD + n, :] = jnp.zeros((CONV_PAD, ch), F32)
    pad_ref[CONV_PAD:CONV_PAD + n, :] = x
    acc = None
    for j in range(width):
        off = CONV_PAD - width // 2 + j
        term = pad_ref[off:off + n, :] * w_ref[j:j + 1, :]
        acc = term if acc is None else acc + term
    return acc


def _chunk_masks(reverse):
    r = _iota((CHUNK, GROUP_W), 0)
    j = _iota((CHUNK, GROUP_W), 1) % HEAD_DIM
    if reverse:
        return j >= r, j > r, j == r
    return j <= r, j < r, j == r


def _adaln_kernel(c_ref, w_ref, b_ref, o_ref):
    c = c_ref[...]
    o_ref[...] = _dot3(_silu(c), w_ref[...]) + b_ref[...]


def _adaln(cond, w_mod, b_mod):
    n_out = w_mod.shape[1]
    tn = 1536
    return pl.pallas_call(
        _adaln_kernel,
        grid=(n_out // tn,),
        in_specs=[pl.BlockSpec((N_COND_ROWS, D_MODEL), lambda j: (0, 0)),
                  pl.BlockSpec((D_MODEL, tn), lambda j: (0, j)),
                  pl.BlockSpec((1, tn), lambda j: (0, j))],
        out_specs=pl.BlockSpec((N_COND_ROWS, tn), lambda j: (0, j)),
        out_shape=jax.ShapeDtypeStruct((N_COND_ROWS, n_out), F32),
        compiler_params=pltpu.CompilerParams(dimension_semantics=("arbitrary",), vmem_limit_bytes=VMEM_LIMIT),
        name="adaln",
    )(cond, w_mod, b_mod.reshape(1, n_out))


def _modulated_norm(x, g, shift, scale):
    ms = jnp.mean(x * x, axis=-1, keepdims=True)
    return (x * lax.rsqrt(ms + EPS) * g) * (1.0 + scale) + shift


def _inproj_kernel(x_ref, mod_ref, g_ref, wa_ref, wb_ref, wc_ref, wd_ref, oa_ref, ob_ref, oc_ref, od_ref):
    h = _modulated_norm(x_ref[...], g_ref[...], mod_ref[0, 0:1, :], mod_ref[0, 1:2, :]).astype(BF16)
    oa_ref[...] = _dot(h, wa_ref[...])
    ob_ref[...] = _dot(h, wb_ref[...])
    oc_ref[...] = _dot(h, wc_ref[...])
    od_ref[...] = _dot(h, wd_ref[...])


def _inproj(x, mod, g, w_slabs, seq_len, tm):
    t = x.shape[0]
    per_seq = seq_len // tm if mod.shape[0] > 1 else None

    def mod_idx(i):
        return (i // per_seq if per_seq else 0, 0, 0)

    widths = (W_SLAB_A, W_SLAB_B, W_SLAB_C, W_SLAB_D)
    return pl.pallas_call(
        _inproj_kernel,
        grid=(t // tm,),
        in_specs=[pl.BlockSpec((tm, D_MODEL), lambda i: (i, 0)),
                  pl.BlockSpec((1, 6, D_MODEL), mod_idx),
                  pl.BlockSpec((1, D_MODEL), lambda i: (0, 0))]
                 + [pl.BlockSpec((D_MODEL, w), lambda i: (0, 0)) for w in widths],
        out_specs=[pl.BlockSpec((tm, w), lambda i: (i, 0)) for w in widths],
        out_shape=[jax.ShapeDtypeStruct((t, w), F32) for w in widths],
        compiler_params=pltpu.CompilerParams(dimension_semantics=("parallel",), vmem_limit_bytes=VMEM_LIMIT),
        name="inproj",
    )(x, mod, g, *w_slabs)


def _chunk_masks_f32(reverse):
    return tuple(jnp.where(m, 1.0, 0.0) for m in _chunk_masks(reverse))


def _blockdiag_mul(x, bd_factor):
    return jnp.concatenate([x] * N_HEADS, axis=0) * bd_factor


def _tri_inverse(lms, eyes, same16, same32, bd_b):
    def prod(xs, ys):
        staged = []
        for a, b in zip(xs, ys):
            ah, al = _split2(a)
            staged.append((jnp.concatenate([ah, al], axis=0), _blockdiag_mul(b.astype(BF16), bd_b)))
        outs = []
        for a_hl, b_bd in staged:
            n = a_hl.shape[0] // 2
            hl = _dot(a_hl, b_bd)
            outs.append(hl[0:n] + hl[n:2 * n])
        return outs

    ms = [-(lm * same16) for lm in lms]
    ps = [eye + m for eye, m in zip(eyes, ms)]
    ms = prod(ms, ms)
    for _ in range(2):
        pms = prod([jnp.concatenate([p, m], axis=0) for p, m in zip(ps, ms)], ms)
        ps = [p + pm[0:CHUNK] for p, pm in zip(ps, pms)]
        ms = [pm[CHUNK:2 * CHUNK] for pm in pms]
    ps = [p + pm for p, pm in zip(ps, prod(ps, ms))]
    for keep in (same32 - same16, 1.0 - same32):
        offs = [lm * keep for lm in lms]
        ps = [p - c for p, c in zip(ps, prod(ps, prod(offs, ps)))]
    return ps


def _gdn_kernel(p_ref, conv_ref, alog_ref, dtb_ref, ng_ref, ea_ref, t4_ref, t4t_ref, s0_ref,
                y_ref, sout_ref,
                q_s, k_s, v_s, gc_s, beta_s, u_s, o_s, w_s, qe_s, attn_s, kdec_s, st_s, pad_s, *, seq_len):
    n_chunks = seq_len // CHUNK
    bd_f = jnp.where(_head_ones(), 1.0, 0.0)
    bd_b = bd_f.astype(BF16)
    ones_bd = bd_b

    qkv = _silu(_short_conv(pad_s, p_ref[0, :, 0:3 * GROUP_W], conv_ref, GDN_CONV, seq_len))
    q = qkv[:, 0:GROUP_W]
    k = qkv[:, GROUP_W:2 * GROUP_W]
    q_s[...] = q * lax.rsqrt(_seg_sum(q * q, ones_bd) + EPS) * (HEAD_DIM ** -0.5)
    k_s[...] = k * lax.rsqrt(_seg_sum(k * k, ones_bd) + EPS)
    v_s[...] = qkv[:, 2 * GROUP_W:3 * GROUP_W]

    raw = p_ref[0, :, 4 * GROUP_W:4 * GROUP_W + LANES]
    lane = _iota(raw.shape, 1)
    log_decay = -jnp.exp(alog_ref[...]) * _softplus(raw + dtb_ref[...])
    compact = jnp.where(lane < 2 * N_HEADS, _dir_cumsum(log_decay, CHUNK, lane),
                        jnp.where(lane < N_GATE_LANES, _sigmoid(raw), 0.0))
    gcb = _expand_gates(compact, ea_ref[...])
    for d in range(2):
        gc_s[d] = gcb[:, d * GROUP_W:(d + 1) * GROUP_W]
        beta_s[d] = gcb[:, (2 + d) * GROUP_W:(3 + d) * GROUP_W]
        st_s[d] = bd_f * _dot_sel(s0_ref[0, d], t4_ref[...])

    masks = [_chunk_masks_f32(False), _chunk_masks_f32(True)]
    r = _iota((CHUNK, GROUP_W), 0)
    j = _iota((CHUNK, GROUP_W), 1) % HEAD_DIM
    same16 = jnp.where((r // 16) == (j // 16), 1.0, 0.0)
    same32 = jnp.where((r // 32) == (j // 32), 1.0, 0.0)
    ones_cc = jnp.ones((CHUNK, CHUNK), BF16)

    def prepare_body(n, carry):
        ds = [d for _ in range(GDN_GROUP) for d in range(2)]
        rows = [pl.ds(pl.multiple_of((GDN_GROUP * n + c) * CHUNK, CHUNK), CHUNK)
                for c in range(GDN_GROUP) for _ in range(2)]
        q_c = [q_s[r, :] for r in rows]
        k_c = [k_s[r, :] for r in rows]
        gc = [gc_s[d, r, :] for d, r in zip(ds, rows)]
        beta = [beta_s[d, r, :] for d, r in zip(ds, rows)]
        gc_row = [_sel_dot(ones_cc, g * masks[d][2]) for d, g in zip(ds, gc)]
        decay = [masks[d][0] * jnp.exp(jnp.minimum(g - gr, 0.0)) for d, g, gr in zip(ds, gc, gc_row)]
        kb = [k * b for k, b in zip(k_c, beta)]
        kq = [_dot_nt(jnp.concatenate([kbi, qi], axis=0).astype(BF16), _blockdiag_mul(ki.astype(BF16), bd_b))
              for kbi, qi, ki in zip(kb, q_c, k_c)]
        tmat = _tri_inverse([kqi[0:CHUNK] * (dec * masks[d][1]) for d, kqi, dec in zip(ds, kq, decay)],
                            [masks[d][2] for d in ds], same16, same32, bd_b)
        egc = [jnp.exp(g) for g in gc]
        rhs = [jnp.concatenate([_blockdiag_mul((v_s[r, :] * b).astype(BF16), bd_b),
                                _blockdiag_mul((kbi * e).astype(BF16), bd_b)], axis=1)
               for r, b, kbi, e in zip(rows, beta, kb, egc)]
        uw = [_dot(t.astype(BF16), x) for t, x in zip(tmat, rhs)]
        for i, (d, r) in enumerate(zip(ds, rows)):
            last = 0 if d == 1 else CHUNK - 1
            u_s[d, r, :] = uw[i][:, 0:GROUP_W]
            w_s[d, r, :] = uw[i][:, GROUP_W:2 * GROUP_W].astype(BF16)
            qe_s[d, r, :] = (q_c[i] * egc[i]).astype(BF16)
            attn_s[d, r, :] = (kq[i][CHUNK:2 * CHUNK] * decay[i]).astype(BF16)
            kdec_s[d, r, :] = (k_c[i] * jnp.exp(gc[i][last:last + 1, :] - gc[i])).astype(BF16)
        return carry

    lax.fori_loop(0, n_chunks // GDN_GROUP, prepare_body, 0)

    def scan_body(n, carry):
        r0 = [pl.multiple_of(n * CHUNK, CHUNK), pl.multiple_of((n_chunks - 1 - n) * CHUNK, CHUNK)]
        rows = [pl.ds(r, CHUNK) for r in r0]
        last = [CHUNK - 1, 0]
        s_bd = [st_s[d] for d in range(2)]
        ws_qs = [_dot(jnp.concatenate([w_s[d, rows[d], :], qe_s[d, rows[d], :]], axis=0), s_bd[d].astype(BF16))
                 for d in range(2)]
        v_new = [(u_s[d, rows[d], :] - ws_qs[d][0:CHUNK]).astype(BF16) for d in range(2)]
        intra = [_dot(attn_s[d, rows[d], :], _blockdiag_mul(v_new[d], bd_b)) for d in range(2)]
        outer = [_dot_tn(kdec_s[d, rows[d], :], v_new[d]) for d in range(2)]
        for d in range(2):
            o_s[d, rows[d], :] = ws_qs[d][CHUNK:2 * CHUNK] + intra[d]
            g_last = jnp.exp(gc_s[d, pl.ds(r0[d] + last[d], 1), :])
            st_s[d] = s_bd[d] * g_last + bd_f * outer[d]
        return carry

    lax.fori_loop(0, n_chunks, scan_body, 0)

    o = o_s[0] + o_s[1]
    o = o * lax.rsqrt(_seg_sum(o * o, ones_bd) * (1.0 / HEAD_DIM) + EPS) * ng_ref[...]
    y_ref[0] = (o * _silu(p_ref[0, :, 3 * GROUP_W:4 * GROUP_W])).astype(BF16)
    for d in range(2):
        sout_ref[0, d] = _dot_sel(st_s[d], t4t_ref[...])


def _compact_row(m, kind):
    row = jnp.zeros((1, LANES), F32)
    return row.at[0, kind * 2 * N_HEADS:(kind + 1) * 2 * N_HEADS].set(m.astype(F32).reshape(2 * N_HEADS))


N_GATE_LANES = 4 * N_HEADS


def _gate_expander():
    sel = np.zeros((LANES, 4 * GROUP_W), np.float32)
    for piece in range(3):
        for kind in range(2):
            for d in range(2):
                for h in range(N_HEADS):
                    c0 = (kind * 2 + d) * GROUP_W + h * HEAD_DIM
                    sel[piece * N_GATE_LANES + kind * 2 * N_HEADS + d * N_HEADS + h, c0:c0 + HEAD_DIM] = 1.0
    return jnp.asarray(sel, BF16)


def _expand_gates(g, sel):
    p1 = g.astype(BF16).astype(F32)
    r1 = g - p1
    p2 = r1.astype(BF16).astype(F32)
    packed = p1 + pltpu.roll(p2, N_GATE_LANES, axis=1) + pltpu.roll(r1 - p2, 2 * N_GATE_LANES, axis=1)
    return _dot(packed.astype(BF16), sel)


def _tile4():
    t = np.concatenate([np.eye(HEAD_DIM, dtype=np.float32)] * N_HEADS, axis=1)
    return jnp.asarray(t, BF16), jnp.asarray(t.T, BF16)


def _seq_spec(seq_len, width):
    return pl.BlockSpec((1, seq_len, width), lambda b: (b, 0, 0))


def _full_spec(shape):
    nd = len(shape)
    return pl.BlockSpec(shape, lambda b: (0,) * nd)


def _state_spec():
    return pl.BlockSpec((1, 2, GROUP_W, HEAD_DIM), lambda b: (b, 0, 0, 0))


def _gdn(slab, conv_w, a_log, dt_bias, norm_g, s0, seq_len):
    assert seq_len % (CHUNK * GDN_GROUP) == 0
    bsz = slab.shape[0]
    t4, t4t = _tile4()
    return pl.pallas_call(
        functools.partial(_gdn_kernel, seq_len=seq_len),
        grid=(bsz,),
        in_specs=[_seq_spec(seq_len, W_SLAB_A), _full_spec((GDN_CONV, 3 * GROUP_W)), _full_spec((1, LANES)),
                  _full_spec((1, LANES)), _full_spec((1, GROUP_W)), _full_spec((LANES, 4 * GROUP_W)),
                  _full_spec((HEAD_DIM, GROUP_W)), _full_spec((GROUP_W, HEAD_DIM)), _state_spec()],
        out_specs=[_seq_spec(seq_len, GROUP_W), _state_spec()],
        out_shape=[jax.ShapeDtypeStruct((bsz, seq_len, GROUP_W), BF16),
                   jax.ShapeDtypeStruct((bsz, 2, GROUP_W, HEAD_DIM), F32)],
        scratch_shapes=[pltpu.VMEM((seq_len, GROUP_W), F32)] * 3
                       + [pltpu.VMEM((2, seq_len, GROUP_W), F32)] * 4
                       + [pltpu.VMEM((2, seq_len, GROUP_W), BF16)] * 4
                       + [pltpu.VMEM((2, GROUP_W, GROUP_W), F32),
                          pltpu.VMEM((seq_len + 2 * CONV_PAD, 3 * GROUP_W), F32)],
        compiler_params=pltpu.CompilerParams(dimension_semantics=("parallel",), vmem_limit_bytes=VMEM_LIMIT),
        name="gdn",
    )(slab, conv_w, _compact_row(a_log, 0), _compact_row(dt_bias, 0), jnp.tile(norm_g, N_HEADS)[None, :],
      _gate_expander(), t4, t4t, s0)


def _mlstm_kernel(p_ref, fb_ref, ng_ref, eg_ref, t4_ref, t4t_ref, c0_ref, n0_ref, m0_ref,
                  y_ref, cout_ref, nout_ref, mout_ref,
                  k_s, bc_s, ig_s, h_s, c_s, n_s, m_s, *, seq_len):
    n_chunks = seq_len // CHUNK
    bd_mask = _head_ones()
    bd_f = jnp.where(bd_mask, 1.0, 0.0)
    bd_b = bd_f.astype(BF16)
    ones_bd = bd_b
    lane_head = _iota((CHUNK, GROUP_W), 1) // HEAD_DIM

    k_s[...] = p_ref[0, :, GROUP_W:2 * GROUP_W] * (HEAD_DIM ** -0.5)
    raw = p_ref[0, :, 4 * GROUP_W:4 * GROUP_W + LANES]
    lane = _iota(raw.shape, 1)
    logf = -_softplus(-(raw + fb_ref[...]))
    compact = jnp.where(lane < 2 * N_HEADS, raw,
                        jnp.where(lane < N_GATE_LANES, _dir_cumsum(logf, CHUNK, lane), 0.0))
    gates = _expand_gates(compact, eg_ref[...])
    for d in range(2):
        ig_s[d] = gates[:, d * GROUP_W:(d + 1) * GROUP_W]
        bc_s[d] = gates[:, (2 + d) * GROUP_W:(3 + d) * GROUP_W]
        c_s[d] = jnp.where(bd_mask, _dot_sel(c0_ref[0, d], t4_ref[...]), 0.0)
        n_s[d] = n0_ref[0, d]
        m_s[d] = m0_ref[0, d]

    masks = [_chunk_masks(False), _chunk_masks(True)]
    ones_cc = jnp.ones((CHUNK, CHUNK), BF16)

    def seg_max(dm):
        dmax = jnp.zeros((CHUNK, GROUP_W), F32)
        for h in range(N_HEADS):
            in_head = lane_head == h
            mh = jnp.max(jnp.where(in_head, dm, -jnp.inf), axis=1, keepdims=True)
            dmax = jnp.where(in_head, mh, dmax)
        return dmax

    def body(n, carry):
        dirs = range(2)
        last = [CHUNK - 1, 0]
        rows = [pl.ds(pl.multiple_of(n * CHUNK, CHUNK), CHUNK),
                pl.ds(pl.multiple_of((n_chunks - 1 - n) * CHUNK, CHUNK), CHUNK)]
        q_c = [p_ref[0, r, 0:GROUP_W] for r in rows]
        k_c = [k_s[r, :] for r in rows]
        v_b = [p_ref[0, r, 2 * GROUP_W:3 * GROUP_W].astype(BF16) for r in rows]
        bc = [bc_s[d, rows[d], :] for d in dirs]
        x = [ig_s[d, rows[d], :] - bc[d] for d in dirs]
        x_row = [_sel_dot(ones_cc, jnp.where(masks[d][2], x[d], 0.0)) for d in dirs]
        dm = [jnp.where(masks[d][0], bc[d] + x_row[d], -jnp.inf) for d in dirs]
        dmax = [seg_max(dm[d]) for d in dirs]
        a = [bc[d] + m_s[d] for d in dirs]
        m_t = [jnp.maximum(a[d], dmax[d]) for d in dirs]
        inter = [jnp.exp(a[d] - m_t[d]) for d in dirs]
        q_b = [q.astype(BF16) for q in q_c]
        qk = [_dot_nt(q_b[d], _blockdiag_mul(k_c[d].astype(BF16), bd_b)) for d in dirs]
        w = [(jnp.exp(dm[d] - m_t[d]) * qk[d]).astype(BF16) for d in dirs]
        c_bd = [c_s[d] for d in dirs]
        n_row = [n_s[d] for d in dirs]
        q_cm = [_dot(q_b[d], c_bd[d].astype(BF16)) for d in dirs]
        q_n = [_seg_sum(q_c[d] * n_row[d], ones_bd) for d in dirs]
        wv = [_dot(w[d], jnp.concatenate([_blockdiag_mul(v_b[d], bd_b), ones_bd], axis=1)) for d in dirs]
        m_new = [m_t[d][last[d]:last[d] + 1, :] for d in dirs]
        carry_w = [jnp.exp(a[d][last[d]:last[d] + 1, :] - m_new[d]) for d in dirs]
        kw = [k_c[d] * jnp.exp(bc[d][last[d]:last[d] + 1, :] + x[d] - m_new[d]) for d in dirs]
        outer = [_dot_tn(kw[d].astype(BF16), v_b[d]) for d in dirs]
        for d in dirs:
            num = inter[d] * q_cm[d] + wv[d][:, 0:GROUP_W]
            den = inter[d] * q_n[d] + wv[d][:, GROUP_W:2 * GROUP_W]
            h_s[d, rows[d], :] = num / jnp.maximum(jnp.abs(den), jnp.exp(-m_t[d]))
            c_s[d] = carry_w[d] * c_bd[d] + bd_f * outer[d]
            n_s[d] = carry_w[d] * n_row[d] + jnp.sum(kw[d], axis=0, keepdims=True)
            m_s[d] = m_new[d]
        return carry

    lax.fori_loop(0, n_chunks, body, 0)

    h = h_s[0] + h_s[1]
    h = h * lax.rsqrt(_seg_sum(h * h, ones_bd) * (1.0 / HEAD_DIM) + EPS) * ng_ref[...]
    y_ref[0] = (h * _sigmoid(p_ref[0, :, 3 * GROUP_W:4 * GROUP_W])).astype(BF16)
    for d in range(2):
        cout_ref[0, d] = _dot_sel(c_s[d], t4t_ref[...])
        nout_ref[0, d] = n_s[d]
        mout_ref[0, d] = m_s[d]


def _row_state_spec():
    return pl.BlockSpec((1, 2, 1, GROUP_W), lambda b: (b, 0, 0, 0))


def _mlstm(slab, f_bias, norm_g, c0, n0, m0, seq_len):
    bsz = slab.shape[0]
    t4, t4t = _tile4()
    return pl.pallas_call(
        functools.partial(_mlstm_kernel, seq_len=seq_len),
        grid=(bsz,),
        in_specs=[_seq_spec(seq_len, W_SLAB_C), _full_spec((1, LANES)), _full_spec((1, GROUP_W)),
                  _full_spec((LANES, 4 * GROUP_W)), _full_spec((HEAD_DIM, GROUP_W)),
                  _full_spec((GROUP_W, HEAD_DIM)), _state_spec(), _row_state_spec(), _row_state_spec()],
        out_specs=[_seq_spec(seq_len, GROUP_W), _state_spec(), _row_state_spec(), _row_state_spec()],
        out_shape=[jax.ShapeDtypeStruct((bsz, seq_len, GROUP_W), BF16),
                   jax.ShapeDtypeStruct((bsz, 2, GROUP_W, HEAD_DIM), F32),
                   jax.ShapeDtypeStruct((bsz, 2, 1, GROUP_W), F32),
                   jax.ShapeDtypeStruct((bsz, 2, 1, GROUP_W), F32)],
        scratch_shapes=[pltpu.VMEM((seq_len, GROUP_W), F32), pltpu.VMEM((2, seq_len, GROUP_W), F32),
                        pltpu.VMEM((2, seq_len, GROUP_W), F32), pltpu.VMEM((2, seq_len, GROUP_W), F32),
                        pltpu.VMEM((2, GROUP_W, GROUP_W), F32), pltpu.VMEM((2, 1, GROUP_W), F32),
                        pltpu.VMEM((2, 1, GROUP_W), F32)],
        compiler_params=pltpu.CompilerParams(dimension_semantics=("parallel",), vmem_limit_bytes=VMEM_LIMIT),
        name="mlstm",
    )(slab, _compact_row(f_bias, 1), jnp.tile(norm_g, N_HEADS)[None, :], _gate_expander(), t4, t4t, c0, n0, m0)


def _hgrn_kernel(p_ref, lb_ref, ng_ref, t4_ref, t4t_ref, s0_ref, y_ref, sout_ref,
                 k_s, gc_s, o_s, st_s, *scan_order_s, seq_len, grid_rows):
    c = HGRN_CHUNK
    n_chunks = seq_len // c
    if grid_rows is None:
        src = p_ref.at[0]
    else:
        (src,) = scan_order_s
        src[...] = _swap_row_groups(p_ref[0], grid_rows, GRID_W)
    bd_mask = _head_ones()
    bd_f = jnp.where(bd_mask, 1.0, 0.0)
    ones_bd = bd_f.astype(BF16)
    sub = _iota((c, GROUP_W), 0)
    row_sum = jnp.where(_iota((c, c * c), 1) // c == _iota((c, c * c), 0), 1.0, 0.0).astype(BF16)

    for d in range(2):
        lb = lb_ref[d:d + 1, :]
        forget = lb + (1.0 - lb) * _sigmoid(src[:, (3 + d) * GROUP_W:(4 + d) * GROUP_W])
        k_s[d] = 1.0 - forget
        gc_s[d] = _seg_cumsum(jnp.log(forget), c, reverse=(d == 1))
        s_bd = jnp.where(bd_mask, _dot_sel(s0_ref[0, d], t4_ref[...]), 0.0)
        st_s[d] = s_bd.T

    def pair_weights(d, q_c, k_c, gc):
        gc2 = gc * LOG2_E
        pieces = []
        for i in range(c):
            keep = (sub >= i) if d == 1 else (sub <= i)
            e_i = jnp.where(keep, jnp.exp2(gc2[i:i + 1, :] - gc2), 0.0)
            pieces.append(e_i * q_c[i:i + 1, :] * k_c)
        return jnp.concatenate(pieces, axis=0).astype(BF16)

    def body(n, carry):
        ds = [0, 1, 0, 1]
        starts = [2 * n, n_chunks - 1 - 2 * n, 2 * n + 1, n_chunks - 2 - 2 * n]
        rows = [pl.ds(pl.multiple_of(s * c, c), c) for s in starts]
        q_c = [src[r, 0:GROUP_W] for r in rows]
        v_c = [src[r, GROUP_W:2 * GROUP_W] for r in rows]
        k_c = [k_s[d, r, :] for d, r in zip(ds, rows)]
        gc = [gc_s[d, r, :] for d, r in zip(ds, rows)]
        pairs =[pair_weights(d, q, k, g) for d, q, k, g in zip(ds, q_c, k_c, gc)]
        attn = [_dot(p, ones_bd) for p in pairs]
        weighted = [(a * jnp.concatenate([v] * c, axis=0)).astype(BF16) for a, v in zip(attn, v_c)]
        intra = [_dot(row_sum, w) for w in weighted]
        q_dec = [(q * jnp.exp(g)).astype(BF16) for q, g in zip(q_c, gc)]
        gc_last = [g[(0 if d == 1 else c - 1):(1 if d == 1 else c), :] for d, g in zip(ds, gc)]
        outer = [_dot_tn(v.astype(BF16), (k * jnp.exp(gl - g)).astype(BF16))
                 for v, k, gl, g in zip(v_c, k_c, gc_last, gc)]
        for i in range(4):
            d = ds[i]
            st = st_s[d]
            o_s[d, rows[i], :] = intra[i] + _dot_nt(q_dec[i], st.astype(BF16))
            st_s[d] = st * jnp.exp(gc_last[i]) + bd_f * outer[i]
        return carry

    lax.fori_loop(0, n_chunks // 2, body, 0)

    o = o_s[0] + o_s[1]
    o = o * lax.rsqrt(_seg_sum(o * o, ones_bd) * (1.0 / HEAD_DIM) + EPS) * ng_ref[...]
    y = o * _sigmoid(src[:, 2 * GROUP_W:3 * GROUP_W])
    if grid_rows is None:
        y_ref[0] = y.astype(BF16)
    else:
        y_ref[0] = _swap_row_groups(y, GRID_W, grid_rows).astype(BF16)
    for d in range(2):
        sout_ref[0, d] = _dot_sel(st_s[d].T, t4t_ref[...])


def _hgrn(slab, lb, norm_g, s0, seq_len, grid_rows):
    bsz = slab.shape[0]
    t4, t4t = _tile4()
    scan_order = [] if grid_rows is None else [pltpu.VMEM((seq_len, W_SLAB_B), F32)]
    return pl.pallas_call(
        functools.partial(_hgrn_kernel, seq_len=seq_len, grid_rows=grid_rows),
        grid=(bsz,),
        in_specs=[_seq_spec(seq_len, W_SLAB_B), _full_spec((2, GROUP_W)), _full_spec((1, GROUP_W)),
                  _full_spec((HEAD_DIM, GROUP_W)), _full_spec((GROUP_W, HEAD_DIM)), _state_spec()],
        out_specs=[_seq_spec(seq_len, GROUP_W), _state_spec()],
        out_shape=[jax.ShapeDtypeStruct((bsz, seq_len, GROUP_W), BF16),
                   jax.ShapeDtypeStruct((bsz, 2, GROUP_W, HEAD_DIM), F32)],
        scratch_shapes=[pltpu.VMEM((2, seq_len, GROUP_W), F32), pltpu.VMEM((2, seq_len, GROUP_W), F32),
                        pltpu.VMEM((2, seq_len, GROUP_W), F32), pltpu.VMEM((2, GROUP_W, GROUP_W), F32)] + scan_order,
        compiler_params=pltpu.CompilerParams(dimension_semantics=("parallel",), vmem_limit_bytes=VMEM_LIMIT),
        name="hgrn",
    )(slab, lb.astype(F32), jnp.tile(norm_g, N_HEADS)[None, :], t4, t4t, s0)


def _dft_tables(n):
    k = np.arange(n, dtype=np.float64)[:, None]
    s = np.arange(n, dtype=np.float64)[None, :]
    ang = np.pi * k * s / n
    cos_m = np.cos(ang)
    sin_m = np.sin(ang)
    alt = (-1.0) ** np.arange(n)
    sin_m[0, :] = alt
    fwd = np.concatenate([cos_m, sin_m], axis=0)
    wgt = np.full((n,), 2.0 / (2 * n))
    wgt[0] = 1.0 / (2 * n)
    inv = np.concatenate([cos_m.T * wgt[None, :], sin_m.T * wgt[None, :]], axis=1)
    out = []
    for m in (fwd, inv):
        m32 = m.astype(np.float32)
        hi = m32.astype(BF16)
        lo = (m32 - hi.astype(np.float32)).astype(BF16)
        out.append(jnp.asarray(np.concatenate([hi, lo], axis=1)))
    return out


def _dft_apply(m_ref, x):
    xb = x.astype(BF16)
    return _dot(m_ref[...], jnp.concatenate([xb, xb], axis=0))


def _dft_apply_f32(m_ref, x):
    k = x.shape[0]
    xh, xl = _split2(x)
    return _dot(m_ref[...], jnp.concatenate([xh, xh], axis=0)) + _dot(m_ref[:, 0:k], xl)


def _filter_kernel(feat_ref, env_ref, w1_ref, b1_ref, w2_ref, b2_ref, w3_ref, freq_ref, f_ref,
                   hr_ref, hs_ref, *, seq_len):
    freq = freq_ref[...]
    hid = jnp.sin(freq * (_dot3(feat_ref[...], w1_ref[...]) + b1_ref[...]))
    hid = jnp.sin(freq * (_dot3(hid, w2_ref[...]) + b2_ref[...]))
    h = _dot3(hid, w3_ref[...])
    env = env_ref[...]
    not_first = _iota((seq_len, GROUP_W), 0) > 0
    first_row = _iota((seq_len, GROUP_W), 0) == 0
    sides = []
    for o in range(HYENA_ORDER):
        pos = h[:, (2 * o) * GROUP_W:(2 * o + 1) * GROUP_W] * env
        neg = jnp.where(not_first, h[:, (2 * o + 1) * GROUP_W:(2 * o + 2) * GROUP_W] * env, 0.0)
        ss = jnp.sum(pos * pos + neg * neg, axis=0, keepdims=True)
        scale = lax.rsqrt(ss + EPS)
        sides += [(pos + neg) * scale, (pos - neg) * scale]
    spec = _dft_apply_f32(f_ref, jnp.concatenate(sides, axis=1))
    for o in range(HYENA_ORDER):
        fa = spec[:, (2 * o) * GROUP_W:(2 * o + 1) * GROUP_W]
        fb = spec[:, (2 * o + 1) * GROUP_W:(2 * o + 2) * GROUP_W]
        hr_ref[o] = fa[0:seq_len]
        hs_ref[o] = jnp.where(first_row, fa[seq_len:2 * seq_len], fb[seq_len:2 * seq_len])


def _hyena_filters(seq_len, w1, b1, w2, b2, w3, freq, tables):
    t = np.arange(seq_len, dtype=np.float64)
    tn = np.linspace(0.0, 1.0, seq_len)
    bands = np.linspace(1e-4, POS_BANDS - 1, POS_BANDS)
    ang = (2.0 * math.pi / seq_len) * t[:, None] * bands[None, :]
    feats = np.zeros((seq_len, LANES), np.float32)
    feats[:, 0:POS_DIM] = np.concatenate([tn[:, None], np.cos(ang), -np.sin(ang)], axis=-1)
    deltas = np.abs(np.linspace(math.log(1e-2) / 1.5, math.log(1e-2) / 0.3, GROUP_W))
    env = np.exp(-tn[:, None] * deltas[None, :]).astype(np.float32)
    w1p = jnp.zeros((LANES, FILTER_HIDDEN), F32).at[0:POS_DIM].set(w1)
    n_h = HYENA_ORDER * 2 * GROUP_W
    full = lambda shape: pl.BlockSpec(shape, lambda: (0,) * len(shape))
    return pl.pallas_call(
        functools.partial(_filter_kernel, seq_len=seq_len),
        in_specs=[full((seq_len, LANES)), full((seq_len, GROUP_W)), full((LANES, FILTER_HIDDEN)),
                  full((1, FILTER_HIDDEN)), full((FILTER_HIDDEN, FILTER_HIDDEN)), full((1, FILTER_HIDDEN)),
                  full((FILTER_HIDDEN, n_h)), full((1, FILTER_HIDDEN)),
                  full((2 * seq_len, 2 * seq_len))],
        out_specs=[full((HYENA_ORDER, seq_len, GROUP_W)), full((HYENA_ORDER, seq_len, GROUP_W))],
        out_shape=[jax.ShapeDtypeStruct((HYENA_ORDER, seq_len, GROUP_W), F32)] * 2,
        compiler_params=pltpu.CompilerParams(vmem_limit_bytes=VMEM_LIMIT),
        name="hyena_filter",
    )(jnp.asarray(feats), jnp.asarray(env), w1p, b1[None, :], w2, b2[None, :], w3, freq[None, :], tables[0])


def _hyena_kernel(p_ref, conv_ref, hr_ref, hs_ref, db_ref, f_ref, i_ref, y_ref, pad_s, *, seq_len):
    us = [_short_conv(pad_s, p_ref[b], conv_ref, HYENA_CONV, seq_len) for b in range(HYENA_SEQS)]
    part = lambda k: jnp.concatenate([u[:, k * GROUP_W:(k + 1) * GROUP_W] for u in us], axis=1)
    wide = lambda t: jnp.concatenate([t] * HYENA_SEQS, axis=1)
    first_row = _iota((seq_len, HYENA_SEQS * GROUP_W), 0) == 0

    def longconv(z, o):
        zf = _dft_apply(f_ref, z)
        zr = zf[0:seq_len]
        zs = zf[seq_len:2 * seq_len]
        hr = wide(hr_ref[o])
        hs = wide(hs_ref[o])
        yr = zr * hr - jnp.where(first_row, 0.0, zs * hs)
        ys = jnp.where(first_row, zs * hs, zr * hs + zs * hr)
        y = _dft_apply(i_ref, jnp.concatenate([yr, ys], axis=0))
        return y + wide(db_ref[o:o + 1, :]) * z

    y = part(2) * longconv(part(1) * longconv(part(0), 0), 1)
    for b in range(HYENA_SEQS):
        y_ref[b] = y[:, b * GROUP_W:(b + 1) * GROUP_W].astype(BF16)


def _hyena(slab, conv_w, hr, hs, dbias, tables, seq_len):
    bsz = slab.shape[0]
    once = lambda shape: pl.BlockSpec(shape, lambda b: (0,) * len(shape), pipeline_mode=pl.Buffered(1))
    return pl.pallas_call(
        functools.partial(_hyena_kernel, seq_len=seq_len),
        grid=(bsz // HYENA_SEQS,),
        in_specs=[pl.BlockSpec((HYENA_SEQS, seq_len, W_SLAB_D), lambda b: (b, 0, 0), pipeline_mode=pl.Buffered(1)),
                  _full_spec((HYENA_CONV, 3 * GROUP_W)),
                  once((HYENA_ORDER, seq_len, GROUP_W)), once((HYENA_ORDER, seq_len, GROUP_W)),
                  _full_spec((HYENA_ORDER, GROUP_W)),
                  once((2 * seq_len, 2 * seq_len)), once((seq_len, 4 * seq_len))],
        out_specs=pl.BlockSpec((HYENA_SEQS, seq_len, GROUP_W), lambda b: (b, 0, 0)),
        out_shape=jax.ShapeDtypeStruct((bsz, seq_len, GROUP_W), BF16),
        scratch_shapes=[pltpu.VMEM((seq_len + 2 * CONV_PAD, 3 * GROUP_W), F32)],
        compiler_params=pltpu.CompilerParams(dimension_semantics=("parallel",), vmem_limit_bytes=VMEM_LIMIT),
        name="hyena",
    )(slab, conv_w, hr, hs, dbias, *tables)


def _outproj_kernel(x_ref, ya_ref, yb_ref, yc_ref, yd_ref, wo_ref, mod_ref, g_ref, rwh_ref, rwl_ref, rb_ref,
                    x1_ref, h2_ref, gates_ref):
    y = (_dot(ya_ref[...], wo_ref[0:GROUP_W, :]) + _dot(yb_ref[...], wo_ref[GROUP_W:2 * GROUP_W, :])
         + _dot(yc_ref[...], wo_ref[2 * GROUP_W:3 * GROUP_W, :]) + _dot(yd_ref[...], wo_ref[3 * GROUP_W:4 * GROUP_W, :]))
    x1 = x_ref[...] + mod_ref[0, 2:3, :] * y
    x1_ref[...] = x1
    h2 = _modulated_norm(x1, g_ref[...], mod_ref[0, 3:4, :], mod_ref[0, 4:5, :])
    hh, hl = _split2(h2)
    h2_ref[...] = hh
    logits = _dot(hh, rwh_ref[...]) + _dot(hl, rwh_ref[...]) + _dot(hh, rwl_ref[...])
    scores = _sigmoid(logits)
    vals = scores + rb_ref[...]
    lane = _iota(vals.shape, 1).astype(F32)
    picked = jnp.zeros(vals.shape, F32)
    for _ in range(TOP_K):
        best = jnp.max(vals, axis=-1, keepdims=True)
        first = jnp.min(jnp.where(vals == best, lane, float(LANES)), axis=-1, keepdims=True)
        hit = lane == first
        picked = jnp.where(hit, scores, picked)
        vals = jnp.where(hit, -jnp.inf, vals)
    gates = picked / jnp.sum(picked, axis=-1, keepdims=True) * ROUTED_SCALE
    p1 = gates.astype(BF16).astype(F32)
    r1 = gates - p1
    p2 = r1.astype(BF16).astype(F32)
    gates_ref[:, 0:LANES] = (p1 + pltpu.roll(p2, N_EXPERTS, axis=1)).astype(BF16)
    gates_ref[:, LANES:2 * LANES] = (r1 - p2).astype(BF16)


def _outproj(x, ys, w_out, mod, g, router_w, router_bias, seq_len, tm):
    t = x.shape[0]
    per_seq = seq_len // tm if mod.shape[0] > 1 else None

    def mod_idx(i):
        return (i // per_seq if per_seq else 0, 0, 0)

    rw = jnp.zeros((D_MODEL, LANES), F32).at[:, 0:N_EXPERTS].set(router_w)
    rwh = rw.astype(BF16)
    rwl = (rw - rwh.astype(F32)).astype(BF16)
    rb = jnp.full((1, LANES), -jnp.inf, F32).at[0, 0:N_EXPERTS].set(router_bias.astype(F32))
    row = lambda w: pl.BlockSpec((tm, w), lambda i: (i, 0))
    fix = lambda shape: pl.BlockSpec(shape, lambda i: (0,) * len(shape))
    return pl.pallas_call(
        _outproj_kernel,
        grid=(t // tm,),
        in_specs=[row(D_MODEL), row(GROUP_W), row(GROUP_W), row(GROUP_W), row(GROUP_W),
                  fix((D_MODEL, D_MODEL)), pl.BlockSpec((1, 6, D_MODEL), mod_idx), fix((1, D_MODEL)),
                  fix((D_MODEL, LANES)), fix((D_MODEL, LANES)), fix((1, LANES))],
        out_specs=[row(D_MODEL), row(D_MODEL), row(GATE_PACK_W)],
        out_shape=[jax.ShapeDtypeStruct((t, D_MODEL), F32), jax.ShapeDtypeStruct((t, D_MODEL), BF16),
                   jax.ShapeDtypeStruct((t, GATE_PACK_W), BF16)],
        compiler_params=pltpu.CompilerParams(dimension_semantics=("parallel",), vmem_limit_bytes=VMEM_LIMIT),
        name="outproj",
    )(x, *ys, w_out.astype(BF16), mod, g, rwh, rwl, rb)


def _moe_kernel(h_ref, gates_ref, x1_ref, mod_ref, wg_ref, wu_ref, wd_ref, sg_ref, su_ref, sd_ref, fg_ref,
                o_ref, acc_ref, *, final_norm):
    j = pl.program_id(1)
    h = h_ref[...]
    width = EXPERTS_PER_STEP * D_EXPERT

    @pl.when(j == 0)
    def _():
        sh = _silu(_dot(h, sg_ref[...].astype(BF16))) * _dot(h, su_ref[...].astype(BF16))
        acc_ref[...] = _dot(sh.astype(BF16), sd_ref[...].astype(BF16))

    w_gu = jnp.concatenate([wg_ref[e] for e in range(EXPERTS_PER_STEP)]
                           + [wu_ref[e] for e in range(EXPERTS_PER_STEP)], axis=1).astype(BF16)
    gu = _dot(h, w_gu)
    piece_row = _iota((GATE_PACK_W, width), 0)
    sel = ((piece_row % N_EXPERTS == EXPERTS_PER_STEP * j + _iota((GATE_PACK_W, width), 1) // D_EXPERT)
           & (piece_row < 3 * N_EXPERTS))
    gate_w = _dot(gates_ref[...], jnp.where(sel, 1.0, 0.0).astype(BF16))
    hid = (_silu(gu[:, 0:width]) * gu[:, width:2 * width]) * gate_w
    w_dn = jnp.concatenate([wd_ref[e] for e in range(EXPERTS_PER_STEP)], axis=0).astype(BF16)
    acc_ref[...] += _dot(hid.astype(BF16), w_dn)

    @pl.when(j == pl.num_programs(1) - 1)
    def _():
        out = x1_ref[...] + mod_ref[0, 5:6, :] * acc_ref[...]
        if final_norm:
            ms = jnp.mean(out * out, axis=-1, keepdims=True)
            out = out * lax.rsqrt(ms + EPS) * fg_ref[...]
        o_ref[...] = out


def _moe(h2, gates, x1, mod, w_gate, w_up, w_down, sh_gate, sh_up, sh_down, final_g, seq_len, tm, final_norm):
    t = h2.shape[0]
    per_seq = max(seq_len // tm, 1) if mod.shape[0] > 1 else None
    seqs_per_tile = max(tm // seq_len, 1)
    assert mod.shape[0] == 1 or seqs_per_tile == 1

    def mod_idx(i, j):
        return (i // per_seq if per_seq else 0, 0, 0)

    row = lambda w: pl.BlockSpec((tm, w), lambda i, j: (i, 0))
    fix = lambda shape: pl.BlockSpec(shape, lambda i, j: (0,) * len(shape))
    eps_ = EXPERTS_PER_STEP
    return pl.pallas_call(
        functools.partial(_moe_kernel, final_norm=final_norm),
        grid=(t // tm, N_EXPERTS // eps_),
        in_specs=[row(D_MODEL), row(GATE_PACK_W), row(D_MODEL), pl.BlockSpec((1, 6, D_MODEL), mod_idx),
                  pl.BlockSpec((eps_, D_MODEL, D_EXPERT), lambda i, j: (j, 0, 0)),
                  pl.BlockSpec((eps_, D_MODEL, D_EXPERT), lambda i, j: (j, 0, 0)),
                  pl.BlockSpec((eps_, D_EXPERT, D_MODEL), lambda i, j: (j, 0, 0)),
                  fix((D_MODEL, D_EXPERT)), fix((D_MODEL, D_EXPERT)), fix((D_EXPERT, D_MODEL)),
                  fix((1, D_MODEL))],
        out_specs=row(D_MODEL),
        out_shape=jax.ShapeDtypeStruct((t, D_MODEL), F32),
        scratch_shapes=[pltpu.VMEM((tm, D_MODEL), F32)],
        compiler_params=pltpu.CompilerParams(dimension_semantics=("parallel", "arbitrary"),
                                             vmem_limit_bytes=VMEM_LIMIT),
        name="moe",
    )(h2, gates, x1, mod, w_gate, w_up, w_down, sh_gate, sh_up, sh_down, final_g)


def _in_weight_slabs(w_in):
    ga = 4 * GROUP_W
    gb = ga + 4 * N_HEADS
    hb = gb + 5 * GROUP_W
    mc = hb + 4 * GROUP_W
    md = mc + 4 * N_HEADS
    pad = jnp.zeros((D_MODEL, LANES - 4 * N_HEADS), w_in.dtype)
    wa = jnp.concatenate([w_in[:, 0:gb], pad], axis=1)
    wb = w_in[:, gb:hb]
    wc = jnp.concatenate([w_in[:, hb:md], pad], axis=1)
    wd = w_in[:, md:]
    return [w.astype(BF16) for w in (wa, wb, wc, wd)]


def _layer(x, mod, states, prm, lb, seq_len, grid_rows, tables, final_g, final_norm):
    bsz = x.shape[0] // seq_len
    tm = 512
    s_gdn, s_hgrn, s_c, s_n, s_m = states
    pa, pb, pc, pd = _inproj(x, mod, prm['norm1_g'][None, :], _in_weight_slabs(prm['w_in']), seq_len, tm)
    pa = pa.reshape(bsz, seq_len, W_SLAB_A)
    pb = pb.reshape(bsz, seq_len, W_SLAB_B)
    pc = pc.reshape(bsz, seq_len, W_SLAB_C)
    pd = pd.reshape(bsz, seq_len, W_SLAB_D)

    ya, st_a = _gdn(pa, prm['gdn_conv'], prm['gdn_a_log'], prm['gdn_dt_bias'], prm['gdn_norm_g'],
                    s_gdn.reshape(bsz, 2, GROUP_W, HEAD_DIM), seq_len)
    yb, st_b = _hgrn(pb, lb, prm['hgrn_norm_g'], s_hgrn.reshape(bsz, 2, GROUP_W, HEAD_DIM), seq_len, grid_rows)
    yc, st_c, st_n, st_m = _mlstm(pc, prm['mlstm_f_bias'], prm['mlstm_norm_g'],
                                  s_c.reshape(bsz, 2, GROUP_W, HEAD_DIM), s_n.reshape(bsz, 2, 1, GROUP_W),
                                  jnp.repeat(s_m, HEAD_DIM, axis=-1).reshape(bsz, 2, 1, GROUP_W), seq_len)
    hr, hs = _hyena_filters(seq_len, prm['filt_w1'], prm['filt_b1'], prm['filt_w2'], prm['filt_b2'],
                            prm['filt_w3'], prm['filt_freq'], tables)
    yd = _hyena(pd, prm['hyena_conv'], hr, hs, prm['hyena_d'], tables, seq_len)

    t = x.shape[0]
    ys = [y.reshape(t, GROUP_W) for y in (ya, yb, yc, yd)]
    x1, h2, gates = _outproj(x, ys, prm['w_out'], mod, prm['norm2_g'][None, :], prm['router_w'],
                             prm['router_bias'], seq_len, tm)
    tm_moe = 1024
    out = _moe(h2, gates, x1, mod, prm['exp_w_gate'], prm['exp_w_up'], prm['exp_w_down'], prm['sh_w_gate'],
               prm['sh_w_up'], prm['sh_w_down'], final_g[None, :], seq_len, tm_moe, final_norm)
    new_states = (st_a.reshape(bsz, 2, N_HEADS, HEAD_DIM, HEAD_DIM),
                  st_b.reshape(bsz, 2, N_HEADS, HEAD_DIM, HEAD_DIM),
                  st_c.reshape(bsz, 2, N_HEADS, HEAD_DIM, HEAD_DIM),
                  st_n.reshape(bsz, 2, N_HEADS, HEAD_DIM),
                  st_m.reshape(bsz, 2, N_HEADS, HEAD_DIM)[..., 0])
    return out, new_states


def kernel(x_prompt, x_sample, c, state_gdn, state_hgrn, state_mlstm_c, state_mlstm_n, state_mlstm_m, c_ctx, norm1_g, norm2_g, w_mod, b_mod, w_in, gdn_conv, gdn_a_log, gdn_dt_bias, gdn_norm_g, hgrn_lb_logits, hgrn_norm_g, mlstm_f_bias, mlstm_norm_g, hyena_conv, filt_w1, filt_b1, filt_w2, filt_b2, filt_w3, filt_freq, hyena_d, w_out, router_w, router_bias, exp_w_gate, exp_w_up, exp_w_down, sh_w_gate, sh_w_up, sh_w_down, final_g):
    depth = w_in.shape[0]
    bsz, seq, _ = x_prompt.shape
    dbsz, dseq, _ = x_sample.shape
    p = jax.nn.softmax(hgrn_lb_logits.astype(F32), axis=0)
    lower = jnp.cumsum(p, axis=0) - p[0:1]

    per_layer = dict(norm1_g=norm1_g, norm2_g=norm2_g, w_in=w_in, gdn_conv=gdn_conv, gdn_a_log=gdn_a_log,
                     gdn_dt_bias=gdn_dt_bias, gdn_norm_g=gdn_norm_g, hgrn_norm_g=hgrn_norm_g,
                     mlstm_f_bias=mlstm_f_bias, mlstm_norm_g=mlstm_norm_g, hyena_conv=hyena_conv,
                     filt_w1=filt_w1, filt_b1=filt_b1, filt_w2=filt_w2, filt_b2=filt_b2, filt_w3=filt_w3,
                     filt_freq=filt_freq, hyena_d=hyena_d, w_out=w_out, router_w=router_w,
                     router_bias=router_bias, exp_w_gate=exp_w_gate, exp_w_up=exp_w_up, exp_w_down=exp_w_down,
                     sh_w_gate=sh_w_gate, sh_w_up=sh_w_up, sh_w_down=sh_w_down)

    cond = jnp.zeros((N_COND_ROWS, D_MODEL), F32).at[0].set(c_ctx).at[1:1 + dbsz].set(c)
    tables_p = _dft_tables(seq)
    tables_s = _dft_tables(dseq)

    zero_states = (jnp.zeros((bsz, 2, N_HEADS, HEAD_DIM, HEAD_DIM), F32),
                   jnp.zeros((bsz, 2, N_HEADS, HEAD_DIM, HEAD_DIM), F32),
                   jnp.zeros((bsz, 2, N_HEADS, HEAD_DIM, HEAD_DIM), F32),
                   jnp.zeros((bsz, 2, N_HEADS, HEAD_DIM), F32),
                   jnp.zeros((bsz, 2, N_HEADS), F32))
    h = x_prompt.reshape(bsz * seq, D_MODEL)
    z = x_sample.reshape(dbsz * dseq, D_MODEL)
    new_states = [[] for _ in range(5)]
    for l in range(depth):
        prm = {name: val[l] for name, val in per_layer.items()}
        mod = _adaln(cond, w_mod[l], b_mod[l]).reshape(N_COND_ROWS, 6, D_MODEL)
        last = l == depth - 1
        h, st = _layer(h, mod[0:1], zero_states, prm, lower[l], seq, None, tables_p, final_g, last)
        for acc, s in zip(new_states, st):
            acc.append(s.astype(x_prompt.dtype))
        cached = (state_gdn[:, l].astype(F32), state_hgrn[:, l].astype(F32), state_mlstm_c[:, l].astype(F32),
                  state_mlstm_n[:, l].astype(F32), state_mlstm_m[:, l].astype(F32))
        z, _ = _layer(z, mod[1:1 + dbsz], cached, prm, lower[l], dseq, dseq // GRID_W, tables_s, final_g, last)

    return (h.reshape(bsz, seq, D_MODEL), z.reshape(dbsz, dseq, D_MODEL),
            *[jnp.stack(acc, axis=1) for acc in new_states])
```

```python
import functools
import math

import numpy as np
import jax
import jax.numpy as jnp
from jax import lax
from jax.experimental import pallas as pl
from jax.experimental.pallas import tpu as pltpu

F32 = jnp.float32
BF16 = jnp.bfloat16

D_MODEL = 1024
N_HEADS = 4
HEAD_DIM = 64
GROUP_W = N_HEADS * HEAD_DIM
GRID_W = 64
CHUNK = 64
HGRN_CHUNK = 16
OUTPROJ_PARTS = 2
HYENA_SEQS = 2
GDN_GROUP = 4
GDN_CONV = 5
HYENA_CONV = 3
HYENA_ORDER = 2
POS_BANDS = 16
POS_DIM = 1 + 2 * POS_BANDS
FILTER_HIDDEN = 64
N_EXPERTS = 64
TOP_K = 8
D_EXPERT = 128
ROUTED_SCALE = 2.5
EPS = 1e-6
LOG2_E = 1.4426950408889634
LANES = 128
CONV_PAD = 8
EXPERTS_PER_STEP = 4
N_COND_ROWS = 16
VMEM_LIMIT = 56 * 1024 * 1024

W_SLAB_A = 4 * GROUP_W + LANES
W_SLAB_B = 5 * GROUP_W
W_SLAB_C = 4 * GROUP_W + LANES
W_SLAB_D = 3 * GROUP_W


def _split2(x):
    hi = x.astype(BF16)
    lo = (x - hi.astype(F32)).astype(BF16)
    return hi, lo


def _split3(x):
    p1 = x.astype(BF16)
    r = x - p1.astype(F32)
    p2 = r.astype(BF16)
    p3 = (r - p2.astype(F32)).astype(BF16)
    return p1, p2, p3


def _dot(a, b):
    return jnp.dot(a, b, preferred_element_type=F32)


def _dot_nt(a, b):
    return lax.dot_general(a, b, (((1,), (1,)), ((), ())), preferred_element_type=F32)


def _dot_tn(a, b):
    return lax.dot_general(a, b, (((0,), (0,)), ((), ())), preferred_element_type=F32)


def _dot3(a, b):
    ah, al = _split2(a)
    bh, bl = _split2(b)
    return _dot(ah, bh) + _dot(al, bh) + _dot(ah, bl)


def _dot_sel(x, sel):
    p1, p2, p3 = _split3(x)
    return _dot(p1, sel) + _dot(p2, sel) + _dot(p3, sel)


def _sel_dot(sel, x):
    p1, p2, p3 = _split3(x)
    return _dot(sel, p1) + _dot(sel, p2) + _dot(sel, p3)


def _iota(shape, dim):
    return lax.broadcasted_iota(jnp.int32, shape, dim)


def _head_ones():
    r = _iota((GROUP_W, GROUP_W), 0) // HEAD_DIM
    c = _iota((GROUP_W, GROUP_W), 1) // HEAD_DIM
    return r == c


def _seg_sum(x, ones_bd):
    hi, lo = _split2(x)
    return _dot(hi, ones_bd) + _dot(lo, ones_bd)


def _blockdiag(x, bd_mask):
    return jnp.where(bd_mask, jnp.concatenate([x] * N_HEADS, axis=0), jnp.zeros((), x.dtype))


def _sigmoid(x):
    return 1.0 / (1.0 + jnp.exp(-x))


def _silu(x):
    return x * _sigmoid(x)


def _softplus(x):
    return jnp.maximum(x, 0.0) + jnp.log(1.0 + jnp.exp(-jnp.abs(x)))


def _seg_cumsum(x, chunk, reverse):
    n = x.shape[0]
    pos = _iota(x.shape, 0) % chunk
    s = 1
    while s < chunk:
        if reverse:
            x = x + jnp.where(pos < chunk - s, pltpu.roll(x, n - s, axis=0), 0.0)
        else:
            x = x + jnp.where(pos >= s, pltpu.roll(x, s, axis=0), 0.0)
        s *= 2
    return x


def _swap_row_groups(x, a, b):
    n, ch = x.shape
    return jnp.swapaxes(x.reshape(a, b, ch), 0, 1).reshape(n, ch)


def _dir_cumsum(x, chunk, lane):
    backward = (lane // N_HEADS) % 2 == 1
    return jnp.where(backward, _seg_cumsum(x, chunk, True), _seg_cumsum(x, chunk, False))


def _short_conv(pad_ref, x, w_ref, width, n):
    ch = x.shape[1]
    pad_ref[0:CONV_PAD, :] = jnp.zeros((CONV_PAD, ch), F32)
    pad_ref[CONV_PAD + n:2 * CONV_PAD + n, :] = jnp.zeros((CONV_PAD, ch), F32)
    pad_ref[CONV_PAD:CONV_PAD + n, :] = x
    acc = None
    for j in range(width):
        off = CONV_PAD - width // 2 + j
        term = pad_ref[off:off + n, :] * w_ref[j:j + 1, :]
        acc = term if acc is None else acc + term
    return acc


def _chunk_masks(reverse):
    r = _iota((CHUNK, GROUP_W), 0)
    j = _iota((CHUNK, GROUP_W), 1) % HEAD_DIM
    if reverse:
        return j >= r, j > r, j == r
    return j <= r, j < r, j == r


def _adaln_kernel(c_ref, w_ref, b_ref, o_ref):
    c = c_ref[...]
    o_ref[...] = _dot3(_silu(c), w_ref[...]) + b_ref[...]


def _adaln(cond, w_mod, b_mod, layer):
    n_out = w_mod.shape[2]
    tn = 1536
    return pl.pallas_call(
        _adaln_kernel,
        grid=(n_out // tn,),
        in_specs=[pl.BlockSpec((N_COND_ROWS, D_MODEL), lambda j: (0, 0)),
                  pl.BlockSpec((None, D_MODEL, tn), lambda j: (layer, 0, j)),
                  pl.BlockSpec((1, tn), lambda j: (0, j))],
        out_specs=pl.BlockSpec((N_COND_ROWS, tn), lambda j: (0, j)),
        out_shape=jax.ShapeDtypeStruct((N_COND_ROWS, n_out), F32),
        compiler_params=pltpu.CompilerParams(dimension_semantics=("arbitrary",), vmem_limit_bytes=VMEM_LIMIT),
        name="adaln",
    )(cond, w_mod, b_mod.reshape(1, n_out))


def _modulated_norm(x, g, shift, scale):
    ms = jnp.mean(x * x, axis=-1, keepdims=True)
    return (x * lax.rsqrt(ms + EPS) * g) * (1.0 + scale) + shift


def _inproj_kernel(x_ref, mod_ref, g_ref, wa_ref, wb_ref, wc_ref, wd_ref, oa_ref, ob_ref, oc_ref, od_ref):
    h = _modulated_norm(x_ref[...], g_ref[...], mod_ref[0, 0:1, :], mod_ref[0, 1:2, :]).astype(BF16)
    oa_ref[...] = _dot(h, wa_ref[...])
    ob_ref[...] = _dot(h, wb_ref[...])
    oc_ref[...] = _dot(h, wc_ref[...])
    od_ref[...] = _dot(h, wd_ref[...])


def _inproj(x, mod, g, w_slabs, seq_len, tm):
    t = x.shape[0]
    per_seq = seq_len // tm if mod.shape[0] > 1 else None

    def mod_idx(i):
        return (i // per_seq if per_seq else 0, 0, 0)

    widths = (W_SLAB_A, W_SLAB_B, W_SLAB_C, W_SLAB_D)
    return pl.pallas_call(
        _inproj_kernel,
        grid=(t // tm,),
        in_specs=[pl.BlockSpec((tm, D_MODEL), lambda i: (i, 0)),
                  pl.BlockSpec((1, 6, D_MODEL), mod_idx),
                  pl.BlockSpec((1, D_MODEL), lambda i: (0, 0))]
                 + [pl.BlockSpec((D_MODEL, w), lambda i: (0, 0)) for w in widths],
        out_specs=[pl.BlockSpec((tm, w), lambda i: (i, 0)) for w in widths],
        out_shape=[jax.ShapeDtypeStruct((t, w), F32) for w in widths],
        compiler_params=pltpu.CompilerParams(dimension_semantics=("parallel",), vmem_limit_bytes=VMEM_LIMIT),
        name="inproj",
    )(x, mod, g, *w_slabs)


def _chunk_masks_f32(reverse):
    return tuple(jnp.where(m, 1.0, 0.0) for m in _chunk_masks(reverse))


def _blockdiag_mul(x, bd_factor):
    return jnp.concatenate([x] * N_HEADS, axis=0) * bd_factor


def _tri_inverse(lms, eyes, same16, same32, bd_b):
    def prod(xs, ys):
        staged = []
        for a, b in zip(xs, ys):
            ah, al = _split2(a)
            staged.append((jnp.concatenate([ah, al], axis=0), _blockdiag_mul(b.astype(BF16), bd_b)))
        outs = []
        for a_hl, b_bd in staged:
            n = a_hl.shape[0] // 2
            hl = _dot(a_hl, b_bd)
            outs.append(hl[0:n] + hl[n:2 * n])
        return outs

    ms = [-(lm * same16) for lm in lms]
    ps = [eye + m for eye, m in zip(eyes, ms)]
    ms = prod(ms, ms)
    for _ in range(2):
        pms = prod([jnp.concatenate([p, m], axis=0) for p, m in zip(ps, ms)], ms)
        ps = [p + pm[0:CHUNK] for p, pm in zip(ps, pms)]
        ms = [pm[CHUNK:2 * CHUNK] for pm in pms]
    ps = [p + pm for p, pm in zip(ps, prod(ps, ms))]
    for keep in (same32 - same16, 1.0 - same32):
        offs = [lm * keep for lm in lms]
        ps = [p - c for p, c in zip(ps, prod(ps, prod(offs, ps)))]
    return ps


def _gdn_kernel(p_ref, conv_ref, alog_ref, dtb_ref, ng_ref, ea_ref, t4_ref, t4t_ref, s0_ref,
                y_ref, sout_ref,
                q_s, k_s, v_s, gc_s, beta_s, u_s, o_s, w_s, qe_s, attn_s, kdec_s, st_s, pad_s, *, seq_len):
    n_chunks = seq_len // CHUNK
    bd_f = jnp.where(_head_ones(), 1.0, 0.0)
    bd_b = bd_f.astype(BF16)
    ones_bd = bd_b

    qkv = _silu(_short_conv(pad_s, p_ref[0, :, 0:3 * GROUP_W], conv_ref, GDN_CONV, seq_len))
    q = qkv[:, 0:GROUP_W]
    k = qkv[:, GROUP_W:2 * GROUP_W]
    q_s[...] = q * lax.rsqrt(_seg_sum(q * q, ones_bd) + EPS) * (HEAD_DIM ** -0.5)
    k_s[...] = k * lax.rsqrt(_seg_sum(k * k, ones_bd) + EPS)
    v_s[...] = qkv[:, 2 * GROUP_W:3 * GROUP_W]

    raw = p_ref[0, :, 4 * GROUP_W:4 * GROUP_W + LANES]
    lane = _iota(raw.shape, 1)
    log_decay = -jnp.exp(alog_ref[...]) * _softplus(raw + dtb_ref[...])
    compact = jnp.where(lane < 2 * N_HEADS, _dir_cumsum(log_decay, CHUNK, lane),
                        jnp.where(lane < N_GATE_LANES, _sigmoid(raw), 0.0))
    gcb = _expand_gates(compact, ea_ref[...])
    for d in range(2):
        gc_s[d] = gcb[:, d * GROUP_W:(d + 1) * GROUP_W]
        beta_s[d] = gcb[:, (2 + d) * GROUP_W:(3 + d) * GROUP_W]
        st_s[d] = bd_f * _dot_sel(s0_ref[0, d], t4_ref[...])

    masks = [_chunk_masks_f32(False), _chunk_masks_f32(True)]
    r = _iota((CHUNK, GROUP_W), 0)
    j = _iota((CHUNK, GROUP_W), 1) % HEAD_DIM
    same16 = jnp.where((r // 16) == (j // 16), 1.0, 0.0)
    same32 = jnp.where((r // 32) == (j // 32), 1.0, 0.0)
    ones_cc = jnp.ones((CHUNK, CHUNK), BF16)

    def prepare_body(n, carry):
        ds = [d for _ in range(GDN_GROUP) for d in range(2)]
        rows = [pl.ds(pl.multiple_of((GDN_GROUP * n + c) * CHUNK, CHUNK), CHUNK)
                for c in range(GDN_GROUP) for _ in range(2)]
        q_c = [q_s[r, :] for r in rows]
        k_c = [k_s[r, :] for r in rows]
        gc = [gc_s[d, r, :] for d, r in zip(ds, rows)]
        beta = [beta_s[d, r, :] for d, r in zip(ds, rows)]
        gc_row = [_sel_dot(ones_cc, g * masks[d][2]) for d, g in zip(ds, gc)]
        decay = [masks[d][0] * jnp.exp(jnp.minimum(g - gr, 0.0)) for d, g, gr in zip(ds, gc, gc_row)]
        kb = [k * b for k, b in zip(k_c, beta)]
        kq = [_dot_nt(jnp.concatenate([kbi, qi], axis=0).astype(BF16), _blockdiag_mul(ki.astype(BF16), bd_b))
              for kbi, qi, ki in zip(kb, q_c, k_c)]
        tmat = _tri_inverse([kqi[0:CHUNK] * (dec * masks[d][1]) for d, kqi, dec in zip(ds, kq, decay)],
                            [masks[d][2] for d in ds], same16, same32, bd_b)
        egc = [jnp.exp(g) for g in gc]
        rhs = [jnp.concatenate([_blockdiag_mul((v_s[r, :] * b).astype(BF16), bd_b),
                                _blockdiag_mul((kbi * e).astype(BF16), bd_b)], axis=1)
               for r, b, kbi, e in zip(rows, beta, kb, egc)]
        uw = [_dot(t.astype(BF16), x) for t, x in zip(tmat, rhs)]
        for i, (d, r) in enumerate(zip(ds, rows)):
            last = 0 if d == 1 else CHUNK - 1
            u_s[d, r, :] = uw[i][:, 0:GROUP_W]
            w_s[d, r, :] = uw[i][:, GROUP_W:2 * GROUP_W].astype(BF16)
            qe_s[d, r, :] = (q_c[i] * egc[i]).astype(BF16)
            attn_s[d, r, :] = (kq[i][CHUNK:2 * CHUNK] * decay[i]).astype(BF16)
            kdec_s[d, r, :] = (k_c[i] * jnp.exp(gc[i][last:last + 1, :] - gc[i])).astype(BF16)
        return carry

    lax.fori_loop(0, n_chunks // GDN_GROUP, prepare_body, 0)

    def scan_body(n, carry):
        r0 = [pl.multiple_of(n * CHUNK, CHUNK), pl.multiple_of((n_chunks - 1 - n) * CHUNK, CHUNK)]
        rows = [pl.ds(r, CHUNK) for r in r0]
        last = [CHUNK - 1, 0]
        s_bd = [st_s[d] for d in range(2)]
        ws_qs = [_dot(jnp.concatenate([w_s[d, rows[d], :], qe_s[d, rows[d], :]], axis=0), s_bd[d].astype(BF16))
                 for d in range(2)]
        v_new = [(u_s[d, rows[d], :] - ws_qs[d][0:CHUNK]).astype(BF16) for d in range(2)]
        intra = [_dot(attn_s[d, rows[d], :], _blockdiag_mul(v_new[d], bd_b)) for d in range(2)]
        outer = [_dot_tn(kdec_s[d, rows[d], :], v_new[d]) for d in range(2)]
        for d in range(2):
            o_s[d, rows[d], :] = ws_qs[d][CHUNK:2 * CHUNK] + intra[d]
            g_last = jnp.exp(gc_s[d, pl.ds(r0[d] + last[d], 1), :])
            st_s[d] = s_bd[d] * g_last + bd_f * outer[d]
        return carry

    lax.fori_loop(0, n_chunks, scan_body, 0)

    o = o_s[0] + o_s[1]
    o = o * lax.rsqrt(_seg_sum(o * o, ones_bd) * (1.0 / HEAD_DIM) + EPS) * ng_ref[...]
    y_ref[0] = (o * _silu(p_ref[0, :, 3 * GROUP_W:4 * GROUP_W])).astype(BF16)
    for d in range(2):
        sout_ref[0, d] = _dot_sel(st_s[d], t4t_ref[...])


def _compact_row(m, kind):
    row = jnp.zeros((1, LANES), F32)
    return row.at[0, kind * 2 * N_HEADS:(kind + 1) * 2 * N_HEADS].set(m.astype(F32).reshape(2 * N_HEADS))


N_GATE_LANES = 4 * N_HEADS


def _gate_expander():
    sel = np.zeros((LANES, 4 * GROUP_W), np.float32)
    for piece in range(3):
        for kind in range(2):
            for d in range(2):
                for h in range(N_HEADS):
                    c0 = (kind * 2 + d) * GROUP_W + h * HEAD_DIM
                    sel[piece * N_GATE_LANES + kind * 2 * N_HEADS + d * N_HEADS + h, c0:c0 + HEAD_DIM] = 1.0
    return jnp.asarray(sel, BF16)


def _expand_gates(g, sel):
    p1 = g.astype(BF16).astype(F32)
    r1 = g - p1
    p2 = r1.astype(BF16).astype(F32)
    packed = p1 + pltpu.roll(p2, N_GATE_LANES, axis=1) + pltpu.roll(r1 - p2, 2 * N_GATE_LANES, axis=1)
    return _dot(packed.astype(BF16), sel)


def _tile4():
    t = np.concatenate([np.eye(HEAD_DIM, dtype=np.float32)] * N_HEADS, axis=1)
    return jnp.asarray(t, BF16), jnp.asarray(t.T, BF16)


def _seq_spec(seq_len, width):
    return pl.BlockSpec((1, seq_len, width), lambda b: (b, 0, 0))


def _full_spec(shape):
    nd = len(shape)
    return pl.BlockSpec(shape, lambda b: (0,) * nd)


def _state_spec():
    return pl.BlockSpec((1, 2, GROUP_W, HEAD_DIM), lambda b: (b, 0, 0, 0))


def _gdn(slab, conv_w, a_log, dt_bias, norm_g, s0, seq_len):
    assert seq_len % (CHUNK * GDN_GROUP) == 0
    bsz = slab.shape[0]
    t4, t4t = _tile4()
    return pl.pallas_call(
        functools.partial(_gdn_kernel, seq_len=seq_len),
        grid=(bsz,),
        in_specs=[_seq_spec(seq_len, W_SLAB_A), _full_spec((GDN_CONV, 3 * GROUP_W)), _full_spec((1, LANES)),
                  _full_spec((1, LANES)), _full_spec((1, GROUP_W)), _full_spec((LANES, 4 * GROUP_W)),
                  _full_spec((HEAD_DIM, GROUP_W)), _full_spec((GROUP_W, HEAD_DIM)), _state_spec()],
        out_specs=[_seq_spec(seq_len, GROUP_W), _state_spec()],
        out_shape=[jax.ShapeDtypeStruct((bsz, seq_len, GROUP_W), BF16),
                   jax.ShapeDtypeStruct((bsz, 2, GROUP_W, HEAD_DIM), F32)],
        scratch_shapes=[pltpu.VMEM((seq_len, GROUP_W), F32)] * 3
                       + [pltpu.VMEM((2, seq_len, GROUP_W), F32)] * 4
                       + [pltpu.VMEM((2, seq_len, GROUP_W), BF16)] * 4
                       + [pltpu.VMEM((2, GROUP_W, GROUP_W), F32),
                          pltpu.VMEM((seq_len + 2 * CONV_PAD, 3 * GROUP_W), F32)],
        compiler_params=pltpu.CompilerParams(dimension_semantics=("parallel",), vmem_limit_bytes=VMEM_LIMIT),
        name="gdn",
    )(slab, conv_w, _compact_row(a_log, 0), _compact_row(dt_bias, 0), jnp.tile(norm_g, N_HEADS)[None, :],
      _gate_expander(), t4, t4t, s0)


def _mlstm_kernel(p_ref, fb_ref, ng_ref, eg_ref, t4_ref, t4t_ref, c0_ref, n0_ref, m0_ref,
                  y_ref, cout_ref, nout_ref, mout_ref,
                  k_s, bc_s, ig_s, h_s, c_s, n_s, m_s, *, seq_len):
    n_chunks = seq_len // CHUNK
    bd_mask = _head_ones()
    bd_f = jnp.where(bd_mask, 1.0, 0.0)
    bd_b = bd_f.astype(BF16)
    ones_bd = bd_b
    lane_head = _iota((CHUNK, GROUP_W), 1) // HEAD_DIM

    k_s[...] = p_ref[0, :, GROUP_W:2 * GROUP_W] * (HEAD_DIM ** -0.5)
    raw = p_ref[0, :, 4 * GROUP_W:4 * GROUP_W + LANES]
    lane = _iota(raw.shape, 1)
    logf = -_softplus(-(raw + fb_ref[...]))
    compact = jnp.where(lane < 2 * N_HEADS, raw,
                        jnp.where(lane < N_GATE_LANES, _dir_cumsum(logf, CHUNK, lane), 0.0))
    gates = _expand_gates(compact, eg_ref[...])
    for d in range(2):
        ig_s[d] = gates[:, d * GROUP_W:(d + 1) * GROUP_W]
        bc_s[d] = gates[:, (2 + d) * GROUP_W:(3 + d) * GROUP_W]
        c_s[d] = jnp.where(bd_mask, _dot_sel(c0_ref[0, d], t4_ref[...]), 0.0)
        n_s[d] = n0_ref[0, d]
        m_s[d] = m0_ref[0, d]

    masks = [_chunk_masks(False), _chunk_masks(True)]
    ones_cc = jnp.ones((CHUNK, CHUNK), BF16)

    def seg_max(dm):
        dmax = jnp.zeros((CHUNK, GROUP_W), F32)
        for h in range(N_HEADS):
            in_head = lane_head == h
            mh = jnp.max(jnp.where(in_head, dm, -jnp.inf), axis=1, keepdims=True)
            dmax = jnp.where(in_head, mh, dmax)
        return dmax

    def body(n, carry):
        dirs = range(2)
        last = [CHUNK - 1, 0]
        rows = [pl.ds(pl.multiple_of(n * CHUNK, CHUNK), CHUNK),
                pl.ds(pl.multiple_of((n_chunks - 1 - n) * CHUNK, CHUNK), CHUNK)]
        q_c = [p_ref[0, r, 0:GROUP_W] for r in rows]
        k_c = [k_s[r, :] for r in rows]
        v_b = [p_ref[0, r, 2 * GROUP_W:3 * GROUP_W].astype(BF16) for r in rows]
        bc = [bc_s[d, rows[d], :] for d in dirs]
        x = [ig_s[d, rows[d], :] - bc[d] for d in dirs]
        x_row = [_sel_dot(ones_cc, jnp.where(masks[d][2], x[d], 0.0)) for d in dirs]
        dm = [jnp.where(masks[d][0], bc[d] + x_row[d], -jnp.inf) for d in dirs]
        dmax = [seg_max(dm[d]) for d in dirs]
        a = [bc[d] + m_s[d] for d in dirs]
        m_t = [jnp.maximum(a[d], dmax[d]) for d in dirs]
        inter = [jnp.exp(a[d] - m_t[d]) for d in dirs]
        q_b = [q.astype(BF16) for q in q_c]
        qk = [_dot_nt(q_b[d], _blockdiag_mul(k_c[d].astype(BF16), bd_b)) for d in dirs]
        w = [(jnp.exp(dm[d] - m_t[d]) * qk[d]).astype(BF16) for d in dirs]
        c_bd = [c_s[d] for d in dirs]
        n_row = [n_s[d] for d in dirs]
        q_cm = [_dot(q_b[d], c_bd[d].astype(BF16)) for d in dirs]
        q_n = [_seg_sum(q_c[d] * n_row[d], ones_bd) for d in dirs]
        wv = [_dot(w[d], jnp.concatenate([_blockdiag_mul(v_b[d], bd_b), ones_bd], axis=1)) for d in dirs]
        m_new = [m_t[d][last[d]:last[d] + 1, :] for d in dirs]
        carry_w = [jnp.exp(a[d][last[d]:last[d] + 1, :] - m_new[d]) for d in dirs]
        kw = [k_c[d] * jnp.exp(bc[d][last[d]:last[d] + 1, :] + x[d] - m_new[d]) for d in dirs]
        outer = [_dot_tn(kw[d].astype(BF16), v_b[d]) for d in dirs]
        for d in dirs:
            num = inter[d] * q_cm[d] + wv[d][:, 0:GROUP_W]
            den = inter[d] * q_n[d] + wv[d][:, GROUP_W:2 * GROUP_W]
            h_s[d, rows[d], :] = num / jnp.maximum(jnp.abs(den), jnp.exp(-m_t[d]))
            c_s[d] = carry_w[d] * c_bd[d] + bd_f * outer[d]
            n_s[d] = carry_w[d] * n_row[d] + jnp.sum(kw[d], axis=0, keepdims=True)
            m_s[d] = m_new[d]
        return carry

    lax.fori_loop(0, n_chunks, body, 0)

    h = h_s[0] + h_s[1]
    h = h * lax.rsqrt(_seg_sum(h * h, ones_bd) * (1.0 / HEAD_DIM) + EPS) * ng_ref[...]
    y_ref[0] = (h * _sigmoid(p_ref[0, :, 3 * GROUP_W:4 * GROUP_W])).astype(BF16)
    for d in range(2):
        cout_ref[0, d] = _dot_sel(c_s[d], t4t_ref[...])
        nout_ref[0, d] = n_s[d]
        mout_ref[0, d] = m_s[d]


def _row_state_spec():
    return pl.BlockSpec((1, 2, 1, GROUP_W), lambda b: (b, 0, 0, 0))


def _mlstm(slab, f_bias, norm_g, c0, n0, m0, seq_len):
    bsz = slab.shape[0]
    t4, t4t = _tile4()
    return pl.pallas_call(
        functools.partial(_mlstm_kernel, seq_len=seq_len),
        grid=(bsz,),
        in_specs=[_seq_spec(seq_len, W_SLAB_C), _full_spec((1, LANES)), _full_spec((1, GROUP_W)),
                  _full_spec((LANES, 4 * GROUP_W)), _full_spec((HEAD_DIM, GROUP_W)),
                  _full_spec((GROUP_W, HEAD_DIM)), _state_spec(), _row_state_spec(), _row_state_spec()],
        out_specs=[_seq_spec(seq_len, GROUP_W), _state_spec(), _row_state_spec(), _row_state_spec()],
        out_shape=[jax.ShapeDtypeStruct((bsz, seq_len, GROUP_W), BF16),
                   jax.ShapeDtypeStruct((bsz, 2, GROUP_W, HEAD_DIM), F32),
                   jax.ShapeDtypeStruct((bsz, 2, 1, GROUP_W), F32),
                   jax.ShapeDtypeStruct((bsz, 2, 1, GROUP_W), F32)],
        scratch_shapes=[pltpu.VMEM((seq_len, GROUP_W), F32), pltpu.VMEM((2, seq_len, GROUP_W), F32),
                        pltpu.VMEM((2, seq_len, GROUP_W), F32), pltpu.VMEM((2, seq_len, GROUP_W), F32),
                        pltpu.VMEM((2, GROUP_W, GROUP_W), F32), pltpu.VMEM((2, 1, GROUP_W), F32),
                        pltpu.VMEM((2, 1, GROUP_W), F32)],
        compiler_params=pltpu.CompilerParams(dimension_semantics=("parallel",), vmem_limit_bytes=VMEM_LIMIT),
        name="mlstm",
    )(slab, _compact_row(f_bias, 1), jnp.tile(norm_g, N_HEADS)[None, :], _gate_expander(), t4, t4t, c0, n0, m0)


def _hgrn_kernel(p_ref, lb_ref, ng_ref, t4_ref, t4t_ref, s0_ref, y_ref, sout_ref,
                 k_s, gc_s, o_s, st_s, *scan_order_s, seq_len, grid_rows):
    c = HGRN_CHUNK
    n_chunks = seq_len // c
    if grid_rows is None:
        src = p_ref.at[0]
    else:
        (src,) = scan_order_s
        src[...] = _swap_row_groups(p_ref[0], grid_rows, GRID_W)
    bd_mask = _head_ones()
    bd_f = jnp.where(bd_mask, 1.0, 0.0)
    ones_bd = bd_f.astype(BF16)
    sub = _iota((c, GROUP_W), 0)
    row_sum = jnp.where(_iota((c, c * c), 1) // c == _iota((c, c * c), 0), 1.0, 0.0).astype(BF16)

    for d in range(2):
        lb = lb_ref[d:d + 1, :]
        forget = lb + (1.0 - lb) * _sigmoid(src[:, (3 + d) * GROUP_W:(4 + d) * GROUP_W])
        k_s[d] = 1.0 - forget
        gc_s[d] = _seg_cumsum(jnp.log(forget), c, reverse=(d == 1))
        s_bd = jnp.where(bd_mask, _dot_sel(s0_ref[0, d], t4_ref[...]), 0.0)
        st_s[d] = s_bd.T

    def pair_weights(d, q_c, k_c, gc):
        gc2 = gc * LOG2_E
        pieces = []
        for i in range(c):
            keep = (sub >= i) if d == 1 else (sub <= i)
            e_i = jnp.where(keep, jnp.exp2(gc2[i:i + 1, :] - gc2), 0.0)
            pieces.append(e_i * q_c[i:i + 1, :] * k_c)
        return jnp.concatenate(pieces, axis=0).astype(BF16)

    def body(n, carry):
        ds = [0, 1, 0, 1]
        starts = [2 * n, n_chunks - 1 - 2 * n, 2 * n + 1, n_chunks - 2 - 2 * n]
        rows = [pl.ds(pl.multiple_of(s * c, c), c) for s in starts]
        q_c = [src[r, 0:GROUP_W] for r in rows]
        v_c = [src[r, GROUP_W:2 * GROUP_W] for r in rows]
        k_c = [k_s[d, r, :] for d, r in zip(ds, rows)]
        gc = [gc_s[d, r, :] for d, r in zip(ds, rows)]
        pairs =[pair_weights(d, q, k, g) for d, q, k, g in zip(ds, q_c, k_c, gc)]
        attn = [_dot(p, ones_bd) for p in pairs]
        weighted = [(a * jnp.concatenate([v] * c, axis=0)).astype(BF16) for a, v in zip(attn, v_c)]
        intra = [_dot(row_sum, w) for w in weighted]
        q_dec = [(q * jnp.exp(g)).astype(BF16) for q, g in zip(q_c, gc)]
        gc_last = [g[(0 if d == 1 else c - 1):(1 if d == 1 else c), :] for d, g in zip(ds, gc)]
        outer = [_dot_tn(v.astype(BF16), (k * jnp.exp(gl - g)).astype(BF16))
                 for v, k, gl, g in zip(v_c, k_c, gc_last, gc)]
        for i in range(4):
            d = ds[i]
            st = st_s[d]
            o_s[d, rows[i], :] = intra[i] + _dot_nt(q_dec[i], st.astype(BF16))
            st_s[d] = st * jnp.exp(gc_last[i]) + bd_f * outer[i]
        return carry

    lax.fori_loop(0, n_chunks // 2, body, 0)

    o = o_s[0] + o_s[1]
    o = o * lax.rsqrt(_seg_sum(o * o, ones_bd) * (1.0 / HEAD_DIM) + EPS) * ng_ref[...]
    y = o * _sigmoid(src[:, 2 * GROUP_W:3 * GROUP_W])
    if grid_rows is None:
        y_ref[0] = y.astype(BF16)
    else:
        y_ref[0] = _swap_row_groups(y, GRID_W, grid_rows).astype(BF16)
    for d in range(2):
        sout_ref[0, d] = _dot_sel(st_s[d].T, t4t_ref[...])


def _hgrn(slab, lb, norm_g, s0, seq_len, grid_rows):
    bsz = slab.shape[0]
    t4, t4t = _tile4()
    scan_order = [] if grid_rows is None else [pltpu.VMEM((seq_len, W_SLAB_B), F32)]
    return pl.pallas_call(
        functools.partial(_hgrn_kernel, seq_len=seq_len, grid_rows=grid_rows),
        grid=(bsz,),
        in_specs=[_seq_spec(seq_len, W_SLAB_B), _full_spec((2, GROUP_W)), _full_spec((1, GROUP_W)),
                  _full_spec((HEAD_DIM, GROUP_W)), _full_spec((GROUP_W, HEAD_DIM)), _state_spec()],
        out_specs=[_seq_spec(seq_len, GROUP_W), _state_spec()],
        out_shape=[jax.ShapeDtypeStruct((bsz, seq_len, GROUP_W), BF16),
                   jax.ShapeDtypeStruct((bsz, 2, GROUP_W, HEAD_DIM), F32)],
        scratch_shapes=[pltpu.VMEM((2, seq_len, GROUP_W), F32), pltpu.VMEM((2, seq_len, GROUP_W), F32),
                        pltpu.VMEM((2, seq_len, GROUP_W), F32), pltpu.VMEM((2, GROUP_W, GROUP_W), F32)] + scan_order,
        compiler_params=pltpu.CompilerParams(dimension_semantics=("parallel",), vmem_limit_bytes=VMEM_LIMIT),
        name="hgrn",
    )(slab, lb.astype(F32), jnp.tile(norm_g, N_HEADS)[None, :], t4, t4t, s0)


def _dft_tables(n):
    k = np.arange(n, dtype=np.float64)[:, None]
    s = np.arange(n, dtype=np.float64)[None, :]
    ang = np.pi * k * s / n
    cos_m = np.cos(ang)
    sin_m = np.sin(ang)
    alt = (-1.0) ** np.arange(n)
    sin_m[0, :] = alt
    fwd = np.concatenate([cos_m, sin_m], axis=0)
    wgt = np.full((n,), 2.0 / (2 * n))
    wgt[0] = 1.0 / (2 * n)
    inv = np.concatenate([cos_m.T * wgt[None, :], sin_m.T * wgt[None, :]], axis=1)
    out = []
    for m in (fwd, inv):
        m32 = m.astype(np.float32)
        hi = m32.astype(BF16)
        lo = (m32 - hi.astype(np.float32)).astype(BF16)
        out.append(jnp.asarray(np.concatenate([hi, lo], axis=1)))
    return out


def _dft_apply(m_ref, x):
    xb = x.astype(BF16)
    return _dot(m_ref[...], jnp.concatenate([xb, xb], axis=0))


def _dft_apply_f32(m_ref, x):
    k = x.shape[0]
    xh, xl = _split2(x)
    return _dot(m_ref[...], jnp.concatenate([xh, xh], axis=0)) + _dot(m_ref[:, 0:k], xl)


def _filter_kernel(feat_ref, env_ref, w1_ref, b1_ref, w2_ref, b2_ref, w3_ref, freq_ref, f_ref,
                   hr_ref, hs_ref, *, seq_len):
    freq = freq_ref[...]
    hid = jnp.sin(freq * (_dot3(feat_ref[...], w1_ref[...]) + b1_ref[...]))
    hid = jnp.sin(freq * (_dot3(hid, w2_ref[...]) + b2_ref[...]))
    h = _dot3(hid, w3_ref[...])
    env = env_ref[...]
    not_first = _iota((seq_len, GROUP_W), 0) > 0
    first_row = _iota((seq_len, GROUP_W), 0) == 0
    sides = []
    for o in range(HYENA_ORDER):
        pos = h[:, (2 * o) * GROUP_W:(2 * o + 1) * GROUP_W] * env
        neg = jnp.where(not_first, h[:, (2 * o + 1) * GROUP_W:(2 * o + 2) * GROUP_W] * env, 0.0)
        ss = jnp.sum(pos * pos + neg * neg, axis=0, keepdims=True)
        scale = lax.rsqrt(ss + EPS)
        sides += [(pos + neg) * scale, (pos - neg) * scale]
    spec = _dft_apply_f32(f_ref, jnp.concatenate(sides, axis=1))
    for o in range(HYENA_ORDER):
        fa = spec[:, (2 * o) * GROUP_W:(2 * o + 1) * GROUP_W]
        fb = spec[:, (2 * o + 1) * GROUP_W:(2 * o + 2) * GROUP_W]
        hr_ref[o] = fa[0:seq_len]
        hs_ref[o] = jnp.where(first_row, fa[seq_len:2 * seq_len], fb[seq_len:2 * seq_len])


def _hyena_filters(seq_len, w1, b1, w2, b2, w3, freq, tables):
    t = np.arange(seq_len, dtype=np.float64)
    tn = np.linspace(0.0, 1.0, seq_len)
    bands = np.linspace(1e-4, POS_BANDS - 1, POS_BANDS)
    ang = (2.0 * math.pi / seq_len) * t[:, None] * bands[None, :]
    feats = np.zeros((seq_len, LANES), np.float32)
    feats[:, 0:POS_DIM] = np.concatenate([tn[:, None], np.cos(ang), -np.sin(ang)], axis=-1)
    deltas = np.abs(np.linspace(math.log(1e-2) / 1.5, math.log(1e-2) / 0.3, GROUP_W))
    env = np.exp(-tn[:, None] * deltas[None, :]).astype(np.float32)
    w1p = jnp.zeros((LANES, FILTER_HIDDEN), F32).at[0:POS_DIM].set(w1)
    n_h = HYENA_ORDER * 2 * GROUP_W
    full = lambda shape: pl.BlockSpec(shape, lambda: (0,) * len(shape))
    return pl.pallas_call(
        functools.partial(_filter_kernel, seq_len=seq_len),
        in_specs=[full((seq_len, LANES)), full((seq_len, GROUP_W)), full((LANES, FILTER_HIDDEN)),
                  full((1, FILTER_HIDDEN)), full((FILTER_HIDDEN, FILTER_HIDDEN)), full((1, FILTER_HIDDEN)),
                  full((FILTER_HIDDEN, n_h)), full((1, FILTER_HIDDEN)),
                  full((2 * seq_len, 2 * seq_len))],
        out_specs=[full((HYENA_ORDER, seq_len, GROUP_W)), full((HYENA_ORDER, seq_len, GROUP_W))],
        out_shape=[jax.ShapeDtypeStruct((HYENA_ORDER, seq_len, GROUP_W), F32)] * 2,
        compiler_params=pltpu.CompilerParams(vmem_limit_bytes=VMEM_LIMIT),
        name="hyena_filter",
    )(jnp.asarray(feats), jnp.asarray(env), w1p, b1[None, :], w2, b2[None, :], w3, freq[None, :], tables[0])


def _hyena_kernel(p_ref, conv_ref, hr_ref, hs_ref, db_ref, f_ref, i_ref, y_ref, pad_s, *, seq_len):
    us = [_short_conv(pad_s, p_ref[b], conv_ref, HYENA_CONV, seq_len) for b in range(HYENA_SEQS)]
    part = lambda k: jnp.concatenate([u[:, k * GROUP_W:(k + 1) * GROUP_W] for u in us], axis=1)
    wide = lambda t: jnp.concatenate([t] * HYENA_SEQS, axis=1)
    first_row = _iota((seq_len, HYENA_SEQS * GROUP_W), 0) == 0

    def longconv(z, o):
        zf = _dft_apply(f_ref, z)
        zr = zf[0:seq_len]
        zs = zf[seq_len:2 * seq_len]
        hr = wide(hr_ref[o])
        hs = wide(hs_ref[o])
        yr = zr * hr - jnp.where(first_row, 0.0, zs * hs)
        ys = jnp.where(first_row, zs * hs, zr * hs + zs * hr)
        y = _dft_apply(i_ref, jnp.concatenate([yr, ys], axis=0))
        return y + wide(db_ref[o:o + 1, :]) * z

    y = part(2) * longconv(part(1) * longconv(part(0), 0), 1)
    for b in range(HYENA_SEQS):
        y_ref[b] = y[:, b * GROUP_W:(b + 1) * GROUP_W].astype(BF16)


def _hyena(slab, conv_w, hr, hs, dbias, tables, seq_len):
    bsz = slab.shape[0]
    once = lambda shape: pl.BlockSpec(shape, lambda b: (0,) * len(shape), pipeline_mode=pl.Buffered(1))
    return pl.pallas_call(
        functools.partial(_hyena_kernel, seq_len=seq_len),
        grid=(bsz // HYENA_SEQS,),
        in_specs=[pl.BlockSpec((HYENA_SEQS, seq_len, W_SLAB_D), lambda b: (b, 0, 0), pipeline_mode=pl.Buffered(1)),
                  _full_spec((HYENA_CONV, 3 * GROUP_W)),
                  once((HYENA_ORDER, seq_len, GROUP_W)), once((HYENA_ORDER, seq_len, GROUP_W)),
                  _full_spec((HYENA_ORDER, GROUP_W)),
                  once((2 * seq_len, 2 * seq_len)), once((seq_len, 4 * seq_len))],
        out_specs=pl.BlockSpec((HYENA_SEQS, seq_len, GROUP_W), lambda b: (b, 0, 0)),
        out_shape=jax.ShapeDtypeStruct((bsz, seq_len, GROUP_W), BF16),
        scratch_shapes=[pltpu.VMEM((seq_len + 2 * CONV_PAD, 3 * GROUP_W), F32)],
        compiler_params=pltpu.CompilerParams(dimension_semantics=("parallel",), vmem_limit_bytes=VMEM_LIMIT),
        name="hyena",
    )(slab, conv_w, hr, hs, dbias, *tables)


def _outproj_kernel(x_ref, ya_ref, yb_ref, yc_ref, yd_ref, wo_ref, mod_ref, g_ref, rwh_ref, rwl_ref, rb_ref,
                    x1_ref, h2_ref, gates_ref):
    rows_per_part = x_ref.shape[0] // OUTPROJ_PARTS
    parts = [slice(i * rows_per_part, (i + 1) * rows_per_part) for i in range(OUTPROJ_PARTS)]
    y = [(_dot(ya_ref[p, :], wo_ref[0:GROUP_W, :]) + _dot(yb_ref[p, :], wo_ref[GROUP_W:2 * GROUP_W, :])
          + _dot(yc_ref[p, :], wo_ref[2 * GROUP_W:3 * GROUP_W, :])
          + _dot(yd_ref[p, :], wo_ref[3 * GROUP_W:4 * GROUP_W, :])) for p in parts]
    x1 = [x_ref[p, :] + mod_ref[0, 2:3, :] * yi for p, yi in zip(parts, y)]
    for p, v in zip(parts, x1):
        x1_ref[p, :] = v
    split = [_split2(_modulated_norm(v, g_ref[...], mod_ref[0, 3:4, :], mod_ref[0, 4:5, :])) for v in x1]
    for p, (hh, _) in zip(parts, split):
        h2_ref[p, :] = hh
    logits = [_dot(hh, rwh_ref[...]) + _dot(hl, rwh_ref[...]) + _dot(hh, rwl_ref[...]) for hh, hl in split]
    scores = [_sigmoid(v) for v in logits]
    vals = [s + rb_ref[...] for s in scores]
    lane = _iota(vals[0].shape, 1).astype(F32)
    picked = [jnp.zeros(v.shape, F32) for v in vals]
    for _ in range(TOP_K):
        best = [jnp.max(v, axis=-1, keepdims=True) for v in vals]
        first = [jnp.min(jnp.where(v == b, lane, float(LANES)), axis=-1, keepdims=True) for v, b in zip(vals, best)]
        hit = [lane == f for f in first]
        picked = [jnp.where(h, s, pk) for h, s, pk in zip(hit, scores, picked)]
        vals = [jnp.where(h, -jnp.inf, v) for h, v in zip(hit, vals)]
    for p, pk in zip(parts, picked):
        gates_ref[p, :] = pk / jnp.sum(pk, axis=-1, keepdims=True) * ROUTED_SCALE


def _outproj(x, ys, w_out, mod, g, router_w, router_bias, seq_len, tm):
    t = x.shape[0]
    per_seq = seq_len // tm if mod.shape[0] > 1 else None

    def mod_idx(i):
        return (i // per_seq if per_seq else 0, 0, 0)

    rw = jnp.zeros((D_MODEL, LANES), F32).at[:, 0:N_EXPERTS].set(router_w)
    rwh = rw.astype(BF16)
    rwl = (rw - rwh.astype(F32)).astype(BF16)
    rb = jnp.full((1, LANES), -jnp.inf, F32).at[0, 0:N_EXPERTS].set(router_bias.astype(F32))
    row = lambda w: pl.BlockSpec((tm, w), lambda i: (i, 0))
    fix = lambda shape: pl.BlockSpec(shape, lambda i: (0,) * len(shape))
    return pl.pallas_call(
        _outproj_kernel,
        grid=(t // tm,),
        in_specs=[row(D_MODEL), row(GROUP_W), row(GROUP_W), row(GROUP_W), row(GROUP_W),
                  fix((D_MODEL, D_MODEL)), pl.BlockSpec((1, 6, D_MODEL), mod_idx), fix((1, D_MODEL)),
                  fix((D_MODEL, LANES)), fix((D_MODEL, LANES)), fix((1, LANES))],
        out_specs=[row(D_MODEL), row(D_MODEL), row(LANES)],
        out_shape=[jax.ShapeDtypeStruct((t, D_MODEL), F32), jax.ShapeDtypeStruct((t, D_MODEL), BF16),
                   jax.ShapeDtypeStruct((t, LANES), F32)],
        compiler_params=pltpu.CompilerParams(dimension_semantics=("parallel",), vmem_limit_bytes=VMEM_LIMIT),
        name="outproj",
    )(x, *ys, w_out.astype(BF16), mod, g, rwh, rwl, rb)


def _moe_kernel(h_ref, gates_ref, x1_ref, mod_ref, wg_ref, wu_ref, wd_ref, sg_ref, su_ref, sd_ref, fg_ref,
                o_ref, acc_ref, *, final_norm):
    j = pl.program_id(1)
    h = h_ref[...]
    width = EXPERTS_PER_STEP * D_EXPERT

    @pl.when(j == 0)
    def _():
        sh = _silu(_dot(h, sg_ref[...].astype(BF16))) * _dot(h, su_ref[...].astype(BF16))
        acc_ref[...] = _dot(sh.astype(BF16), sd_ref[...].astype(BF16))

    w_gu = jnp.concatenate([wg_ref[e] for e in range(EXPERTS_PER_STEP)]
                           + [wu_ref[e] for e in range(EXPERTS_PER_STEP)], axis=1).astype(BF16)
    gu = _dot(h, w_gu)
    mine = pltpu.roll(gates_ref[...], (LANES - EXPERTS_PER_STEP * j) % LANES, axis=1)
    gate_w = jnp.concatenate([jnp.broadcast_to(mine[:, e:e + 1], (mine.shape[0], D_EXPERT))
                              for e in range(EXPERTS_PER_STEP)], axis=1)
    hid = (_silu(gu[:, 0:width]) * gu[:, width:2 * width]) * gate_w
    w_dn = jnp.concatenate([wd_ref[e] for e in range(EXPERTS_PER_STEP)], axis=0).astype(BF16)
    acc_ref[...] += _dot(hid.astype(BF16), w_dn)

    @pl.when(j == pl.num_programs(1) - 1)
    def _():
        out = x1_ref[...] + mod_ref[0, 5:6, :] * acc_ref[...]
        if final_norm:
            ms = jnp.mean(out * out, axis=-1, keepdims=True)
            out = out * lax.rsqrt(ms + EPS) * fg_ref[...]
        o_ref[...] = out


def _moe(h2, gates, x1, mod, layer, w_gate, w_up, w_down, sh_gate, sh_up, sh_down, final_g, seq_len, tm,
         final_norm):
    t = h2.shape[0]
    per_seq = max(seq_len // tm, 1) if mod.shape[0] > 1 else None
    seqs_per_tile = max(tm // seq_len, 1)
    assert mod.shape[0] == 1 or seqs_per_tile == 1

    def mod_idx(i, j):
        return (i // per_seq if per_seq else 0, 0, 0)

    row = lambda w: pl.BlockSpec((tm, w), lambda i, j: (i, 0))
    fix = lambda shape: pl.BlockSpec(shape, lambda i, j: (0,) * len(shape))
    eps_ = EXPERTS_PER_STEP
    return pl.pallas_call(
        functools.partial(_moe_kernel, final_norm=final_norm),
        grid=(t // tm, N_EXPERTS // eps_),
        in_specs=[row(D_MODEL), row(LANES), row(D_MODEL), pl.BlockSpec((1, 6, D_MODEL), mod_idx),
                  pl.BlockSpec((None, eps_, D_MODEL, D_EXPERT), lambda i, j: (layer, j, 0, 0)),
                  pl.BlockSpec((None, eps_, D_MODEL, D_EXPERT), lambda i, j: (layer, j, 0, 0)),
                  pl.BlockSpec((None, eps_, D_EXPERT, D_MODEL), lambda i, j: (layer, j, 0, 0)),
                  fix((D_MODEL, D_EXPERT)), fix((D_MODEL, D_EXPERT)), fix((D_EXPERT, D_MODEL)),
                  fix((1, D_MODEL))],
        out_specs=row(D_MODEL),
        out_shape=jax.ShapeDtypeStruct((t, D_MODEL), F32),
        scratch_shapes=[pltpu.VMEM((tm, D_MODEL), F32)],
        compiler_params=pltpu.CompilerParams(dimension_semantics=("parallel", "arbitrary"),
                                             vmem_limit_bytes=VMEM_LIMIT),
        name="moe",
    )(h2, gates, x1, mod, w_gate, w_up, w_down, sh_gate, sh_up, sh_down, final_g)


def _in_weight_slabs(w_in):
    ga = 4 * GROUP_W
    gb = ga + 4 * N_HEADS
    hb = gb + 5 * GROUP_W
    mc = hb + 4 * GROUP_W
    md = mc + 4 * N_HEADS
    pad = jnp.zeros((D_MODEL, LANES - 4 * N_HEADS), w_in.dtype)
    wa = jnp.concatenate([w_in[:, 0:gb], pad], axis=1)
    wb = w_in[:, gb:hb]
    wc = jnp.concatenate([w_in[:, hb:md], pad], axis=1)
    wd = w_in[:, md:]
    return [w.astype(BF16) for w in (wa, wb, wc, wd)]


def _layer(x, mod, states, prm, stacked, layer, lb, seq_len, grid_rows, tables, final_g, final_norm):
    bsz = x.shape[0] // seq_len
    tm = 512
    s_gdn, s_hgrn, s_c, s_n, s_m = states
    pa, pb, pc, pd = _inproj(x, mod, prm['norm1_g'][None, :], _in_weight_slabs(prm['w_in']), seq_len, tm)
    pa = pa.reshape(bsz, seq_len, W_SLAB_A)
    pb = pb.reshape(bsz, seq_len, W_SLAB_B)
    pc = pc.reshape(bsz, seq_len, W_SLAB_C)
    pd = pd.reshape(bsz, seq_len, W_SLAB_D)

    ya, st_a = _gdn(pa, prm['gdn_conv'], prm['gdn_a_log'], prm['gdn_dt_bias'], prm['gdn_norm_g'],
                    s_gdn.reshape(bsz, 2, GROUP_W, HEAD_DIM), seq_len)
    yb, st_b = _hgrn(pb, lb, prm['hgrn_norm_g'], s_hgrn.reshape(bsz, 2, GROUP_W, HEAD_DIM), seq_len, grid_rows)
    yc, st_c, st_n, st_m = _mlstm(pc, prm['mlstm_f_bias'], prm['mlstm_norm_g'],
                                  s_c.reshape(bsz, 2, GROUP_W, HEAD_DIM), s_n.reshape(bsz, 2, 1, GROUP_W),
                                  jnp.repeat(s_m, HEAD_DIM, axis=-1).reshape(bsz, 2, 1, GROUP_W), seq_len)
    hr, hs = _hyena_filters(seq_len, prm['filt_w1'], prm['filt_b1'], prm['filt_w2'], prm['filt_b2'],
                            prm['filt_w3'], prm['filt_freq'], tables)
    yd = _hyena(pd, prm['hyena_conv'], hr, hs, prm['hyena_d'], tables, seq_len)

    t = x.shape[0]
    ys = [y.reshape(t, GROUP_W) for y in (ya, yb, yc, yd)]
    x1, h2, gates = _outproj(x, ys, prm['w_out'], mod, prm['norm2_g'][None, :], prm['router_w'],
                             prm['router_bias'], seq_len, tm)
    tm_moe = 1024
    out = _moe(h2, gates, x1, mod, layer, stacked['exp_w_gate'], stacked['exp_w_up'], stacked['exp_w_down'],
               prm['sh_w_gate'], prm['sh_w_up'], prm['sh_w_down'], final_g[None, :], seq_len, tm_moe, final_norm)
    new_states = (st_a.reshape(bsz, 2, N_HEADS, HEAD_DIM, HEAD_DIM),
                  st_b.reshape(bsz, 2, N_HEADS, HEAD_DIM, HEAD_DIM),
                  st_c.reshape(bsz, 2, N_HEADS, HEAD_DIM, HEAD_DIM),
                  st_n.reshape(bsz, 2, N_HEADS, HEAD_DIM),
                  st_m.reshape(bsz, 2, N_HEADS, HEAD_DIM)[..., 0])
    return out, new_states


def kernel(x_prompt, x_sample, c, state_gdn, state_hgrn, state_mlstm_c, state_mlstm_n, state_mlstm_m, c_ctx, norm1_g, norm2_g, w_mod, b_mod, w_in, gdn_conv, gdn_a_log, gdn_dt_bias, gdn_norm_g, hgrn_lb_logits, hgrn_norm_g, mlstm_f_bias, mlstm_norm_g, hyena_conv, filt_w1, filt_b1, filt_w2, filt_b2, filt_w3, filt_freq, hyena_d, w_out, router_w, router_bias, exp_w_gate, exp_w_up, exp_w_down, sh_w_gate, sh_w_up, sh_w_down, final_g):
    depth = w_in.shape[0]
    bsz, seq, _ = x_prompt.shape
    dbsz, dseq, _ = x_sample.shape
    p = jax.nn.softmax(hgrn_lb_logits.astype(F32), axis=0)
    lower = jnp.cumsum(p, axis=0) - p[0:1]

    per_layer = dict(norm1_g=norm1_g, norm2_g=norm2_g, w_in=w_in, gdn_conv=gdn_conv, gdn_a_log=gdn_a_log,
                     gdn_dt_bias=gdn_dt_bias, gdn_norm_g=gdn_norm_g, hgrn_norm_g=hgrn_norm_g,
                     mlstm_f_bias=mlstm_f_bias, mlstm_norm_g=mlstm_norm_g, hyena_conv=hyena_conv,
                     filt_w1=filt_w1, filt_b1=filt_b1, filt_w2=filt_w2, filt_b2=filt_b2, filt_w3=filt_w3,
                     filt_freq=filt_freq, hyena_d=hyena_d, w_out=w_out, router_w=router_w,
                     router_bias=router_bias, sh_w_gate=sh_w_gate, sh_w_up=sh_w_up, sh_w_down=sh_w_down)
    stacked = dict(exp_w_gate=exp_w_gate, exp_w_up=exp_w_up, exp_w_down=exp_w_down)

    cond = jnp.zeros((N_COND_ROWS, D_MODEL), F32).at[0].set(c_ctx).at[1:1 + dbsz].set(c)
    tables_p = _dft_tables(seq)
    tables_s = _dft_tables(dseq)

    zero_states = (jnp.zeros((bsz, 2, N_HEADS, HEAD_DIM, HEAD_DIM), F32),
                   jnp.zeros((bsz, 2, N_HEADS, HEAD_DIM, HEAD_DIM), F32),
                   jnp.zeros((bsz, 2, N_HEADS, HEAD_DIM, HEAD_DIM), F32),
                   jnp.zeros((bsz, 2, N_HEADS, HEAD_DIM), F32),
                   jnp.zeros((bsz, 2, N_HEADS), F32))
    h = x_prompt.reshape(bsz * seq, D_MODEL)
    z = x_sample.reshape(dbsz * dseq, D_MODEL)
    new_states = [[] for _ in range(5)]
    for l in range(depth):
        prm = {name: val[l] for name, val in per_layer.items()}
        mod = _adaln(cond, w_mod, b_mod[l], l).reshape(N_COND_ROWS, 6, D_MODEL)
        last = l == depth - 1
        h, st = _layer(h, mod[0:1], zero_states, prm, stacked, l, lower[l], seq, None, tables_p, final_g, last)
        for acc, s in zip(new_states, st):
            acc.append(s.astype(x_prompt.dtype))
        cached = (state_gdn[:, l].astype(F32), state_hgrn[:, l].astype(F32), state_mlstm_c[:, l].astype(F32),
                  state_mlstm_n[:, l].astype(F32), state_mlstm_m[:, l].astype(F32))
        z, _ = _layer(z, mod[1:1 + dbsz], cached, prm, stacked, l, lower[l], dseq, dseq // GRID_W, tables_s,
                      final_g, last)

    return (h.reshape(bsz, seq, D_MODEL), z.reshape(dbsz, dseq, D_MODEL),
            *[jnp.stack(acc, axis=1) for acc in new_states])
```

```python
import functools
import math

import numpy as np
import jax
import jax.numpy as jnp
from jax import lax
from jax.experimental import pallas as pl
from jax.experimental.pallas import tpu as pltpu

F32 = jnp.float32
BF16 = jnp.bfloat16

D_MODEL = 1024
N_HEADS = 4
HEAD_DIM = 64
GROUP_W = N_HEADS * HEAD_DIM
GRID_W = 64
CHUNK = 64
HGRN_CHUNK = 16
MLSTM_SEQS = 2
HGRN_GROUP = 2
OUTPROJ_PARTS = 2
HYENA_PREFETCH_MAX_BYTES = 4 * 1024 * 1024
HYENA_SEQS = 2
GDN_GROUP = 4
GDN_CONV = 5
HYENA_CONV = 3
HYENA_ORDER = 2
POS_BANDS = 16
POS_DIM = 1 + 2 * POS_BANDS
FILTER_HIDDEN = 64
N_EXPERTS = 64
TOP_K = 8
D_EXPERT = 128
ROUTED_SCALE = 2.5
EPS = 1e-6
LOG2_E = 1.4426950408889634
LANES = 128
CONV_PAD = 8
EXPERTS_PER_STEP = 4
N_COND_ROWS = 16
VMEM_LIMIT = 56 * 1024 * 1024

W_SLAB_A = 4 * GROUP_W + LANES
W_SLAB_B = 5 * GROUP_W
W_SLAB_C = 4 * GROUP_W + LANES
W_SLAB_D = 3 * GROUP_W


def _split2(x):
    hi = x.astype(BF16)
    lo = (x - hi.astype(F32)).astype(BF16)
    return hi, lo


def _split3(x):
    p1 = x.astype(BF16)
    r = x - p1.astype(F32)
    p2 = r.astype(BF16)
    p3 = (r - p2.astype(F32)).astype(BF16)
    return p1, p2, p3


def _dot(a, b):
    return jnp.dot(a, b, preferred_element_type=F32)


def _dot_nt(a, b):
    return lax.dot_general(a, b, (((1,), (1,)), ((), ())), preferred_element_type=F32)


def _dot_tn(a, b):
    return lax.dot_general(a, b, (((0,), (0,)), ((), ())), preferred_element_type=F32)


def _dot3(a, b):
    ah, al = _split2(a)
    bh, bl = _split2(b)
    return _dot(ah, bh) + _dot(al, bh) + _dot(ah, bl)


def _dot_sel(x, sel):
    p1, p2, p3 = _split3(x)
    return _dot(p1, sel) + _dot(p2, sel) + _dot(p3, sel)


def _sel_dot(sel, x):
    p1, p2, p3 = _split3(x)
    return _dot(sel, p1) + _dot(sel, p2) + _dot(sel, p3)


def _iota(shape, dim):
    return lax.broadcasted_iota(jnp.int32, shape, dim)


def _head_ones():
    r = _iota((GROUP_W, GROUP_W), 0) // HEAD_DIM
    c = _iota((GROUP_W, GROUP_W), 1) // HEAD_DIM
    return r == c


def _seg_sum(x, ones_bd):
    hi, lo = _split2(x)
    return _dot(hi, ones_bd) + _dot(lo, ones_bd)


def _sigmoid(x):
    return 1.0 / (1.0 + jnp.exp(-x))


def _silu(x):
    return x * _sigmoid(x)


def _softplus(x):
    return jnp.maximum(x, 0.0) + jnp.log(1.0 + jnp.exp(-jnp.abs(x)))


def _seg_cumsum(x, chunk, reverse):
    n = x.shape[0]
    pos = _iota(x.shape, 0) % chunk
    s = 1
    while s < chunk:
        if reverse:
            x = x + jnp.where(pos < chunk - s, pltpu.roll(x, n - s, axis=0), 0.0)
        else:
            x = x + jnp.where(pos >= s, pltpu.roll(x, s, axis=0), 0.0)
        s *= 2
    return x


def _swap_row_groups(x, a, b):
    n, ch = x.shape
    return jnp.swapaxes(x.reshape(a, b, ch), 0, 1).reshape(n, ch)


def _dir_cumsum(x, chunk, lane):
    backward = (lane // N_HEADS) % 2 == 1
    return jnp.where(backward, _seg_cumsum(x, chunk, True), _seg_cumsum(x, chunk, False))


def _short_conv(pad_ref, x, w_ref, width, n):
    ch = x.shape[1]
    pad_ref[0:CONV_PAD, :] = jnp.zeros((CONV_PAD, ch), F32)
    pad_ref[CONV_PAD + n:2 * CONV_PAD + n, :] = jnp.zeros((CONV_PAD, ch), F32)
    pad_ref[CONV_PAD:CONV_PAD + n, :] = x
    acc = None
    for j in range(width):
        off = CONV_PAD - width // 2 + j
        term = pad_ref[off:off + n, :] * w_ref[j:j + 1, :]
        acc = term if acc is None else acc + term
    return acc


def _chunk_masks(reverse):
    r = _iota((CHUNK, GROUP_W), 0)
    j = _iota((CHUNK, GROUP_W), 1) % HEAD_DIM
    if reverse:
        return j >= r, j > r, j == r
    return j <= r, j < r, j == r


def _adaln_kernel(c_ref, w_ref, b_ref, o_ref):
    c = c_ref[...]
    o_ref[...] = _dot3(_silu(c), w_ref[...]) + b_ref[...]


def _adaln(cond, w_mod, b_mod, layer):
    n_out = w_mod.shape[2]
    tn = 1536
    return pl.pallas_call(
        _adaln_kernel,
        grid=(n_out // tn,),
        in_specs=[pl.BlockSpec((N_COND_ROWS, D_MODEL), lambda j: (0, 0)),
                  pl.BlockSpec((None, D_MODEL, tn), lambda j: (layer, 0, j)),
                  pl.BlockSpec((1, tn), lambda j: (0, j))],
        out_specs=pl.BlockSpec((N_COND_ROWS, tn), lambda j: (0, j)),
        out_shape=jax.ShapeDtypeStruct((N_COND_ROWS, n_out), F32),
        compiler_params=pltpu.CompilerParams(dimension_semantics=("arbitrary",), vmem_limit_bytes=VMEM_LIMIT),
        name="adaln",
    )(cond, w_mod, b_mod.reshape(1, n_out))


def _modulated_norm(x, g, shift, scale):
    ms = jnp.mean(x * x, axis=-1, keepdims=True)
    return (x * lax.rsqrt(ms + EPS) * g) * (1.0 + scale) + shift


def _inproj_kernel(x_ref, mod_ref, g_ref, wa_ref, wb_ref, wc_ref, wd_ref, oa_ref, ob_ref, oc_ref, od_ref):
    h = _modulated_norm(x_ref[...], g_ref[...], mod_ref[0, 0:1, :], mod_ref[0, 1:2, :]).astype(BF16)
    oa_ref[...] = _dot(h, wa_ref[...])
    ob_ref[...] = _dot(h, wb_ref[...])
    oc_ref[...] = _dot(h, wc_ref[...])
    od_ref[...] = _dot(h, wd_ref[...])


def _inproj(x, mod, g, w_slabs, seq_len, tm):
    t = x.shape[0]
    per_seq = seq_len // tm if mod.shape[0] > 1 else None

    def mod_idx(i):
        return (i // per_seq if per_seq else 0, 0, 0)

    widths = (W_SLAB_A, W_SLAB_B, W_SLAB_C, W_SLAB_D)
    return pl.pallas_call(
        _inproj_kernel,
        grid=(t // tm,),
        in_specs=[pl.BlockSpec((tm, D_MODEL), lambda i: (i, 0)),
                  pl.BlockSpec((1, 6, D_MODEL), mod_idx),
                  pl.BlockSpec((1, D_MODEL), lambda i: (0, 0))]
                 + [pl.BlockSpec((D_MODEL, w), lambda i: (0, 0)) for w in widths],
        out_specs=[pl.BlockSpec((tm, w), lambda i: (i, 0)) for w in widths],
        out_shape=[jax.ShapeDtypeStruct((t, w), F32) for w in widths],
        compiler_params=pltpu.CompilerParams(dimension_semantics=("parallel",), vmem_limit_bytes=VMEM_LIMIT),
        name="inproj",
    )(x, mod, g, *w_slabs)


def _chunk_masks_f32(reverse):
    return tuple(jnp.where(m, 1.0, 0.0) for m in _chunk_masks(reverse))


def _blockdiag_mul(x, bd_factor):
    return jnp.concatenate([x] * N_HEADS, axis=0) * bd_factor


def _tri_inverse(lms, eyes, same16, same32, bd_b):
    def prod(xs, ys):
        staged = []
        for a, b in zip(xs, ys):
            ah, al = _split2(a)
            staged.append((jnp.concatenate([ah, al], axis=0), _blockdiag_mul(b.astype(BF16), bd_b)))
        outs = []
        for a_hl, b_bd in staged:
            n = a_hl.shape[0] // 2
            hl = _dot(a_hl, b_bd)
            outs.append(hl[0:n] + hl[n:2 * n])
        return outs

    ms = [-(lm * same16) for lm in lms]
    ps = [eye + m for eye, m in zip(eyes, ms)]
    ms = prod(ms, ms)
    for _ in range(2):
        pms = prod([jnp.concatenate([p, m], axis=0) for p, m in zip(ps, ms)], ms)
        ps = [p + pm[0:CHUNK] for p, pm in zip(ps, pms)]
        ms = [pm[CHUNK:2 * CHUNK] for pm in pms]
    ps = [p + pm for p, pm in zip(ps, prod(ps, ms))]
    for keep in (same32 - same16, 1.0 - same32):
        offs = [lm * keep for lm in lms]
        ps = [p - c for p, c in zip(ps, prod(ps, prod(offs, ps)))]
    return ps


def _gdn_kernel(p_ref, conv_ref, alog_ref, dtb_ref, ng_ref, ea_ref, t4_ref, t4t_ref, s0_ref,
                y_ref, sout_ref,
                q_s, k_s, v_s, gc_s, beta_s, u_s, o_s, w_s, qe_s, attn_s, kdec_s, st_s, pad_s, *, seq_len):
    n_chunks = seq_len // CHUNK
    bd_f = jnp.where(_head_ones(), 1.0, 0.0)
    bd_b = bd_f.astype(BF16)
    ones_bd = bd_b

    qkv = _silu(_short_conv(pad_s, p_ref[0, :, 0:3 * GROUP_W], conv_ref, GDN_CONV, seq_len))
    q = qkv[:, 0:GROUP_W]
    k = qkv[:, GROUP_W:2 * GROUP_W]
    q_s[...] = q * lax.rsqrt(_seg_sum(q * q, ones_bd) + EPS) * (HEAD_DIM ** -0.5)
    k_s[...] = k * lax.rsqrt(_seg_sum(k * k, ones_bd) + EPS)
    v_s[...] = qkv[:, 2 * GROUP_W:3 * GROUP_W]

    raw = p_ref[0, :, 4 * GROUP_W:4 * GROUP_W + LANES]
    lane = _iota(raw.shape, 1)
    log_decay = -jnp.exp(alog_ref[...]) * _softplus(raw + dtb_ref[...])
    compact = jnp.where(lane < 2 * N_HEADS, _dir_cumsum(log_decay, CHUNK, lane),
                        jnp.where(lane < N_GATE_LANES, _sigmoid(raw), 0.0))
    gcb = _expand_gates(compact, ea_ref[...])
    for d in range(2):
        gc_s[d] = gcb[:, d * GROUP_W:(d + 1) * GROUP_W]
        beta_s[d] = gcb[:, (2 + d) * GROUP_W:(3 + d) * GROUP_W]
        st_s[d] = bd_f * _dot_sel(s0_ref[0, d], t4_ref[...])

    masks = [_chunk_masks_f32(False), _chunk_masks_f32(True)]
    r = _iota((CHUNK, GROUP_W), 0)
    j = _iota((CHUNK, GROUP_W), 1) % HEAD_DIM
    same16 = jnp.where((r // 16) == (j // 16), 1.0, 0.0)
    same32 = jnp.where((r // 32) == (j // 32), 1.0, 0.0)
    ones_cc = jnp.ones((CHUNK, CHUNK), BF16)

    def prepare_body(n, carry):
        ds = [d for _ in range(GDN_GROUP) for d in range(2)]
        rows = [pl.ds(pl.multiple_of((GDN_GROUP * n + c) * CHUNK, CHUNK), CHUNK)
                for c in range(GDN_GROUP) for _ in range(2)]
        q_c = [q_s[r, :] for r in rows]
        k_c = [k_s[r, :] for r in rows]
        gc = [gc_s[d, r, :] for d, r in zip(ds, rows)]
        beta = [beta_s[d, r, :] for d, r in zip(ds, rows)]
        gc_row = [_sel_dot(ones_cc, g * masks[d][2]) for d, g in zip(ds, gc)]
        decay = [masks[d][0] * jnp.exp(jnp.minimum(g - gr, 0.0)) for d, g, gr in zip(ds, gc, gc_row)]
        kb = [k * b for k, b in zip(k_c, beta)]
        kq = [_dot_nt(jnp.concatenate([kbi, qi], axis=0).astype(BF16), _blockdiag_mul(ki.astype(BF16), bd_b))
              for kbi, qi, ki in zip(kb, q_c, k_c)]
        tmat = _tri_inverse([kqi[0:CHUNK] * (dec * masks[d][1]) for d, kqi, dec in zip(ds, kq, decay)],
                            [masks[d][2] for d in ds], same16, same32, bd_b)
        egc = [jnp.exp(g) for g in gc]
        rhs = [jnp.concatenate([_blockdiag_mul((v_s[r, :] * b).astype(BF16), bd_b),
                                _blockdiag_mul((kbi * e).astype(BF16), bd_b)], axis=1)
               for r, b, kbi, e in zip(rows, beta, kb, egc)]
        uw = [_dot(t.astype(BF16), x) for t, x in zip(tmat, rhs)]
        for i, (d, r) in enumerate(zip(ds, rows)):
            last = 0 if d == 1 else CHUNK - 1
            u_s[d, r, :] = uw[i][:, 0:GROUP_W]
            w_s[d, r, :] = uw[i][:, GROUP_W:2 * GROUP_W].astype(BF16)
            qe_s[d, r, :] = (q_c[i] * egc[i]).astype(BF16)
            attn_s[d, r, :] = (kq[i][CHUNK:2 * CHUNK] * decay[i]).astype(BF16)
            kdec_s[d, r, :] = (k_c[i] * jnp.exp(gc[i][last:last + 1, :] - gc[i])).astype(BF16)
        return carry

    lax.fori_loop(0, n_chunks // GDN_GROUP, prepare_body, 0)

    def scan_body(n, carry):
        r0 = [pl.multiple_of(n * CHUNK, CHUNK), pl.multiple_of((n_chunks - 1 - n) * CHUNK, CHUNK)]
        rows = [pl.ds(r, CHUNK) for r in r0]
        last = [CHUNK - 1, 0]
        s_bd = [st_s[d] for d in range(2)]
        ws_qs = [_dot(jnp.concatenate([w_s[d, rows[d], :], qe_s[d, rows[d], :]], axis=0), s_bd[d].astype(BF16))
                 for d in range(2)]
        v_new = [(u_s[d, rows[d], :] - ws_qs[d][0:CHUNK]).astype(BF16) for d in range(2)]
        intra = [_dot(attn_s[d, rows[d], :], _blockdiag_mul(v_new[d], bd_b)) for d in range(2)]
        outer = [_dot_tn(kdec_s[d, rows[d], :], v_new[d]) for d in range(2)]
        for d in range(2):
            o_s[d, rows[d], :] = ws_qs[d][CHUNK:2 * CHUNK] + intra[d]
            g_last = jnp.exp(gc_s[d, pl.ds(r0[d] + last[d], 1), :])
            st_s[d] = s_bd[d] * g_last + bd_f * outer[d]
        return carry

    lax.fori_loop(0, n_chunks, scan_body, 0)

    o = o_s[0] + o_s[1]
    o = o * lax.rsqrt(_seg_sum(o * o, ones_bd) * (1.0 / HEAD_DIM) + EPS) * ng_ref[...]
    y_ref[0] = (o * _silu(p_ref[0, :, 3 * GROUP_W:4 * GROUP_W])).astype(BF16)
    for d in range(2):
        sout_ref[0, d] = _dot_sel(st_s[d], t4t_ref[...])


def _compact_row(m, kind):
    row = jnp.zeros((1, LANES), F32)
    return row.at[0, kind * 2 * N_HEADS:(kind + 1) * 2 * N_HEADS].set(m.astype(F32).reshape(2 * N_HEADS))


N_GATE_LANES = 4 * N_HEADS


def _gate_expander():
    sel = np.zeros((LANES, 4 * GROUP_W), np.float32)
    for piece in range(3):
        for kind in range(2):
            for d in range(2):
                for h in range(N_HEADS):
                    c0 = (kind * 2 + d) * GROUP_W + h * HEAD_DIM
                    sel[piece * N_GATE_LANES + kind * 2 * N_HEADS + d * N_HEADS + h, c0:c0 + HEAD_DIM] = 1.0
    return jnp.asarray(sel, BF16)


def _expand_gates(g, sel):
    p1 = g.astype(BF16).astype(F32)
    r1 = g - p1
    p2 = r1.astype(BF16).astype(F32)
    packed = p1 + pltpu.roll(p2, N_GATE_LANES, axis=1) + pltpu.roll(r1 - p2, 2 * N_GATE_LANES, axis=1)
    return _dot(packed.astype(BF16), sel)


def _tile4():
    t = np.concatenate([np.eye(HEAD_DIM, dtype=np.float32)] * N_HEADS, axis=1)
    return jnp.asarray(t, BF16), jnp.asarray(t.T, BF16)


def _seq_spec(seq_len, width):
    return pl.BlockSpec((1, seq_len, width), lambda b: (b, 0, 0))


def _full_spec(shape):
    nd = len(shape)
    return pl.BlockSpec(shape, lambda b: (0,) * nd)


def _state_spec():
    return pl.BlockSpec((1, 2, GROUP_W, HEAD_DIM), lambda b: (b, 0, 0, 0))


def _gdn(slab, conv_w, a_log, dt_bias, norm_g, s0, seq_len):
    assert seq_len % (CHUNK * GDN_GROUP) == 0
    bsz = slab.shape[0]
    t4, t4t = _tile4()
    return pl.pallas_call(
        functools.partial(_gdn_kernel, seq_len=seq_len),
        grid=(bsz,),
        in_specs=[_seq_spec(seq_len, W_SLAB_A), _full_spec((GDN_CONV, 3 * GROUP_W)), _full_spec((1, LANES)),
                  _full_spec((1, LANES)), _full_spec((1, GROUP_W)), _full_spec((LANES, 4 * GROUP_W)),
                  _full_spec((HEAD_DIM, GROUP_W)), _full_spec((GROUP_W, HEAD_DIM)), _state_spec()],
        out_specs=[_seq_spec(seq_len, GROUP_W), _state_spec()],
        out_shape=[jax.ShapeDtypeStruct((bsz, seq_len, GROUP_W), BF16),
                   jax.ShapeDtypeStruct((bsz, 2, GROUP_W, HEAD_DIM), F32)],
        scratch_shapes=[pltpu.VMEM((seq_len, GROUP_W), F32)] * 3
                       + [pltpu.VMEM((2, seq_len, GROUP_W), F32)] * 4
                       + [pltpu.VMEM((2, seq_len, GROUP_W), BF16)] * 4
                       + [pltpu.VMEM((2, GROUP_W, GROUP_W), F32),
                          pltpu.VMEM((seq_len + 2 * CONV_PAD, 3 * GROUP_W), F32)],
        compiler_params=pltpu.CompilerParams(dimension_semantics=("parallel",), vmem_limit_bytes=VMEM_LIMIT),
        name="gdn",
    )(slab, conv_w, _compact_row(a_log, 0), _compact_row(dt_bias, 0), jnp.tile(norm_g, N_HEADS)[None, :],
      _gate_expander(), t4, t4t, s0)


def _mlstm_kernel(p_ref, fb_ref, ng_ref, eg_ref, t4_ref, t4t_ref, c0_ref, n0_ref, m0_ref,
                  y_ref, cout_ref, nout_ref, mout_ref,
                  k_s, bc_s, ig_s, h_s, c_s, n_s, m_s, *, seq_len):
    n_chunks = seq_len // CHUNK
    bd_mask = _head_ones()
    bd_f = jnp.where(bd_mask, 1.0, 0.0)
    bd_b = bd_f.astype(BF16)
    ones_bd = bd_b
    lane_head = _iota((CHUNK, GROUP_W), 1) // HEAD_DIM
    chains = [(s, d) for s in range(MLSTM_SEQS) for d in range(2)]

    for s in range(MLSTM_SEQS):
        k_s[s] = p_ref[s, :, GROUP_W:2 * GROUP_W] * (HEAD_DIM ** -0.5)
        raw = p_ref[s, :, 4 * GROUP_W:4 * GROUP_W + LANES]
        lane = _iota(raw.shape, 1)
        logf = -_softplus(-(raw + fb_ref[...]))
        compact = jnp.where(lane < 2 * N_HEADS, raw,
                            jnp.where(lane < N_GATE_LANES, _dir_cumsum(logf, CHUNK, lane), 0.0))
        gates = _expand_gates(compact, eg_ref[...])
        for d in range(2):
            ig_s[2 * s + d] = gates[:, d * GROUP_W:(d + 1) * GROUP_W]
            bc_s[2 * s + d] = gates[:, (2 + d) * GROUP_W:(3 + d) * GROUP_W]
            c_s[2 * s + d] = jnp.where(bd_mask, _dot_sel(c0_ref[s, d], t4_ref[...]), 0.0)
            n_s[2 * s + d] = n0_ref[s, d]
            m_s[2 * s + d] = m0_ref[s, d]

    masks = [_chunk_masks(False), _chunk_masks(True)]
    ones_cc = jnp.ones((CHUNK, CHUNK), BF16)

    def seg_max(dm):
        dmax = jnp.zeros((CHUNK, GROUP_W), F32)
        for h in range(N_HEADS):
            in_head = lane_head == h
            mh = jnp.max(jnp.where(in_head, dm, -jnp.inf), axis=1, keepdims=True)
            dmax = jnp.where(in_head, mh, dmax)
        return dmax

    def body(n, carry):
        last = [CHUNK - 1 if d == 0 else 0 for _, d in chains]
        fwd_rows = pl.ds(pl.multiple_of(n * CHUNK, CHUNK), CHUNK)
        bwd_rows = pl.ds(pl.multiple_of((n_chunks - 1 - n) * CHUNK, CHUNK), CHUNK)
        rows = [bwd_rows if d == 1 else fwd_rows for _, d in chains]
        ids = range(len(chains))
        q_c = [p_ref[s, rows[i], 0:GROUP_W] for i, (s, _) in enumerate(chains)]
        k_c = [k_s[s, rows[i], :] for i, (s, _) in enumerate(chains)]
        v_b = [p_ref[s, rows[i], 2 * GROUP_W:3 * GROUP_W].astype(BF16) for i, (s, _) in enumerate(chains)]
        bc = [bc_s[i, rows[i], :] for i in ids]
        x = [ig_s[i, rows[i], :] - bc[i] for i in ids]
        x_row = [_sel_dot(ones_cc, jnp.where(masks[d][2], x[i], 0.0))
                 for i, (_, d) in enumerate(chains)]
        dm = [jnp.where(masks[d][0], bc[i] + x_row[i], -jnp.inf) for i, (_, d) in enumerate(chains)]
        dmax = [seg_max(v) for v in dm]
        a = [bc[i] + m_s[i] for i in ids]
        m_t = [jnp.maximum(a[i], dmax[i]) for i in ids]
        inter = [jnp.exp(a[i] - m_t[i]) for i in ids]
        q_b = [q.astype(BF16) for q in q_c]
        qk = [_dot_nt(q_b[i], _blockdiag_mul(k_c[i].astype(BF16), bd_b)) for i in ids]
        w = [(jnp.exp(dm[i] - m_t[i]) * qk[i]).astype(BF16) for i in ids]
        c_bd = [c_s[i] for i in ids]
        n_row = [n_s[i] for i in ids]
        q_cm = [_dot(q_b[i], c_bd[i].astype(BF16)) for i in ids]
        q_n = [_seg_sum(q_c[i] * n_row[i], ones_bd) for i in ids]
        wv = [_dot(w[i], jnp.concatenate([_blockdiag_mul(v_b[i], bd_b), ones_bd], axis=1)) for i in ids]
        m_new = [m_t[i][last[i]:last[i] + 1, :] for i in ids]
        carry_w = [jnp.exp(a[i][last[i]:last[i] + 1, :] - m_new[i]) for i in ids]
        kw = [k_c[i] * jnp.exp(bc[i][last[i]:last[i] + 1, :] + x[i] - m_new[i]) for i in ids]
        outer = [_dot_tn(kw[i].astype(BF16), v_b[i]) for i in ids]
        for i in ids:
            num = inter[i] * q_cm[i] + wv[i][:, 0:GROUP_W]
            den = inter[i] * q_n[i] + wv[i][:, GROUP_W:2 * GROUP_W]
            h_s[i, rows[i], :] = num / jnp.maximum(jnp.abs(den), jnp.exp(-m_t[i]))
            c_s[i] = carry_w[i] * c_bd[i] + bd_f * outer[i]
            n_s[i] = carry_w[i] * n_row[i] + jnp.sum(kw[i], axis=0, keepdims=True)
            m_s[i] = m_new[i]
        return carry

    lax.fori_loop(0, n_chunks, body, 0)

    for s in range(MLSTM_SEQS):
        h = h_s[2 * s] + h_s[2 * s + 1]
        h = h * lax.rsqrt(_seg_sum(h * h, ones_bd) * (1.0 / HEAD_DIM) + EPS) * ng_ref[...]
        y_ref[s] = (h * _sigmoid(p_ref[s, :, 3 * GROUP_W:4 * GROUP_W])).astype(BF16)
        for d in range(2):
            cout_ref[s, d] = _dot_sel(c_s[2 * s + d], t4t_ref[...])
            nout_ref[s, d] = n_s[2 * s + d]
            mout_ref[s, d] = m_s[2 * s + d]


def _mlstm(slab, f_bias, norm_g, c0, n0, m0, seq_len):
    bsz = slab.shape[0]
    ns = MLSTM_SEQS
    assert bsz % ns == 0
    seqs = lambda *tail: pl.BlockSpec((ns,) + tail, lambda b: (b,) + (0,) * len(tail))
    t4, t4t = _tile4()
    return pl.pallas_call(
        functools.partial(_mlstm_kernel, seq_len=seq_len),
        grid=(bsz // ns,),
        in_specs=[seqs(seq_len, W_SLAB_C), _full_spec((1, LANES)), _full_spec((1, GROUP_W)),
                  _full_spec((LANES, 4 * GROUP_W)), _full_spec((HEAD_DIM, GROUP_W)),
                  _full_spec((GROUP_W, HEAD_DIM)), seqs(2, GROUP_W, HEAD_DIM), seqs(2, 1, GROUP_W),
                  seqs(2, 1, GROUP_W)],
        out_specs=[seqs(seq_len, GROUP_W), seqs(2, GROUP_W, HEAD_DIM), seqs(2, 1, GROUP_W), seqs(2, 1, GROUP_W)],
        out_shape=[jax.ShapeDtypeStruct((bsz, seq_len, GROUP_W), BF16),
                   jax.ShapeDtypeStruct((bsz, 2, GROUP_W, HEAD_DIM), F32),
                   jax.ShapeDtypeStruct((bsz, 2, 1, GROUP_W), F32),
                   jax.ShapeDtypeStruct((bsz, 2, 1, GROUP_W), F32)],
        scratch_shapes=[pltpu.VMEM((ns, seq_len, GROUP_W), F32), pltpu.VMEM((2 * ns, seq_len, GROUP_W), F32),
                        pltpu.VMEM((2 * ns, seq_len, GROUP_W), F32), pltpu.VMEM((2 * ns, seq_len, GROUP_W), F32),
                        pltpu.VMEM((2 * ns, GROUP_W, GROUP_W), F32), pltpu.VMEM((2 * ns, 1, GROUP_W), F32),
                        pltpu.VMEM((2 * ns, 1, GROUP_W), F32)],
        compiler_params=pltpu.CompilerParams(dimension_semantics=("parallel",), vmem_limit_bytes=VMEM_LIMIT),
        name="mlstm",
    )(slab, _compact_row(f_bias, 1), jnp.tile(norm_g, N_HEADS)[None, :], _gate_expander(), t4, t4t, c0, n0, m0)


def _hgrn_kernel(p_ref, lb_ref, ng_ref, t4_ref, t4t_ref, s0_ref, y_ref, sout_ref,
                 k_s, gc_s, o_s, st_s, *scan_order_s, seq_len, grid_rows):
    c = HGRN_CHUNK
    n_chunks = seq_len // c
    if grid_rows is None:
        src = p_ref.at[0]
    else:
        (src,) = scan_order_s
        src[...] = _swap_row_groups(p_ref[0], grid_rows, GRID_W)
    bd_mask = _head_ones()
    bd_f = jnp.where(bd_mask, 1.0, 0.0)
    ones_bd = bd_f.astype(BF16)
    sub = _iota((c, GROUP_W), 0)
    row_sum = jnp.where(_iota((c, c * c), 1) // c == _iota((c, c * c), 0), 1.0, 0.0).astype(BF16)

    for d in range(2):
        lb = lb_ref[d:d + 1, :]
        forget = lb + (1.0 - lb) * _sigmoid(src[:, (3 + d) * GROUP_W:(4 + d) * GROUP_W])
        k_s[d] = 1.0 - forget
        gc_s[d] = _seg_cumsum(jnp.log(forget), c, reverse=(d == 1))
        s_bd = jnp.where(bd_mask, _dot_sel(s0_ref[0, d], t4_ref[...]), 0.0)
        st_s[d] = s_bd.T

    def pair_weights(d, q_c, k_c, gc):
        gc2 = gc * LOG2_E
        pieces = []
        for i in range(c):
            keep = (sub >= i) if d == 1 else (sub <= i)
            e_i = jnp.where(keep, jnp.exp2(gc2[i:i + 1, :] - gc2), 0.0)
            pieces.append(e_i * q_c[i:i + 1, :] * k_c)
        return jnp.concatenate(pieces, axis=0).astype(BF16)

    def body(n, carry):
        ds = [d for _ in range(HGRN_GROUP) for d in range(2)]
        starts = [HGRN_GROUP * n + g if d == 0 else n_chunks - 1 - HGRN_GROUP * n - g
                  for g in range(HGRN_GROUP) for d in range(2)]
        rows = [pl.ds(pl.multiple_of(s * c, c), c) for s in starts]
        q_c = [src[r, 0:GROUP_W] for r in rows]
        v_c = [src[r, GROUP_W:2 * GROUP_W] for r in rows]
        k_c = [k_s[d, r, :] for d, r in zip(ds, rows)]
        gc = [gc_s[d, r, :] for d, r in zip(ds, rows)]
        pairs =[pair_weights(d, q, k, g) for d, q, k, g in zip(ds, q_c, k_c, gc)]
        attn = [_dot(p, ones_bd) for p in pairs]
        weighted = [(a * jnp.concatenate([v] * c, axis=0)).astype(BF16) for a, v in zip(attn, v_c)]
        intra = [_dot(row_sum, w) for w in weighted]
        q_dec = [(q * jnp.exp(g)).astype(BF16) for q, g in zip(q_c, gc)]
        gc_last = [g[(0 if d == 1 else c - 1):(1 if d == 1 else c), :] for d, g in zip(ds, gc)]
        outer = [_dot_tn(v.astype(BF16), (k * jnp.exp(gl - g)).astype(BF16))
                 for v, k, gl, g in zip(v_c, k_c, gc_last, gc)]
        for i, d in enumerate(ds):
            st = st_s[d]
            o_s[d, rows[i], :] = intra[i] + _dot_nt(q_dec[i], st.astype(BF16))
            st_s[d] = st * jnp.exp(gc_last[i]) + bd_f * outer[i]
        return carry

    lax.fori_loop(0, n_chunks // HGRN_GROUP, body, 0)

    o = o_s[0] + o_s[1]
    o = o * lax.rsqrt(_seg_sum(o * o, ones_bd) * (1.0 / HEAD_DIM) + EPS) * ng_ref[...]
    y = o * _sigmoid(src[:, 2 * GROUP_W:3 * GROUP_W])
    if grid_rows is None:
        y_ref[0] = y.astype(BF16)
    else:
        y_ref[0] = _swap_row_groups(y, GRID_W, grid_rows).astype(BF16)
    for d in range(2):
        sout_ref[0, d] = _dot_sel(st_s[d].T, t4t_ref[...])


def _hgrn(slab, lb, norm_g, s0, seq_len, grid_rows):
    bsz = slab.shape[0]
    t4, t4t = _tile4()
    scan_order = [] if grid_rows is None else [pltpu.VMEM((seq_len, W_SLAB_B), F32)]
    return pl.pallas_call(
        functools.partial(_hgrn_kernel, seq_len=seq_len, grid_rows=grid_rows),
        grid=(bsz,),
        in_specs=[_seq_spec(seq_len, W_SLAB_B), _full_spec((2, GROUP_W)), _full_spec((1, GROUP_W)),
                  _full_spec((HEAD_DIM, GROUP_W)), _full_spec((GROUP_W, HEAD_DIM)), _state_spec()],
        out_specs=[_seq_spec(seq_len, GROUP_W), _state_spec()],
        out_shape=[jax.ShapeDtypeStruct((bsz, seq_len, GROUP_W), BF16),
                   jax.ShapeDtypeStruct((bsz, 2, GROUP_W, HEAD_DIM), F32)],
        scratch_shapes=[pltpu.VMEM((2, seq_len, GROUP_W), F32), pltpu.VMEM((2, seq_len, GROUP_W), F32),
                        pltpu.VMEM((2, seq_len, GROUP_W), F32), pltpu.VMEM((2, GROUP_W, GROUP_W), F32)] + scan_order,
        compiler_params=pltpu.CompilerParams(dimension_semantics=("parallel",), vmem_limit_bytes=VMEM_LIMIT),
        name="hgrn",
    )(slab, lb.astype(F32), jnp.tile(norm_g, N_HEADS)[None, :], t4, t4t, s0)


def _dft_tables(n):
    k = np.arange(n, dtype=np.float64)[:, None]
    s = np.arange(n, dtype=np.float64)[None, :]
    ang = np.pi * k * s / n
    cos_m = np.cos(ang)
    sin_m = np.sin(ang)
    alt = (-1.0) ** np.arange(n)
    sin_m[0, :] = alt
    fwd = np.concatenate([cos_m, sin_m], axis=0)
    wgt = np.full((n,), 2.0 / (2 * n))
    wgt[0] = 1.0 / (2 * n)
    inv = np.concatenate([cos_m.T * wgt[None, :], sin_m.T * wgt[None, :]], axis=1)
    out = []
    for m in (fwd, inv):
        m32 = m.astype(np.float32)
        hi = m32.astype(BF16)
        lo = (m32 - hi.astype(np.float32)).astype(BF16)
        out.append(jnp.asarray(np.concatenate([hi, lo], axis=1)))
    return out


def _dft_apply(m_ref, x):
    xb = x.astype(BF16)
    return _dot(m_ref[...], jnp.concatenate([xb, xb], axis=0))


def _dft_apply_f32(m_ref, x):
    k = x.shape[0]
    xh, xl = _split2(x)
    return _dot(m_ref[...], jnp.concatenate([xh, xh], axis=0)) + _dot(m_ref[:, 0:k], xl)


def _filter_kernel(feat_ref, env_ref, w1_ref, b1_ref, w2_ref, b2_ref, w3_ref, freq_ref, f_ref,
                   hr_ref, hs_ref, *, seq_len):
    freq = freq_ref[...]
    hid = jnp.sin(freq * (_dot3(feat_ref[...], w1_ref[...]) + b1_ref[...]))
    hid = jnp.sin(freq * (_dot3(hid, w2_ref[...]) + b2_ref[...]))
    h = _dot3(hid, w3_ref[...])
    env = env_ref[...]
    not_first = _iota((seq_len, GROUP_W), 0) > 0
    first_row = _iota((seq_len, GROUP_W), 0) == 0
    sides = []
    for o in range(HYENA_ORDER):
        pos = h[:, (2 * o) * GROUP_W:(2 * o + 1) * GROUP_W] * env
        neg = jnp.where(not_first, h[:, (2 * o + 1) * GROUP_W:(2 * o + 2) * GROUP_W] * env, 0.0)
        ss = jnp.sum(pos * pos + neg * neg, axis=0, keepdims=True)
        scale = lax.rsqrt(ss + EPS)
        sides += [(pos + neg) * scale, (pos - neg) * scale]
    spec = _dft_apply_f32(f_ref, jnp.concatenate(sides, axis=1))
    for o in range(HYENA_ORDER):
        fa = spec[:, (2 * o) * GROUP_W:(2 * o + 1) * GROUP_W]
        fb = spec[:, (2 * o + 1) * GROUP_W:(2 * o + 2) * GROUP_W]
        hr_ref[o] = fa[0:seq_len]
        hs_ref[o] = jnp.where(first_row, fa[seq_len:2 * seq_len], fb[seq_len:2 * seq_len])


def _hyena_filters(seq_len, w1, b1, w2, b2, w3, freq, tables):
    t = np.arange(seq_len, dtype=np.float64)
    tn = np.linspace(0.0, 1.0, seq_len)
    bands = np.linspace(1e-4, POS_BANDS - 1, POS_BANDS)
    ang = (2.0 * math.pi / seq_len) * t[:, None] * bands[None, :]
    feats = np.zeros((seq_len, LANES), np.float32)
    feats[:, 0:POS_DIM] = np.concatenate([tn[:, None], np.cos(ang), -np.sin(ang)], axis=-1)
    deltas = np.abs(np.linspace(math.log(1e-2) / 1.5, math.log(1e-2) / 0.3, GROUP_W))
    env = np.exp(-tn[:, None] * deltas[None, :]).astype(np.float32)
    w1p = jnp.zeros((LANES, FILTER_HIDDEN), F32).at[0:POS_DIM].set(w1)
    n_h = HYENA_ORDER * 2 * GROUP_W
    full = lambda shape: pl.BlockSpec(shape, lambda: (0,) * len(shape))
    return pl.pallas_call(
        functools.partial(_filter_kernel, seq_len=seq_len),
        in_specs=[full((seq_len, LANES)), full((seq_len, GROUP_W)), full((LANES, FILTER_HIDDEN)),
                  full((1, FILTER_HIDDEN)), full((FILTER_HIDDEN, FILTER_HIDDEN)), full((1, FILTER_HIDDEN)),
                  full((FILTER_HIDDEN, n_h)), full((1, FILTER_HIDDEN)),
                  full((2 * seq_len, 2 * seq_len))],
        out_specs=[full((HYENA_ORDER, seq_len, GROUP_W)), full((HYENA_ORDER, seq_len, GROUP_W))],
        out_shape=[jax.ShapeDtypeStruct((HYENA_ORDER, seq_len, GROUP_W), F32)] * 2,
        compiler_params=pltpu.CompilerParams(vmem_limit_bytes=VMEM_LIMIT),
        name="hyena_filter",
    )(jnp.asarray(feats), jnp.asarray(env), w1p, b1[None, :], w2, b2[None, :], w3, freq[None, :], tables[0])


def _hyena_kernel(p_ref, conv_ref, hr_ref, hs_ref, db_ref, f_ref, i_ref, y_ref, pad_s, *, seq_len):
    us = [_short_conv(pad_s, p_ref[b], conv_ref, HYENA_CONV, seq_len) for b in range(HYENA_SEQS)]
    part = lambda k: jnp.concatenate([u[:, k * GROUP_W:(k + 1) * GROUP_W] for u in us], axis=1)
    wide = lambda t: jnp.concatenate([t] * HYENA_SEQS, axis=1)
    first_row = _iota((seq_len, HYENA_SEQS * GROUP_W), 0) == 0

    def longconv(z, o):
        zf = _dft_apply(f_ref, z)
        zr = zf[0:seq_len]
        zs = zf[seq_len:2 * seq_len]
        hr = wide(hr_ref[o])
        hs = wide(hs_ref[o])
        yr = zr * hr - jnp.where(first_row, 0.0, zs * hs)
        ys = jnp.where(first_row, zs * hs, zr * hs + zs * hr)
        y = _dft_apply(i_ref, jnp.concatenate([yr, ys], axis=0))
        return y + wide(db_ref[o:o + 1, :]) * z

    y = part(2) * longconv(part(1) * longconv(part(0), 0), 1)
    for b in range(HYENA_SEQS):
        y_ref[b] = y[:, b * GROUP_W:(b + 1) * GROUP_W].astype(BF16)


def _hyena(slab, conv_w, hr, hs, dbias, tables, seq_len):
    bsz = slab.shape[0]
    once = lambda shape: pl.BlockSpec(shape, lambda b: (0,) * len(shape), pipeline_mode=pl.Buffered(1))
    in_bytes = HYENA_SEQS * seq_len * W_SLAB_D * 4
    return pl.pallas_call(
        functools.partial(_hyena_kernel, seq_len=seq_len),
        grid=(bsz // HYENA_SEQS,),
        in_specs=[pl.BlockSpec((HYENA_SEQS, seq_len, W_SLAB_D), lambda b: (b, 0, 0),
                               pipeline_mode=pl.Buffered(2 if in_bytes <= HYENA_PREFETCH_MAX_BYTES else 1)),
                  _full_spec((HYENA_CONV, 3 * GROUP_W)),
                  once((HYENA_ORDER, seq_len, GROUP_W)), once((HYENA_ORDER, seq_len, GROUP_W)),
                  _full_spec((HYENA_ORDER, GROUP_W)),
                  once((2 * seq_len, 2 * seq_len)), once((seq_len, 4 * seq_len))],
        out_specs=pl.BlockSpec((HYENA_SEQS, seq_len, GROUP_W), lambda b: (b, 0, 0)),
        out_shape=jax.ShapeDtypeStruct((bsz, seq_len, GROUP_W), BF16),
        scratch_shapes=[pltpu.VMEM((seq_len + 2 * CONV_PAD, 3 * GROUP_W), F32)],
        compiler_params=pltpu.CompilerParams(dimension_semantics=("parallel",), vmem_limit_bytes=VMEM_LIMIT),
        name="hyena",
    )(slab, conv_w, hr, hs, dbias, *tables)


def _outproj_kernel(x_ref, ya_ref, yb_ref, yc_ref, yd_ref, wo_ref, mod_ref, g_ref, rwh_ref, rwl_ref, rb_ref,
                    x1_ref, h2_ref, gates_ref):
    rows_per_part = x_ref.shape[0] // OUTPROJ_PARTS
    parts = [slice(i * rows_per_part, (i + 1) * rows_per_part) for i in range(OUTPROJ_PARTS)]
    y = [(_dot(ya_ref[p, :], wo_ref[0:GROUP_W, :]) + _dot(yb_ref[p, :], wo_ref[GROUP_W:2 * GROUP_W, :])
          + _dot(yc_ref[p, :], wo_ref[2 * GROUP_W:3 * GROUP_W, :])
          + _dot(yd_ref[p, :], wo_ref[3 * GROUP_W:4 * GROUP_W, :])) for p in parts]
    x1 = [x_ref[p, :] + mod_ref[0, 2:3, :] * yi for p, yi in zip(parts, y)]
    for p, v in zip(parts, x1):
        x1_ref[p, :] = v
    split = [_split2(_modulated_norm(v, g_ref[...], mod_ref[0, 3:4, :], mod_ref[0, 4:5, :])) for v in x1]
    for p, (hh, _) in zip(parts, split):
        h2_ref[p, :] = hh
    logits = [_dot(hh, rwh_ref[...]) + _dot(hl, rwh_ref[...]) + _dot(hh, rwl_ref[...]) for hh, hl in split]
    scores = [_sigmoid(v) for v in logits]
    vals = [s + rb_ref[...] for s in scores]
    lane = _iota(vals[0].shape, 1).astype(F32)
    picked = [jnp.zeros(v.shape, F32) for v in vals]
    for _ in range(TOP_K):
        best = [jnp.max(v, axis=-1, keepdims=True) for v in vals]
        first = [jnp.min(jnp.where(v == b, lane, float(LANES)), axis=-1, keepdims=True) for v, b in zip(vals, best)]
        hit = [lane == f for f in first]
        picked = [jnp.where(h, s, pk) for h, s, pk in zip(hit, scores, picked)]
        vals = [jnp.where(h, -jnp.inf, v) for h, v in zip(hit, vals)]
    for p, pk in zip(parts, picked):
        gates_ref[p, :] = pk / jnp.sum(pk, axis=-1, keepdims=True) * ROUTED_SCALE


def _outproj(x, ys, w_out, mod, g, router_w, router_bias, seq_len, tm):
    t = x.shape[0]
    per_seq = seq_len // tm if mod.shape[0] > 1 else None

    def mod_idx(i):
        return (i // per_seq if per_seq else 0, 0, 0)

    rw = jnp.zeros((D_MODEL, LANES), F32).at[:, 0:N_EXPERTS].set(router_w)
    rwh = rw.astype(BF16)
    rwl = (rw - rwh.astype(F32)).astype(BF16)
    rb = jnp.full((1, LANES), -jnp.inf, F32).at[0, 0:N_EXPERTS].set(router_bias.astype(F32))
    row = lambda w: pl.BlockSpec((tm, w), lambda i: (i, 0))
    fix = lambda shape: pl.BlockSpec(shape, lambda i: (0,) * len(shape))
    return pl.pallas_call(
        _outproj_kernel,
        grid=(t // tm,),
        in_specs=[row(D_MODEL), row(GROUP_W), row(GROUP_W), row(GROUP_W), row(GROUP_W),
                  fix((D_MODEL, D_MODEL)), pl.BlockSpec((1, 6, D_MODEL), mod_idx), fix((1, D_MODEL)),
                  fix((D_MODEL, LANES)), fix((D_MODEL, LANES)), fix((1, LANES))],
        out_specs=[row(D_MODEL), row(D_MODEL), row(LANES)],
        out_shape=[jax.ShapeDtypeStruct((t, D_MODEL), F32), jax.ShapeDtypeStruct((t, D_MODEL), BF16),
                   jax.ShapeDtypeStruct((t, LANES), F32)],
        compiler_params=pltpu.CompilerParams(dimension_semantics=("parallel",), vmem_limit_bytes=VMEM_LIMIT),
        name="outproj",
    )(x, *ys, w_out.astype(BF16), mod, g, rwh, rwl, rb)


def _moe_kernel(h_ref, gates_ref, x1_ref, mod_ref, wg_ref, wu_ref, wd_ref, sg_ref, su_ref, sd_ref, fg_ref,
                o_ref, acc_ref, *, final_norm):
    j = pl.program_id(1)
    h = h_ref[...]
    width = EXPERTS_PER_STEP * D_EXPERT

    @pl.when(j == 0)
    def _():
        sh = _silu(_dot(h, sg_ref[...].astype(BF16))) * _dot(h, su_ref[...].astype(BF16))
        acc_ref[...] = _dot(sh.astype(BF16), sd_ref[...].astype(BF16))

    w_gu = jnp.concatenate([wg_ref[e] for e in range(EXPERTS_PER_STEP)]
                           + [wu_ref[e] for e in range(EXPERTS_PER_STEP)], axis=1).astype(BF16)
    gu = _dot(h, w_gu)
    mine = pltpu.roll(gates_ref[...], (LANES - EXPERTS_PER_STEP * j) % LANES, axis=1)
    gate_w = jnp.concatenate([jnp.broadcast_to(mine[:, e:e + 1], (mine.shape[0], D_EXPERT))
                              for e in range(EXPERTS_PER_STEP)], axis=1)
    hid = (_silu(gu[:, 0:width]) * gu[:, width:2 * width]) * gate_w
    w_dn = jnp.concatenate([wd_ref[e] for e in range(EXPERTS_PER_STEP)], axis=0).astype(BF16)
    acc_ref[...] += _dot(hid.astype(BF16), w_dn)

    @pl.when(j == pl.num_programs(1) - 1)
    def _():
        out = x1_ref[...] + mod_ref[0, 5:6, :] * acc_ref[...]
        if final_norm:
            ms = jnp.mean(out * out, axis=-1, keepdims=True)
            out = out * lax.rsqrt(ms + EPS) * fg_ref[...]
        o_ref[...] = out


def _moe(h2, gates, x1, mod, layer, w_gate, w_up, w_down, sh_gate, sh_up, sh_down, final_g, seq_len, tm,
         final_norm):
    t = h2.shape[0]
    per_seq = max(seq_len // tm, 1) if mod.shape[0] > 1 else None
    seqs_per_tile = max(tm // seq_len, 1)
    assert mod.shape[0] == 1 or seqs_per_tile == 1

    def mod_idx(i, j):
        return (i // per_seq if per_seq else 0, 0, 0)

    row = lambda w: pl.BlockSpec((tm, w), lambda i, j: (i, 0))
    fix = lambda shape: pl.BlockSpec(shape, lambda i, j: (0,) * len(shape))
    eps_ = EXPERTS_PER_STEP
    return pl.pallas_call(
        functools.partial(_moe_kernel, final_norm=final_norm),
        grid=(t // tm, N_EXPERTS // eps_),
        in_specs=[row(D_MODEL), row(LANES), row(D_MODEL), pl.BlockSpec((1, 6, D_MODEL), mod_idx),
                  pl.BlockSpec((None, eps_, D_MODEL, D_EXPERT), lambda i, j: (layer, j, 0, 0)),
                  pl.BlockSpec((None, eps_, D_MODEL, D_EXPERT), lambda i, j: (layer, j, 0, 0)),
                  pl.BlockSpec((None, eps_, D_EXPERT, D_MODEL), lambda i, j: (layer, j, 0, 0)),
                  fix((D_MODEL, D_EXPERT)), fix((D_MODEL, D_EXPERT)), fix((D_EXPERT, D_MODEL)),
                  fix((1, D_MODEL))],
        out_specs=row(D_MODEL),
        out_shape=jax.ShapeDtypeStruct((t, D_MODEL), F32),
        scratch_shapes=[pltpu.VMEM((tm, D_MODEL), F32)],
        compiler_params=pltpu.CompilerParams(dimension_semantics=("parallel", "arbitrary"),
                                             vmem_limit_bytes=VMEM_LIMIT),
        name="moe",
    )(h2, gates, x1, mod, w_gate, w_up, w_down, sh_gate, sh_up, sh_down, final_g)


def _in_weight_slabs(w_in):
    ga = 4 * GROUP_W
    gb = ga + 4 * N_HEADS
    hb = gb + 5 * GROUP_W
    mc = hb + 4 * GROUP_W
    md = mc + 4 * N_HEADS
    pad = jnp.zeros((D_MODEL, LANES - 4 * N_HEADS), w_in.dtype)
    wa = jnp.concatenate([w_in[:, 0:gb], pad], axis=1)
    wb = w_in[:, gb:hb]
    wc = jnp.concatenate([w_in[:, hb:md], pad], axis=1)
    wd = w_in[:, md:]
    return [w.astype(BF16) for w in (wa, wb, wc, wd)]


def _layer(x, mod, states, prm, stacked, layer, lb, seq_len, grid_rows, tables, final_g, final_norm):
    bsz = x.shape[0] // seq_len
    tm = 512
    s_gdn, s_hgrn, s_c, s_n, s_m = states
    pa, pb, pc, pd = _inproj(x, mod, prm['norm1_g'][None, :], _in_weight_slabs(prm['w_in']), seq_len, tm)
    pa = pa.reshape(bsz, seq_len, W_SLAB_A)
    pb = pb.reshape(bsz, seq_len, W_SLAB_B)
    pc = pc.reshape(bsz, seq_len, W_SLAB_C)
    pd = pd.reshape(bsz, seq_len, W_SLAB_D)

    ya, st_a = _gdn(pa, prm['gdn_conv'], prm['gdn_a_log'], prm['gdn_dt_bias'], prm['gdn_norm_g'],
                    s_gdn.reshape(bsz, 2, GROUP_W, HEAD_DIM), seq_len)
    yb, st_b = _hgrn(pb, lb, prm['hgrn_norm_g'], s_hgrn.reshape(bsz, 2, GROUP_W, HEAD_DIM), seq_len, grid_rows)
    yc, st_c, st_n, st_m = _mlstm(pc, prm['mlstm_f_bias'], prm['mlstm_norm_g'],
                                  s_c.reshape(bsz, 2, GROUP_W, HEAD_DIM), s_n.reshape(bsz, 2, 1, GROUP_W),
                                  jnp.repeat(s_m, HEAD_DIM, axis=-1).reshape(bsz, 2, 1, GROUP_W), seq_len)
    hr, hs = _hyena_filters(seq_len, prm['filt_w1'], prm['filt_b1'], prm['filt_w2'], prm['filt_b2'],
                            prm['filt_w3'], prm['filt_freq'], tables)
    yd = _hyena(pd, prm['hyena_conv'], hr, hs, prm['hyena_d'], tables, seq_len)

    t = x.shape[0]
    ys = [y.reshape(t, GROUP_W) for y in (ya, yb, yc, yd)]
    x1, h2, gates = _outproj(x, ys, prm['w_out'], mod, prm['norm2_g'][None, :], prm['router_w'],
                             prm['router_bias'], seq_len, tm)
    tm_moe = 1024
    out = _moe(h2, gates, x1, mod, layer, stacked['exp_w_gate'], stacked['exp_w_up'], stacked['exp_w_down'],
               prm['sh_w_gate'], prm['sh_w_up'], prm['sh_w_down'], final_g[None, :], seq_len, tm_moe, final_norm)
    new_states = (st_a.reshape(bsz, 2, N_HEADS, HEAD_DIM, HEAD_DIM),
                  st_b.reshape(bsz, 2, N_HEADS, HEAD_DIM, HEAD_DIM),
                  st_c.reshape(bsz, 2, N_HEADS, HEAD_DIM, HEAD_DIM),
                  st_n.reshape(bsz, 2, N_HEADS, HEAD_DIM),
                  st_m.reshape(bsz, 2, N_HEADS, HEAD_DIM)[..., 0])
    return out, new_states


def kernel(x_prompt, x_sample, c, state_gdn, state_hgrn, state_mlstm_c, state_mlstm_n, state_mlstm_m, c_ctx, norm1_g, norm2_g, w_mod, b_mod, w_in, gdn_conv, gdn_a_log, gdn_dt_bias, gdn_norm_g, hgrn_lb_logits, hgrn_norm_g, mlstm_f_bias, mlstm_norm_g, hyena_conv, filt_w1, filt_b1, filt_w2, filt_b2, filt_w3, filt_freq, hyena_d, w_out, router_w, router_bias, exp_w_gate, exp_w_up, exp_w_down, sh_w_gate, sh_w_up, sh_w_down, final_g):
    depth = w_in.shape[0]
    bsz, seq, _ = x_prompt.shape
    dbsz, dseq, _ = x_sample.shape
    p = jax.nn.softmax(hgrn_lb_logits.astype(F32), axis=0)
    lower = jnp.cumsum(p, axis=0) - p[0:1]

    per_layer = dict(norm1_g=norm1_g, norm2_g=norm2_g, w_in=w_in, gdn_conv=gdn_conv, gdn_a_log=gdn_a_log,
                     gdn_dt_bias=gdn_dt_bias, gdn_norm_g=gdn_norm_g, hgrn_norm_g=hgrn_norm_g,
                     mlstm_f_bias=mlstm_f_bias, mlstm_norm_g=mlstm_norm_g, hyena_conv=hyena_conv,
                     filt_w1=filt_w1, filt_b1=filt_b1, filt_w2=filt_w2, filt_b2=filt_b2, filt_w3=filt_w3,
                     filt_freq=filt_freq, hyena_d=hyena_d, w_out=w_out, router_w=router_w,
                     router_bias=router_bias, sh_w_gate=sh_w_gate, sh_w_up=sh_w_up, sh_w_down=sh_w_down)
    stacked = dict(exp_w_gate=exp_w_gate, exp_w_up=exp_w_up, exp_w_down=exp_w_down)

    cond = jnp.zeros((N_COND_ROWS, D_MODEL), F32).at[0].set(c_ctx).at[1:1 + dbsz].set(c)
    tables_p = _dft_tables(seq)
    tables_s = _dft_tables(dseq)

    zero_states = (jnp.zeros((bsz, 2, N_HEADS, HEAD_DIM, HEAD_DIM), F32),
                   jnp.zeros((bsz, 2, N_HEADS, HEAD_DIM, HEAD_DIM), F32),
                   jnp.zeros((bsz, 2, N_HEADS, HEAD_DIM, HEAD_DIM), F32),
                   jnp.zeros((bsz, 2, N_HEADS, HEAD_DIM), F32),
                   jnp.zeros((bsz, 2, N_HEADS), F32))
    h = x_prompt.reshape(bsz * seq, D_MODEL)
    z = x_sample.reshape(dbsz * dseq, D_MODEL)
    new_states = [[] for _ in range(5)]
    for l in range(depth):
        prm = {name: val[l] for name, val in per_layer.items()}
        mod = _adaln(cond, w_mod, b_mod[l], l).reshape(N_COND_ROWS, 6, D_MODEL)
        last = l == depth - 1
        h, st = _layer(h, mod[0:1], zero_states, prm, stacked, l, lower[l], seq, None, tables_p, final_g, last)
        for acc, s in zip(new_states, st):
            acc.append(s.astype(x_prompt.dtype))
        cached = (state_gdn[:, l].astype(F32), state_hgrn[:, l].astype(F32), state_mlstm_c[:, l].astype(F32),
                  state_mlstm_n[:, l].astype(F32), state_mlstm_m[:, l].astype(F32))
        z, _ = _layer(z, mod[1:1 + dbsz], cached, prm, stacked, l, lower[l], dseq, dseq // GRID_W, tables_s,
                      final_g, last)

    return (h.reshape(bsz, seq, D_MODEL), z.reshape(dbsz, dseq, D_MODEL),
            *[jnp.stack(acc, axis=1) for acc in new_states])
```

```python
import functools
import math

import numpy as np
import jax
import jax.numpy as jnp
from jax import lax
from jax.experimental import pallas as pl
from jax.experimental.pallas import tpu as pltpu

F32 = jnp.float32
BF16 = jnp.bfloat16

D_MODEL = 1024
N_HEADS = 4
HEAD_DIM = 64
GROUP_W = N_HEADS * HEAD_DIM
GRID_W = 64
CHUNK = 64
HGRN_CHUNK = 16
MLSTM_SEQS = 2
HGRN_GROUP = 2
OUTPROJ_PARTS = 2
HYENA_PREFETCH_MAX_BYTES = 4 * 1024 * 1024
HYENA_SEQS = 2
GDN_GROUP = 4
GDN_CONV = 5
HYENA_CONV = 3
HYENA_ORDER = 2
POS_BANDS = 16
POS_DIM = 1 + 2 * POS_BANDS
FILTER_HIDDEN = 64
N_EXPERTS = 64
TOP_K = 8
D_EXPERT = 128
ROUTED_SCALE = 2.5
EPS = 1e-6
LOG2_E = 1.4426950408889634
LANES = 128
CONV_PAD = 8
EXPERTS_PER_STEP = 4
N_COND_ROWS = 16
VMEM_LIMIT = 56 * 1024 * 1024

W_SLAB_A = 4 * GROUP_W + LANES
W_SLAB_B = 5 * GROUP_W
W_SLAB_C = 4 * GROUP_W + LANES
W_SLAB_D = 3 * GROUP_W


def _split2(x):
    hi = x.astype(BF16)
    lo = (x - hi.astype(F32)).astype(BF16)
    return hi, lo


def _split3(x):
    p1 = x.astype(BF16)
    r = x - p1.astype(F32)
    p2 = r.astype(BF16)
    p3 = (r - p2.astype(F32)).astype(BF16)
    return p1, p2, p3


def _dot(a, b):
    return jnp.dot(a, b, preferred_element_type=F32)


def _dot_nt(a, b):
    return lax.dot_general(a, b, (((1,), (1,)), ((), ())), preferred_element_type=F32)


def _dot_tn(a, b):
    return lax.dot_general(a, b, (((0,), (0,)), ((), ())), preferred_element_type=F32)


def _dot3(a, b):
    ah, al = _split2(a)
    bh, bl = _split2(b)
    return _dot(ah, bh) + _dot(al, bh) + _dot(ah, bl)


def _dot_sel(x, sel):
    p1, p2, p3 = _split3(x)
    return _dot(p1, sel) + _dot(p2, sel) + _dot(p3, sel)


def _sel_dot(sel, x):
    p1, p2, p3 = _split3(x)
    return _dot(sel, p1) + _dot(sel, p2) + _dot(sel, p3)


def _iota(shape, dim):
    return lax.broadcasted_iota(jnp.int32, shape, dim)


def _head_ones():
    r = _iota((GROUP_W, GROUP_W), 0) // HEAD_DIM
    c = _iota((GROUP_W, GROUP_W), 1) // HEAD_DIM
    return r == c


def _seg_sum(x, ones_bd):
    hi, lo = _split2(x)
    return _dot(hi, ones_bd) + _dot(lo, ones_bd)


def _sigmoid(x):
    return 1.0 / (1.0 + jnp.exp(-x))


def _silu(x):
    return x * _sigmoid(x)


def _softplus(x):
    return jnp.maximum(x, 0.0) + jnp.log(1.0 + jnp.exp(-jnp.abs(x)))


def _seg_cumsum(x, chunk, reverse):
    n = x.shape[0]
    pos = _iota(x.shape, 0) % chunk
    s = 1
    while s < chunk:
        if reverse:
            x = x + jnp.where(pos < chunk - s, pltpu.roll(x, n - s, axis=0), 0.0)
        else:
            x = x + jnp.where(pos >= s, pltpu.roll(x, s, axis=0), 0.0)
        s *= 2
    return x


def _swap_row_groups(x, a, b):
    n, ch = x.shape
    return jnp.swapaxes(x.reshape(a, b, ch), 0, 1).reshape(n, ch)


def _dir_cumsum(x, chunk, lane):
    backward = (lane // N_HEADS) % 2 == 1
    return jnp.where(backward, _seg_cumsum(x, chunk, True), _seg_cumsum(x, chunk, False))


def _short_conv(pad_ref, x, w_ref, width, n):
    ch = x.shape[1]
    pad_ref[0:CONV_PAD, :] = jnp.zeros((CONV_PAD, ch), F32)
    pad_ref[CONV_PAD + n:2 * CONV_PAD + n, :] = jnp.zeros((CONV_PAD, ch), F32)
    pad_ref[CONV_PAD:CONV_PAD + n, :] = x
    acc = None
    for j in range(width):
        off = CONV_PAD - width // 2 + j
        term = pad_ref[off:off + n, :] * w_ref[j:j + 1, :]
        acc = term if acc is None else acc + term
    return acc


def _chunk_masks(reverse):
    r = _iota((CHUNK, GROUP_W), 0)
    j = _iota((CHUNK, GROUP_W), 1) % HEAD_DIM
    if reverse:
        return j >= r, j > r, j == r
    return j <= r, j < r, j == r


def _adaln_kernel(c_ref, w_ref, b_ref, o_ref):
    c = c_ref[...]
    o_ref[...] = _dot3(_silu(c), w_ref[...]) + b_ref[...]


def _adaln(cond, w_mod, b_mod, layer):
    n_out = w_mod.shape[2]
    tn = 1536
    return pl.pallas_call(
        _adaln_kernel,
        grid=(n_out // tn,),
        in_specs=[pl.BlockSpec((N_COND_ROWS, D_MODEL), lambda j: (0, 0)),
                  pl.BlockSpec((None, D_MODEL, tn), lambda j: (layer, 0, j)),
                  pl.BlockSpec((1, tn), lambda j: (0, j))],
        out_specs=pl.BlockSpec((N_COND_ROWS, tn), lambda j: (0, j)),
        out_shape=jax.ShapeDtypeStruct((N_COND_ROWS, n_out), F32),
        compiler_params=pltpu.CompilerParams(dimension_semantics=("arbitrary",), vmem_limit_bytes=VMEM_LIMIT),
        name="adaln",
    )(cond, w_mod, b_mod.reshape(1, n_out))


def _modulated_norm(x, g, shift, scale):
    ms = jnp.mean(x * x, axis=-1, keepdims=True)
    return (x * lax.rsqrt(ms + EPS) * g) * (1.0 + scale) + shift


def _inproj_kernel(x_ref, mod_ref, g_ref, wa_ref, wb_ref, wc_ref, wd_ref, oa_ref, ob_ref, oc_ref, od_ref):
    h = _modulated_norm(x_ref[...], g_ref[...], mod_ref[0, 0:1, :], mod_ref[0, 1:2, :]).astype(BF16)
    oa_ref[...] = _dot(h, wa_ref[...])
    ob_ref[...] = _dot(h, wb_ref[...])
    oc_ref[...] = _dot(h, wc_ref[...])
    od_ref[...] = _dot(h, wd_ref[...])


def _inproj(x, mod, g, w_slabs, seq_len, tm):
    t = x.shape[0]
    per_seq = seq_len // tm if mod.shape[0] > 1 else None

    def mod_idx(i):
        return (i // per_seq if per_seq else 0, 0, 0)

    widths = (W_SLAB_A, W_SLAB_B, W_SLAB_C, W_SLAB_D)
    return pl.pallas_call(
        _inproj_kernel,
        grid=(t // tm,),
        in_specs=[pl.BlockSpec((tm, D_MODEL), lambda i: (i, 0)),
                  pl.BlockSpec((1, 6, D_MODEL), mod_idx),
                  pl.BlockSpec((1, D_MODEL), lambda i: (0, 0))]
                 + [pl.BlockSpec((D_MODEL, w), lambda i: (0, 0)) for w in widths],
        out_specs=[pl.BlockSpec((tm, w), lambda i: (i, 0)) for w in widths],
        out_shape=[jax.ShapeDtypeStruct((t, w), F32) for w in widths],
        compiler_params=pltpu.CompilerParams(dimension_semantics=("parallel",), vmem_limit_bytes=VMEM_LIMIT),
        name="inproj",
    )(x, mod, g, *w_slabs)


def _chunk_masks_f32(reverse):
    return tuple(jnp.where(m, 1.0, 0.0) for m in _chunk_masks(reverse))


def _blockdiag_mul(x, bd_factor):
    return jnp.concatenate([x] * N_HEADS, axis=0) * bd_factor


def _tri_inverse(lms, eyes, same16, same32, bd_b, fillers):
    pending = list(fillers)

    def prod(xs, ys):
        if pending:
            pending.pop(0)()
        staged = []
        for a, b in zip(xs, ys):
            ah, al = _split2(a)
            staged.append((jnp.concatenate([ah, al], axis=0), _blockdiag_mul(b.astype(BF16), bd_b)))
        outs = []
        for a_hl, b_bd in staged:
            n = a_hl.shape[0] // 2
            hl = _dot(a_hl, b_bd)
            outs.append(hl[0:n] + hl[n:2 * n])
        return outs

    ms = [-(lm * same16) for lm in lms]
    ps = [eye + m for eye, m in zip(eyes, ms)]
    ms = prod(ms, ms)
    for _ in range(2):
        pms = prod([jnp.concatenate([p, m], axis=0) for p, m in zip(ps, ms)], ms)
        ps = [p + pm[0:CHUNK] for p, pm in zip(ps, pms)]
        ms = [pm[CHUNK:2 * CHUNK] for pm in pms]
    ps = [p + pm for p, pm in zip(ps, prod(ps, ms))]
    for keep in (same32 - same16, 1.0 - same32):
        offs = [lm * keep for lm in lms]
        ps = [p - c for p, c in zip(ps, prod(ps, prod(offs, ps)))]
    for rest in pending:
        rest()
    return ps


def _gdn_kernel(p_ref, conv_ref, alog_ref, dtb_ref, ng_ref, ea_ref, t4_ref, t4t_ref, s0_ref,
                y_ref, sout_ref,
                q_s, k_s, v_s, gc_s, beta_s, u_s, o_s, w_s, qe_s, attn_s, kdec_s, st_s, pad_s, *, seq_len):
    n_chunks = seq_len // CHUNK
    bd_f = jnp.where(_head_ones(), 1.0, 0.0)
    bd_b = bd_f.astype(BF16)
    ones_bd = bd_b

    qkv = _silu(_short_conv(pad_s, p_ref[0, :, 0:3 * GROUP_W], conv_ref, GDN_CONV, seq_len))
    q = qkv[:, 0:GROUP_W]
    k = qkv[:, GROUP_W:2 * GROUP_W]
    q_s[...] = q * lax.rsqrt(_seg_sum(q * q, ones_bd) + EPS) * (HEAD_DIM ** -0.5)
    k_s[...] = k * lax.rsqrt(_seg_sum(k * k, ones_bd) + EPS)
    v_s[...] = qkv[:, 2 * GROUP_W:3 * GROUP_W]

    raw = p_ref[0, :, 4 * GROUP_W:4 * GROUP_W + LANES]
    lane = _iota(raw.shape, 1)
    log_decay = -jnp.exp(alog_ref[...]) * _softplus(raw + dtb_ref[...])
    compact = jnp.where(lane < 2 * N_HEADS, _dir_cumsum(log_decay, CHUNK, lane),
                        jnp.where(lane < N_GATE_LANES, _sigmoid(raw), 0.0))
    gcb = _expand_gates(compact, ea_ref[...])
    for d in range(2):
        gc_s[d] = gcb[:, d * GROUP_W:(d + 1) * GROUP_W]
        beta_s[d] = gcb[:, (2 + d) * GROUP_W:(3 + d) * GROUP_W]
        st_s[d] = bd_f * _dot_sel(s0_ref[0, d], t4_ref[...])

    masks = [_chunk_masks_f32(False), _chunk_masks_f32(True)]
    r = _iota((CHUNK, GROUP_W), 0)
    j = _iota((CHUNK, GROUP_W), 1) % HEAD_DIM
    same16 = jnp.where((r // 16) == (j // 16), 1.0, 0.0)
    same32 = jnp.where((r // 32) == (j // 32), 1.0, 0.0)
    ones_cc = jnp.ones((CHUNK, CHUNK), BF16)

    n_groups = n_chunks // GDN_GROUP

    def scan_step(n):
        r0 = [pl.multiple_of(n * CHUNK, CHUNK), pl.multiple_of((n_chunks - 1 - n) * CHUNK, CHUNK)]
        rows = [pl.ds(r, CHUNK) for r in r0]
        last = [CHUNK - 1, 0]
        s_bd = [st_s[d] for d in range(2)]
        ws_qs = [_dot(jnp.concatenate([w_s[d, rows[d], :], qe_s[d, rows[d], :]], axis=0), s_bd[d].astype(BF16))
                 for d in range(2)]
        v_new = [(u_s[d, rows[d], :] - ws_qs[d][0:CHUNK]).astype(BF16) for d in range(2)]
        intra = [_dot(attn_s[d, rows[d], :], _blockdiag_mul(v_new[d], bd_b)) for d in range(2)]
        outer = [_dot_tn(kdec_s[d, rows[d], :], v_new[d]) for d in range(2)]
        for d in range(2):
            o_s[d, rows[d], :] = ws_qs[d][CHUNK:2 * CHUNK] + intra[d]
            g_last = jnp.exp(gc_s[d, pl.ds(r0[d] + last[d], 1), :])
            st_s[d] = s_bd[d] * g_last + bd_f * outer[d]

    def prepare_group(t, fillers):
        ds = [d for _ in range(GDN_GROUP) for d in range(2)]
        rows = [pl.ds(pl.multiple_of((GDN_GROUP * t + c if d == 0 else n_chunks - GDN_GROUP * (t + 1) + c) * CHUNK,
                                     CHUNK), CHUNK) for c in range(GDN_GROUP) for d in range(2)]
        q_c = [q_s[r, :] for r in rows]
        k_c = [k_s[r, :] for r in rows]
        gc = [gc_s[d, r, :] for d, r in zip(ds, rows)]
        beta = [beta_s[d, r, :] for d, r in zip(ds, rows)]
        gc_row = [_sel_dot(ones_cc, g * masks[d][2]) for d, g in zip(ds, gc)]
        decay = [masks[d][0] * jnp.exp(jnp.minimum(g - gr, 0.0)) for d, g, gr in zip(ds, gc, gc_row)]
        kb = [k * b for k, b in zip(k_c, beta)]
        kq = [_dot_nt(jnp.concatenate([kbi, qi], axis=0).astype(BF16), _blockdiag_mul(ki.astype(BF16), bd_b))
              for kbi, qi, ki in zip(kb, q_c, k_c)]
        tmat = _tri_inverse([kqi[0:CHUNK] * (dec * masks[d][1]) for d, kqi, dec in zip(ds, kq, decay)],
                            [masks[d][2] for d in ds], same16, same32, bd_b, fillers)
        egc = [jnp.exp(g) for g in gc]
        rhs = [jnp.concatenate([_blockdiag_mul((v_s[r, :] * b).astype(BF16), bd_b),
                                _blockdiag_mul((kbi * e).astype(BF16), bd_b)], axis=1)
               for r, b, kbi, e in zip(rows, beta, kb, egc)]
        uw = [_dot(t.astype(BF16), x) for t, x in zip(tmat, rhs)]
        for i, (d, r) in enumerate(zip(ds, rows)):
            last = 0 if d == 1 else CHUNK - 1
            u_s[d, r, :] = uw[i][:, 0:GROUP_W]
            w_s[d, r, :] = uw[i][:, GROUP_W:2 * GROUP_W].astype(BF16)
            qe_s[d, r, :] = (q_c[i] * egc[i]).astype(BF16)
            attn_s[d, r, :] = (kq[i][CHUNK:2 * CHUNK] * decay[i]).astype(BF16)
            kdec_s[d, r, :] = (k_c[i] * jnp.exp(gc[i][last:last + 1, :] - gc[i])).astype(BF16)

    prepare_group(0, [])

    def pipelined(t, carry):
        prepare_group(t, [functools.partial(scan_step, GDN_GROUP * (t - 1) + k) for k in range(GDN_GROUP)])
        return carry

    lax.fori_loop(1, n_groups, pipelined, 0)
    for k in range(GDN_GROUP):
        scan_step(GDN_GROUP * (n_groups - 1) + k)

    o = o_s[0] + o_s[1]
    o = o * lax.rsqrt(_seg_sum(o * o, ones_bd) * (1.0 / HEAD_DIM) + EPS) * ng_ref[...]
    y_ref[0] = (o * _silu(p_ref[0, :, 3 * GROUP_W:4 * GROUP_W])).astype(BF16)
    for d in range(2):
        sout_ref[0, d] = _dot_sel(st_s[d], t4t_ref[...])


def _compact_row(m, kind):
    row = jnp.zeros((1, LANES), F32)
    return row.at[0, kind * 2 * N_HEADS:(kind + 1) * 2 * N_HEADS].set(m.astype(F32).reshape(2 * N_HEADS))


N_GATE_LANES = 4 * N_HEADS


def _gate_expander():
    sel = np.zeros((LANES, 4 * GROUP_W), np.float32)
    for piece in range(3):
        for kind in range(2):
            for d in range(2):
                for h in range(N_HEADS):
                    c0 = (kind * 2 + d) * GROUP_W + h * HEAD_DIM
                    sel[piece * N_GATE_LANES + kind * 2 * N_HEADS + d * N_HEADS + h, c0:c0 + HEAD_DIM] = 1.0
    return jnp.asarray(sel, BF16)


def _expand_gates(g, sel):
    p1 = g.astype(BF16).astype(F32)
    r1 = g - p1
    p2 = r1.astype(BF16).astype(F32)
    packed = p1 + pltpu.roll(p2, N_GATE_LANES, axis=1) + pltpu.roll(r1 - p2, 2 * N_GATE_LANES, axis=1)
    return _dot(packed.astype(BF16), sel)


def _tile4():
    t = np.concatenate([np.eye(HEAD_DIM, dtype=np.float32)] * N_HEADS, axis=1)
    return jnp.asarray(t, BF16), jnp.asarray(t.T, BF16)


def _seq_spec(seq_len, width):
    return pl.BlockSpec((1, seq_len, width), lambda b: (b, 0, 0))


def _full_spec(shape):
    nd = len(shape)
    return pl.BlockSpec(shape, lambda b: (0,) * nd)


def _state_spec():
    return pl.BlockSpec((1, 2, GROUP_W, HEAD_DIM), lambda b: (b, 0, 0, 0))


def _gdn(slab, conv_w, a_log, dt_bias, norm_g, s0, seq_len):
    assert seq_len % (CHUNK * GDN_GROUP) == 0
    bsz = slab.shape[0]
    t4, t4t = _tile4()
    return pl.pallas_call(
        functools.partial(_gdn_kernel, seq_len=seq_len),
        grid=(bsz,),
        in_specs=[_seq_spec(seq_len, W_SLAB_A), _full_spec((GDN_CONV, 3 * GROUP_W)), _full_spec((1, LANES)),
                  _full_spec((1, LANES)), _full_spec((1, GROUP_W)), _full_spec((LANES, 4 * GROUP_W)),
                  _full_spec((HEAD_DIM, GROUP_W)), _full_spec((GROUP_W, HEAD_DIM)), _state_spec()],
        out_specs=[_seq_spec(seq_len, GROUP_W), _state_spec()],
        out_shape=[jax.ShapeDtypeStruct((bsz, seq_len, GROUP_W), BF16),
                   jax.ShapeDtypeStruct((bsz, 2, GROUP_W, HEAD_DIM), F32)],
        scratch_shapes=[pltpu.VMEM((seq_len, GROUP_W), F32)] * 3
                       + [pltpu.VMEM((2, seq_len, GROUP_W), F32)] * 4
                       + [pltpu.VMEM((2, seq_len, GROUP_W), BF16)] * 4
                       + [pltpu.VMEM((2, GROUP_W, GROUP_W), F32),
                          pltpu.VMEM((seq_len + 2 * CONV_PAD, 3 * GROUP_W), F32)],
        compiler_params=pltpu.CompilerParams(dimension_semantics=("parallel",), vmem_limit_bytes=VMEM_LIMIT),
        name="gdn",
    )(slab, conv_w, _compact_row(a_log, 0), _compact_row(dt_bias, 0), jnp.tile(norm_g, N_HEADS)[None, :],
      _gate_expander(), t4, t4t, s0)


def _mlstm_kernel(p_ref, fb_ref, ng_ref, eg_ref, t4_ref, t4t_ref, c0_ref, n0_ref, m0_ref,
                  y_ref, cout_ref, nout_ref, mout_ref,
                  k_s, bc_s, ig_s, h_s, c_s, n_s, m_s, *, seq_len):
    n_chunks = seq_len // CHUNK
    bd_mask = _head_ones()
    bd_f = jnp.where(bd_mask, 1.0, 0.0)
    bd_b = bd_f.astype(BF16)
    ones_bd = bd_b
    lane_head = _iota((CHUNK, GROUP_W), 1) // HEAD_DIM
    chains = [(s, d) for s in range(MLSTM_SEQS) for d in range(2)]

    for s in range(MLSTM_SEQS):
        k_s[s] = p_ref[s, :, GROUP_W:2 * GROUP_W] * (HEAD_DIM ** -0.5)
        raw = p_ref[s, :, 4 * GROUP_W:4 * GROUP_W + LANES]
        lane = _iota(raw.shape, 1)
        logf = -_softplus(-(raw + fb_ref[...]))
        compact = jnp.where(lane < 2 * N_HEADS, raw,
                            jnp.where(lane < N_GATE_LANES, _dir_cumsum(logf, CHUNK, lane), 0.0))
        gates = _expand_gates(compact, eg_ref[...])
        for d in range(2):
            ig_s[2 * s + d] = gates[:, d * GROUP_W:(d + 1) * GROUP_W]
            bc_s[2 * s + d] = gates[:, (2 + d) * GROUP_W:(3 + d) * GROUP_W]
            c_s[2 * s + d] = jnp.where(bd_mask, _dot_sel(c0_ref[s, d], t4_ref[...]), 0.0)
            n_s[2 * s + d] = n0_ref[s, d]
            m_s[2 * s + d] = m0_ref[s, d]

    masks = [_chunk_masks(False), _chunk_masks(True)]
    ones_cc = jnp.ones((CHUNK, CHUNK), BF16)

    def seg_max(dm):
        dmax = jnp.zeros((CHUNK, GROUP_W), F32)
        for h in range(N_HEADS):
            in_head = lane_head == h
            mh = jnp.max(jnp.where(in_head, dm, -jnp.inf), axis=1, keepdims=True)
            dmax = jnp.where(in_head, mh, dmax)
        return dmax

    def body(n, carry):
        last = [CHUNK - 1 if d == 0 else 0 for _, d in chains]
        fwd_rows = pl.ds(pl.multiple_of(n * CHUNK, CHUNK), CHUNK)
        bwd_rows = pl.ds(pl.multiple_of((n_chunks - 1 - n) * CHUNK, CHUNK), CHUNK)
        rows = [bwd_rows if d == 1 else fwd_rows for _, d in chains]
        ids = range(len(chains))
        q_c = [p_ref[s, rows[i], 0:GROUP_W] for i, (s, _) in enumerate(chains)]
        k_c = [k_s[s, rows[i], :] for i, (s, _) in enumerate(chains)]
        v_b = [p_ref[s, rows[i], 2 * GROUP_W:3 * GROUP_W].astype(BF16) for i, (s, _) in enumerate(chains)]
        bc = [bc_s[i, rows[i], :] for i in ids]
        x = [ig_s[i, rows[i], :] - bc[i] for i in ids]
        x_row = [_sel_dot(ones_cc, jnp.where(masks[d][2], x[i], 0.0))
                 for i, (_, d) in enumerate(chains)]
        dm = [jnp.where(masks[d][0], bc[i] + x_row[i], -jnp.inf) for i, (_, d) in enumerate(chains)]
        dmax = [seg_max(v) for v in dm]
        a = [bc[i] + m_s[i] for i in ids]
        m_t = [jnp.maximum(a[i], dmax[i]) for i in ids]
        inter = [jnp.exp(a[i] - m_t[i]) for i in ids]
        q_b = [q.astype(BF16) for q in q_c]
        qk = [_dot_nt(q_b[i], _blockdiag_mul(k_c[i].astype(BF16), bd_b)) for i in ids]
        w = [(jnp.exp(dm[i] - m_t[i]) * qk[i]).astype(BF16) for i in ids]
        c_bd = [c_s[i] for i in ids]
        n_row = [n_s[i] for i in ids]
        q_cm = [_dot(q_b[i], c_bd[i].astype(BF16)) for i in ids]
        q_n = [_seg_sum(q_c[i] * n_row[i], ones_bd) for i in ids]
        wv = [_dot(w[i], jnp.concatenate([_blockdiag_mul(v_b[i], bd_b), ones_bd], axis=1)) for i in ids]
        m_new = [m_t[i][last[i]:last[i] + 1, :] for i in ids]
        carry_w = [jnp.exp(a[i][last[i]:last[i] + 1, :] - m_new[i]) for i in ids]
        kw = [k_c[i] * jnp.exp(bc[i][last[i]:last[i] + 1, :] + x[i] - m_new[i]) for i in ids]
        outer = [_dot_tn(kw[i].astype(BF16), v_b[i]) for i in ids]
        for i in ids:
            num = inter[i] * q_cm[i] + wv[i][:, 0:GROUP_W]
            den = inter[i] * q_n[i] + wv[i][:, GROUP_W:2 * GROUP_W]
            h_s[i, rows[i], :] = num / jnp.maximum(jnp.abs(den), jnp.exp(-m_t[i]))
            c_s[i] = carry_w[i] * c_bd[i] + bd_f * outer[i]
            n_s[i] = carry_w[i] * n_row[i] + jnp.sum(kw[i], axis=0, keepdims=True)
            m_s[i] = m_new[i]
        return carry

    lax.fori_loop(0, n_chunks, body, 0)

    for s in range(MLSTM_SEQS):
        h = h_s[2 * s] + h_s[2 * s + 1]
        h = h * lax.rsqrt(_seg_sum(h * h, ones_bd) * (1.0 / HEAD_DIM) + EPS) * ng_ref[...]
        y_ref[s] = (h * _sigmoid(p_ref[s, :, 3 * GROUP_W:4 * GROUP_W])).astype(BF16)
        for d in range(2):
            cout_ref[s, d] = _dot_sel(c_s[2 * s + d], t4t_ref[...])
            nout_ref[s, d] = n_s[2 * s + d]
            mout_ref[s, d] = m_s[2 * s + d]


def _mlstm(slab, f_bias, norm_g, c0, n0, m0, seq_len):
    bsz = slab.shape[0]
    ns = MLSTM_SEQS
    assert bsz % ns == 0
    seqs = lambda *tail: pl.BlockSpec((ns,) + tail, lambda b: (b,) + (0,) * len(tail))
    t4, t4t = _tile4()
    return pl.pallas_call(
        functools.partial(_mlstm_kernel, seq_len=seq_len),
        grid=(bsz // ns,),
        in_specs=[seqs(seq_len, W_SLAB_C), _full_spec((1, LANES)), _full_spec((1, GROUP_W)),
                  _full_spec((LANES, 4 * GROUP_W)), _full_spec((HEAD_DIM, GROUP_W)),
                  _full_spec((GROUP_W, HEAD_DIM)), seqs(2, GROUP_W, HEAD_DIM), seqs(2, 1, GROUP_W),
                  seqs(2, 1, GROUP_W)],
        out_specs=[seqs(seq_len, GROUP_W), seqs(2, GROUP_W, HEAD_DIM), seqs(2, 1, GROUP_W), seqs(2, 1, GROUP_W)],
        out_shape=[jax.ShapeDtypeStruct((bsz, seq_len, GROUP_W), BF16),
                   jax.ShapeDtypeStruct((bsz, 2, GROUP_W, HEAD_DIM), F32),
                   jax.ShapeDtypeStruct((bsz, 2, 1, GROUP_W), F32),
                   jax.ShapeDtypeStruct((bsz, 2, 1, GROUP_W), F32)],
        scratch_shapes=[pltpu.VMEM((ns, seq_len, GROUP_W), F32), pltpu.VMEM((2 * ns, seq_len, GROUP_W), F32),
                        pltpu.VMEM((2 * ns, seq_len, GROUP_W), F32), pltpu.VMEM((2 * ns, seq_len, GROUP_W), F32),
                        pltpu.VMEM((2 * ns, GROUP_W, GROUP_W), F32), pltpu.VMEM((2 * ns, 1, GROUP_W), F32),
                        pltpu.VMEM((2 * ns, 1, GROUP_W), F32)],
        compiler_params=pltpu.CompilerParams(dimension_semantics=("parallel",), vmem_limit_bytes=VMEM_LIMIT),
        name="mlstm",
    )(slab, _compact_row(f_bias, 1), jnp.tile(norm_g, N_HEADS)[None, :], _gate_expander(), t4, t4t, c0, n0, m0)


def _hgrn_kernel(p_ref, lb_ref, ng_ref, t4_ref, t4t_ref, s0_ref, y_ref, sout_ref,
                 k_s, gc_s, o_s, st_s, *scan_order_s, seq_len, grid_rows):
    c = HGRN_CHUNK
    n_chunks = seq_len // c
    if grid_rows is None:
        src = p_ref.at[0]
    else:
        (src,) = scan_order_s
        src[...] = _swap_row_groups(p_ref[0], grid_rows, GRID_W)
    bd_mask = _head_ones()
    bd_f = jnp.where(bd_mask, 1.0, 0.0)
    ones_bd = bd_f.astype(BF16)
    sub = _iota((c, GROUP_W), 0)
    row_sum = jnp.where(_iota((c, c * c), 1) // c == _iota((c, c * c), 0), 1.0, 0.0).astype(BF16)

    for d in range(2):
        lb = lb_ref[d:d + 1, :]
        forget = lb + (1.0 - lb) * _sigmoid(src[:, (3 + d) * GROUP_W:(4 + d) * GROUP_W])
        k_s[d] = 1.0 - forget
        gc_s[d] = _seg_cumsum(jnp.log(forget), c, reverse=(d == 1))
        s_bd = jnp.where(bd_mask, _dot_sel(s0_ref[0, d], t4_ref[...]), 0.0)
        st_s[d] = s_bd.T

    def pair_weights(d, q_c, k_c, gc):
        gc2 = gc * LOG2_E
        pieces = []
        for i in range(c):
            keep = (sub >= i) if d == 1 else (sub <= i)
            e_i = jnp.where(keep, jnp.exp2(gc2[i:i + 1, :] - gc2), 0.0)
            pieces.append(e_i * q_c[i:i + 1, :] * k_c)
        return jnp.concatenate(pieces, axis=0).astype(BF16)

    def body(n, carry):
        ds = [d for _ in range(HGRN_GROUP) for d in range(2)]
        starts = [HGRN_GROUP * n + g if d == 0 else n_chunks - 1 - HGRN_GROUP * n - g
                  for g in range(HGRN_GROUP) for d in range(2)]
        rows = [pl.ds(pl.multiple_of(s * c, c), c) for s in starts]
        q_c = [src[r, 0:GROUP_W] for r in rows]
        v_c = [src[r, GROUP_W:2 * GROUP_W] for r in rows]
        k_c = [k_s[d, r, :] for d, r in zip(ds, rows)]
        gc = [gc_s[d, r, :] for d, r in zip(ds, rows)]
        pairs =[pair_weights(d, q, k, g) for d, q, k, g in zip(ds, q_c, k_c, gc)]
        attn = [_dot(p, ones_bd) for p in pairs]
        weighted = [(a * jnp.concatenate([v] * c, axis=0)).astype(BF16) for a, v in zip(attn, v_c)]
        intra = [_dot(row_sum, w) for w in weighted]
        q_dec = [(q * jnp.exp(g)).astype(BF16) for q, g in zip(q_c, gc)]
        gc_last = [g[(0 if d == 1 else c - 1):(1 if d == 1 else c), :] for d, g in zip(ds, gc)]
        outer = [_dot_tn(v.astype(BF16), (k * jnp.exp(gl - g)).astype(BF16))
                 for v, k, gl, g in zip(v_c, k_c, gc_last, gc)]
        for i, d in enumerate(ds):
            st = st_s[d]
            o_s[d, rows[i], :] = intra[i] + _dot_nt(q_dec[i], st.astype(BF16))
            st_s[d] = st * jnp.exp(gc_last[i]) + bd_f * outer[i]
        return carry

    lax.fori_loop(0, n_chunks // HGRN_GROUP, body, 0)

    o = o_s[0] + o_s[1]
    o = o * lax.rsqrt(_seg_sum(o * o, ones_bd) * (1.0 / HEAD_DIM) + EPS) * ng_ref[...]
    y = o * _sigmoid(src[:, 2 * GROUP_W:3 * GROUP_W])
    if grid_rows is None:
        y_ref[0] = y.astype(BF16)
    else:
        y_ref[0] = _swap_row_groups(y, GRID_W, grid_rows).astype(BF16)
    for d in range(2):
        sout_ref[0, d] = _dot_sel(st_s[d].T, t4t_ref[...])


def _hgrn(slab, lb, norm_g, s0, seq_len, grid_rows):
    bsz = slab.shape[0]
    t4, t4t = _tile4()
    scan_order = [] if grid_rows is None else [pltpu.VMEM((seq_len, W_SLAB_B), F32)]
    return pl.pallas_call(
        functools.partial(_hgrn_kernel, seq_len=seq_len, grid_rows=grid_rows),
        grid=(bsz,),
        in_specs=[_seq_spec(seq_len, W_SLAB_B), _full_spec((2, GROUP_W)), _full_spec((1, GROUP_W)),
                  _full_spec((HEAD_DIM, GROUP_W)), _full_spec((GROUP_W, HEAD_DIM)), _state_spec()],
        out_specs=[_seq_spec(seq_len, GROUP_W), _state_spec()],
        out_shape=[jax.ShapeDtypeStruct((bsz, seq_len, GROUP_W), BF16),
                   jax.ShapeDtypeStruct((bsz, 2, GROUP_W, HEAD_DIM), F32)],
        scratch_shapes=[pltpu.VMEM((2, seq_len, GROUP_W), F32), pltpu.VMEM((2, seq_len, GROUP_W), F32),
                        pltpu.VMEM((2, seq_len, GROUP_W), F32), pltpu.VMEM((2, GROUP_W, GROUP_W), F32)] + scan_order,
        compiler_params=pltpu.CompilerParams(dimension_semantics=("parallel",), vmem_limit_bytes=VMEM_LIMIT),
        name="hgrn",
    )(slab, lb.astype(F32), jnp.tile(norm_g, N_HEADS)[None, :], t4, t4t, s0)


def _dft_tables(n):
    k = np.arange(n, dtype=np.float64)[:, None]
    s = np.arange(n, dtype=np.float64)[None, :]
    ang = np.pi * k * s / n
    cos_m = np.cos(ang)
    sin_m = np.sin(ang)
    alt = (-1.0) ** np.arange(n)
    sin_m[0, :] = alt
    fwd = np.concatenate([cos_m, sin_m], axis=0)
    wgt = np.full((n,), 2.0 / (2 * n))
    wgt[0] = 1.0 / (2 * n)
    inv = np.concatenate([cos_m.T * wgt[None, :], sin_m.T * wgt[None, :]], axis=1)
    out = []
    for m in (fwd, inv):
        m32 = m.astype(np.float32)
        hi = m32.astype(BF16)
        lo = (m32 - hi.astype(np.float32)).astype(BF16)
        out.append(jnp.asarray(np.concatenate([hi, lo], axis=1)))
    return out


def _dft_apply(m_ref, x):
    xb = x.astype(BF16)
    return _dot(m_ref[...], jnp.concatenate([xb, xb], axis=0))


def _dft_apply_f32(m_ref, x):
    k = x.shape[0]
    xh, xl = _split2(x)
    return _dot(m_ref[...], jnp.concatenate([xh, xh], axis=0)) + _dot(m_ref[:, 0:k], xl)


def _filter_kernel(feat_ref, env_ref, w1_ref, b1_ref, w2_ref, b2_ref, w3_ref, freq_ref, f_ref,
                   hr_ref, hs_ref, *, seq_len):
    freq = freq_ref[...]
    hid = jnp.sin(freq * (_dot3(feat_ref[...], w1_ref[...]) + b1_ref[...]))
    hid = jnp.sin(freq * (_dot3(hid, w2_ref[...]) + b2_ref[...]))
    h = _dot3(hid, w3_ref[...])
    env = env_ref[...]
    not_first = _iota((seq_len, GROUP_W), 0) > 0
    first_row = _iota((seq_len, GROUP_W), 0) == 0
    sides = []
    for o in range(HYENA_ORDER):
        pos = h[:, (2 * o) * GROUP_W:(2 * o + 1) * GROUP_W] * env
        neg = jnp.where(not_first, h[:, (2 * o + 1) * GROUP_W:(2 * o + 2) * GROUP_W] * env, 0.0)
        ss = jnp.sum(pos * pos + neg * neg, axis=0, keepdims=True)
        scale = lax.rsqrt(ss + EPS)
        sides += [(pos + neg) * scale, (pos - neg) * scale]
    spec = _dft_apply_f32(f_ref, jnp.concatenate(sides, axis=1))
    for o in range(HYENA_ORDER):
        fa = spec[:, (2 * o) * GROUP_W:(2 * o + 1) * GROUP_W]
        fb = spec[:, (2 * o + 1) * GROUP_W:(2 * o + 2) * GROUP_W]
        hr_ref[o] = fa[0:seq_len]
        hs_ref[o] = jnp.where(first_row, fa[seq_len:2 * seq_len], fb[seq_len:2 * seq_len])


def _hyena_filters(seq_len, w1, b1, w2, b2, w3, freq, tables):
    t = np.arange(seq_len, dtype=np.float64)
    tn = np.linspace(0.0, 1.0, seq_len)
    bands = np.linspace(1e-4, POS_BANDS - 1, POS_BANDS)
    ang = (2.0 * math.pi / seq_len) * t[:, None] * bands[None, :]
    feats = np.zeros((seq_len, LANES), np.float32)
    feats[:, 0:POS_DIM] = np.concatenate([tn[:, None], np.cos(ang), -np.sin(ang)], axis=-1)
    deltas = np.abs(np.linspace(math.log(1e-2) / 1.5, math.log(1e-2) / 0.3, GROUP_W))
    env = np.exp(-tn[:, None] * deltas[None, :]).astype(np.float32)
    w1p = jnp.zeros((LANES, FILTER_HIDDEN), F32).at[0:POS_DIM].set(w1)
    n_h = HYENA_ORDER * 2 * GROUP_W
    full = lambda shape: pl.BlockSpec(shape, lambda: (0,) * len(shape))
    return pl.pallas_call(
        functools.partial(_filter_kernel, seq_len=seq_len),
        in_specs=[full((seq_len, LANES)), full((seq_len, GROUP_W)), full((LANES, FILTER_HIDDEN)),
                  full((1, FILTER_HIDDEN)), full((FILTER_HIDDEN, FILTER_HIDDEN)), full((1, FILTER_HIDDEN)),
                  full((FILTER_HIDDEN, n_h)), full((1, FILTER_HIDDEN)),
                  full((2 * seq_len, 2 * seq_len))],
        out_specs=[full((HYENA_ORDER, seq_len, GROUP_W)), full((HYENA_ORDER, seq_len, GROUP_W))],
        out_shape=[jax.ShapeDtypeStruct((HYENA_ORDER, seq_len, GROUP_W), F32)] * 2,
        compiler_params=pltpu.CompilerParams(vmem_limit_bytes=VMEM_LIMIT),
        name="hyena_filter",
    )(jnp.asarray(feats), jnp.asarray(env), w1p, b1[None, :], w2, b2[None, :], w3, freq[None, :], tables[0])


def _hyena_kernel(p_ref, conv_ref, hr_ref, hs_ref, db_ref, f_ref, i_ref, y_ref, pad_s, *, seq_len):
    us = [_short_conv(pad_s, p_ref[b], conv_ref, HYENA_CONV, seq_len) for b in range(HYENA_SEQS)]
    part = lambda k: jnp.concatenate([u[:, k * GROUP_W:(k + 1) * GROUP_W] for u in us], axis=1)
    wide = lambda t: jnp.concatenate([t] * HYENA_SEQS, axis=1)
    first_row = _iota((seq_len, HYENA_SEQS * GROUP_W), 0) == 0

    def longconv(z, o):
        zf = _dft_apply(f_ref, z)
        zr = zf[0:seq_len]
        zs = zf[seq_len:2 * seq_len]
        hr = wide(hr_ref[o])
        hs = wide(hs_ref[o])
        yr = zr * hr - jnp.where(first_row, 0.0, zs * hs)
        ys = jnp.where(first_row, zs * hs, zr * hs + zs * hr)
        y = _dft_apply(i_ref, jnp.concatenate([yr, ys], axis=0))
        return y + wide(db_ref[o:o + 1, :]) * z

    y = part(2) * longconv(part(1) * longconv(part(0), 0), 1)
    for b in range(HYENA_SEQS):
        y_ref[b] = y[:, b * GROUP_W:(b + 1) * GROUP_W].astype(BF16)


def _hyena(slab, conv_w, hr, hs, dbias, tables, seq_len):
    bsz = slab.shape[0]
    once = lambda shape: pl.BlockSpec(shape, lambda b: (0,) * len(shape), pipeline_mode=pl.Buffered(1))
    in_bytes = HYENA_SEQS * seq_len * W_SLAB_D * 4
    return pl.pallas_call(
        functools.partial(_hyena_kernel, seq_len=seq_len),
        grid=(bsz // HYENA_SEQS,),
        in_specs=[pl.BlockSpec((HYENA_SEQS, seq_len, W_SLAB_D), lambda b: (b, 0, 0),
                               pipeline_mode=pl.Buffered(2 if in_bytes <= HYENA_PREFETCH_MAX_BYTES else 1)),
                  _full_spec((HYENA_CONV, 3 * GROUP_W)),
                  once((HYENA_ORDER, seq_len, GROUP_W)), once((HYENA_ORDER, seq_len, GROUP_W)),
                  _full_spec((HYENA_ORDER, GROUP_W)),
                  once((2 * seq_len, 2 * seq_len)), once((seq_len, 4 * seq_len))],
        out_specs=pl.BlockSpec((HYENA_SEQS, seq_len, GROUP_W), lambda b: (b, 0, 0)),
        out_shape=jax.ShapeDtypeStruct((bsz, seq_len, GROUP_W), BF16),
        scratch_shapes=[pltpu.VMEM((seq_len + 2 * CONV_PAD, 3 * GROUP_W), F32)],
        compiler_params=pltpu.CompilerParams(dimension_semantics=("parallel",), vmem_limit_bytes=VMEM_LIMIT),
        name="hyena",
    )(slab, conv_w, hr, hs, dbias, *tables)


def _outproj_kernel(x_ref, ya_ref, yb_ref, yc_ref, yd_ref, wo_ref, mod_ref, g_ref, rwh_ref, rwl_ref, rb_ref,
                    x1_ref, h2_ref, gates_ref):
    rows_per_part = x_ref.shape[0] // OUTPROJ_PARTS
    parts = [slice(i * rows_per_part, (i + 1) * rows_per_part) for i in range(OUTPROJ_PARTS)]
    y = [(_dot(ya_ref[p, :], wo_ref[0:GROUP_W, :]) + _dot(yb_ref[p, :], wo_ref[GROUP_W:2 * GROUP_W, :])
          + _dot(yc_ref[p, :], wo_ref[2 * GROUP_W:3 * GROUP_W, :])
          + _dot(yd_ref[p, :], wo_ref[3 * GROUP_W:4 * GROUP_W, :])) for p in parts]
    x1 = [x_ref[p, :] + mod_ref[0, 2:3, :] * yi for p, yi in zip(parts, y)]
    for p, v in zip(parts, x1):
        x1_ref[p, :] = v
    split = [_split2(_modulated_norm(v, g_ref[...], mod_ref[0, 3:4, :], mod_ref[0, 4:5, :])) for v in x1]
    for p, (hh, _) in zip(parts, split):
        h2_ref[p, :] = hh
    logits = [_dot(hh, rwh_ref[...]) + _dot(hl, rwh_ref[...]) + _dot(hh, rwl_ref[...]) for hh, hl in split]
    scores = [_sigmoid(v) for v in logits]
    vals = [s + rb_ref[...] for s in scores]
    lane = _iota(vals[0].shape, 1).astype(F32)
    picked = [jnp.zeros(v.shape, F32) for v in vals]
    for _ in range(TOP_K):
        best = [jnp.max(v, axis=-1, keepdims=True) for v in vals]
        first = [jnp.min(jnp.where(v == b, lane, float(LANES)), axis=-1, keepdims=True) for v, b in zip(vals, best)]
        hit = [lane == f for f in first]
        picked = [jnp.where(h, s, pk) for h, s, pk in zip(hit, scores, picked)]
        vals = [jnp.where(h, -jnp.inf, v) for h, v in zip(hit, vals)]
    for p, pk in zip(parts, picked):
        gates_ref[p, :] = pk / jnp.sum(pk, axis=-1, keepdims=True) * ROUTED_SCALE


def _outproj(x, ys, w_out, mod, g, router_w, router_bias, seq_len, tm):
    t = x.shape[0]
    per_seq = seq_len // tm if mod.shape[0] > 1 else None

    def mod_idx(i):
        return (i // per_seq if per_seq else 0, 0, 0)

    rw = jnp.zeros((D_MODEL, LANES), F32).at[:, 0:N_EXPERTS].set(router_w)
    rwh = rw.astype(BF16)
    rwl = (rw - rwh.astype(F32)).astype(BF16)
    rb = jnp.full((1, LANES), -jnp.inf, F32).at[0, 0:N_EXPERTS].set(router_bias.astype(F32))
    row = lambda w: pl.BlockSpec((tm, w), lambda i: (i, 0))
    fix = lambda shape: pl.BlockSpec(shape, lambda i: (0,) * len(shape))
    return pl.pallas_call(
        _outproj_kernel,
        grid=(t // tm,),
        in_specs=[row(D_MODEL), row(GROUP_W), row(GROUP_W), row(GROUP_W), row(GROUP_W),
                  fix((D_MODEL, D_MODEL)), pl.BlockSpec((1, 6, D_MODEL), mod_idx), fix((1, D_MODEL)),
                  fix((D_MODEL, LANES)), fix((D_MODEL, LANES)), fix((1, LANES))],
        out_specs=[row(D_MODEL), row(D_MODEL), row(LANES)],
        out_shape=[jax.ShapeDtypeStruct((t, D_MODEL), F32), jax.ShapeDtypeStruct((t, D_MODEL), BF16),
                   jax.ShapeDtypeStruct((t, LANES), F32)],
        compiler_params=pltpu.CompilerParams(dimension_semantics=("parallel",), vmem_limit_bytes=VMEM_LIMIT),
        name="outproj",
    )(x, *ys, w_out.astype(BF16), mod, g, rwh, rwl, rb)


def _moe_kernel(h_ref, gates_ref, x1_ref, mod_ref, wg_ref, wu_ref, wd_ref, sg_ref, su_ref, sd_ref, fg_ref,
                o_ref, acc_ref, *, final_norm):
    j = pl.program_id(1)
    h = h_ref[...]
    width = EXPERTS_PER_STEP * D_EXPERT

    @pl.when(j == 0)
    def _():
        sh = _silu(_dot(h, sg_ref[...].astype(BF16))) * _dot(h, su_ref[...].astype(BF16))
        acc_ref[...] = _dot(sh.astype(BF16), sd_ref[...].astype(BF16))

    w_gu = jnp.concatenate([wg_ref[e] for e in range(EXPERTS_PER_STEP)]
                           + [wu_ref[e] for e in range(EXPERTS_PER_STEP)], axis=1).astype(BF16)
    gu = _dot(h, w_gu)
    mine = pltpu.roll(gates_ref[...], (LANES - EXPERTS_PER_STEP * j) % LANES, axis=1)
    gate_w = jnp.concatenate([jnp.broadcast_to(mine[:, e:e + 1], (mine.shape[0], D_EXPERT))
                              for e in range(EXPERTS_PER_STEP)], axis=1)
    hid = (_silu(gu[:, 0:width]) * gu[:, width:2 * width]) * gate_w
    w_dn = jnp.concatenate([wd_ref[e] for e in range(EXPERTS_PER_STEP)], axis=0).astype(BF16)
    acc_ref[...] += _dot(hid.astype(BF16), w_dn)

    @pl.when(j == pl.num_programs(1) - 1)
    def _():
        out = x1_ref[...] + mod_ref[0, 5:6, :] * acc_ref[...]
        if final_norm:
            ms = jnp.mean(out * out, axis=-1, keepdims=True)
            out = out * lax.rsqrt(ms + EPS) * fg_ref[...]
        o_ref[...] = out


def _moe(h2, gates, x1, mod, layer, w_gate, w_up, w_down, sh_gate, sh_up, sh_down, final_g, seq_len, tm,
         final_norm):
    t = h2.shape[0]
    per_seq = max(seq_len // tm, 1) if mod.shape[0] > 1 else None
    seqs_per_tile = max(tm // seq_len, 1)
    assert mod.shape[0] == 1 or seqs_per_tile == 1

    def mod_idx(i, j):
        return (i // per_seq if per_seq else 0, 0, 0)

    row = lambda w: pl.BlockSpec((tm, w), lambda i, j: (i, 0))
    fix = lambda shape: pl.BlockSpec(shape, lambda i, j: (0,) * len(shape))
    eps_ = EXPERTS_PER_STEP
    return pl.pallas_call(
        functools.partial(_moe_kernel, final_norm=final_norm),
        grid=(t // tm, N_EXPERTS // eps_),
        in_specs=[row(D_MODEL), row(LANES), row(D_MODEL), pl.BlockSpec((1, 6, D_MODEL), mod_idx),
                  pl.BlockSpec((None, eps_, D_MODEL, D_EXPERT), lambda i, j: (layer, j, 0, 0)),
                  pl.BlockSpec((None, eps_, D_MODEL, D_EXPERT), lambda i, j: (layer, j, 0, 0)),
                  pl.BlockSpec((None, eps_, D_EXPERT, D_MODEL), lambda i, j: (layer, j, 0, 0)),
                  fix((D_MODEL, D_EXPERT)), fix((D_MODEL, D_EXPERT)), fix((D_EXPERT, D_MODEL)),
                  fix((1, D_MODEL))],
        out_specs=row(D_MODEL),
        out_shape=jax.ShapeDtypeStruct((t, D_MODEL), F32),
        scratch_shapes=[pltpu.VMEM((tm, D_MODEL), F32)],
        compiler_params=pltpu.CompilerParams(dimension_semantics=("parallel", "arbitrary"),
                                             vmem_limit_bytes=VMEM_LIMIT),
        name="moe",
    )(h2, gates, x1, mod, w_gate, w_up, w_down, sh_gate, sh_up, sh_down, final_g)


def _in_weight_slabs(w_in):
    ga = 4 * GROUP_W
    gb = ga + 4 * N_HEADS
    hb = gb + 5 * GROUP_W
    mc = hb + 4 * GROUP_W
    md = mc + 4 * N_HEADS
    pad = jnp.zeros((D_MODEL, LANES - 4 * N_HEADS), w_in.dtype)
    wa = jnp.concatenate([w_in[:, 0:gb], pad], axis=1)
    wb = w_in[:, gb:hb]
    wc = jnp.concatenate([w_in[:, hb:md], pad], axis=1)
    wd = w_in[:, md:]
    return [w.astype(BF16) for w in (wa, wb, wc, wd)]


def _layer(x, mod, states, prm, stacked, layer, lb, seq_len, grid_rows, tables, final_g, final_norm):
    bsz = x.shape[0] // seq_len
    tm = 512
    s_gdn, s_hgrn, s_c, s_n, s_m = states
    pa, pb, pc, pd = _inproj(x, mod, prm['norm1_g'][None, :], _in_weight_slabs(prm['w_in']), seq_len, tm)
    pa = pa.reshape(bsz, seq_len, W_SLAB_A)
    pb = pb.reshape(bsz, seq_len, W_SLAB_B)
    pc = pc.reshape(bsz, seq_len, W_SLAB_C)
    pd = pd.reshape(bsz, seq_len, W_SLAB_D)

    ya, st_a = _gdn(pa, prm['gdn_conv'], prm['gdn_a_log'], prm['gdn_dt_bias'], prm['gdn_norm_g'],
                    s_gdn.reshape(bsz, 2, GROUP_W, HEAD_DIM), seq_len)
    yb, st_b = _hgrn(pb, lb, prm['hgrn_norm_g'], s_hgrn.reshape(bsz, 2, GROUP_W, HEAD_DIM), seq_len, grid_rows)
    yc, st_c, st_n, st_m = _mlstm(pc, prm['mlstm_f_bias'], prm['mlstm_norm_g'],
                                  s_c.reshape(bsz, 2, GROUP_W, HEAD_DIM), s_n.reshape(bsz, 2, 1, GROUP_W),
                                  jnp.repeat(s_m, HEAD_DIM, axis=-1).reshape(bsz, 2, 1, GROUP_W), seq_len)
    hr, hs = _hyena_filters(seq_len, prm['filt_w1'], prm['filt_b1'], prm['filt_w2'], prm['filt_b2'],
                            prm['filt_w3'], prm['filt_freq'], tables)
    yd = _hyena(pd, prm['hyena_conv'], hr, hs, prm['hyena_d'], tables, seq_len)

    t = x.shape[0]
    ys = [y.reshape(t, GROUP_W) for y in (ya, yb, yc, yd)]
    x1, h2, gates = _outproj(x, ys, prm['w_out'], mod, prm['norm2_g'][None, :], prm['router_w'],
                             prm['router_bias'], seq_len, tm)
    tm_moe = 1024
    out = _moe(h2, gates, x1, mod, layer, stacked['exp_w_gate'], stacked['exp_w_up'], stacked['exp_w_down'],
               prm['sh_w_gate'], prm['sh_w_up'], prm['sh_w_down'], final_g[None, :], seq_len, tm_moe, final_norm)
    new_states = (st_a.reshape(bsz, 2, N_HEADS, HEAD_DIM, HEAD_DIM),
                  st_b.reshape(bsz, 2, N_HEADS, HEAD_DIM, HEAD_DIM),
                  st_c.reshape(bsz, 2, N_HEADS, HEAD_DIM, HEAD_DIM),
                  st_n.reshape(bsz, 2, N_HEADS, HEAD_DIM),
                  st_m.reshape(bsz, 2, N_HEADS, HEAD_DIM)[..., 0])
    return out, new_states


def kernel(x_prompt, x_sample, c, state_gdn, state_hgrn, state_mlstm_c, state_mlstm_n, state_mlstm_m, c_ctx, norm1_g, norm2_g, w_mod, b_mod, w_in, gdn_conv, gdn_a_log, gdn_dt_bias, gdn_norm_g, hgrn_lb_logits, hgrn_norm_g, mlstm_f_bias, mlstm_norm_g, hyena_conv, filt_w1, filt_b1, filt_w2, filt_b2, filt_w3, filt_freq, hyena_d, w_out, router_w, router_bias, exp_w_gate, exp_w_up, exp_w_down, sh_w_gate, sh_w_up, sh_w_down, final_g):
    depth = w_in.shape[0]
    bsz, seq, _ = x_prompt.shape
    dbsz, dseq, _ = x_sample.shape
    p = jax.nn.softmax(hgrn_lb_logits.astype(F32), axis=0)
    lower = jnp.cumsum(p, axis=0) - p[0:1]

    per_layer = dict(norm1_g=norm1_g, norm2_g=norm2_g, w_in=w_in, gdn_conv=gdn_conv, gdn_a_log=gdn_a_log,
                     gdn_dt_bias=gdn_dt_bias, gdn_norm_g=gdn_norm_g, hgrn_norm_g=hgrn_norm_g,
                     mlstm_f_bias=mlstm_f_bias, mlstm_norm_g=mlstm_norm_g, hyena_conv=hyena_conv,
                     filt_w1=filt_w1, filt_b1=filt_b1, filt_w2=filt_w2, filt_b2=filt_b2, filt_w3=filt_w3,
                     filt_freq=filt_freq, hyena_d=hyena_d, w_out=w_out, router_w=router_w,
                     router_bias=router_bias, sh_w_gate=sh_w_gate, sh_w_up=sh_w_up, sh_w_down=sh_w_down)
    stacked = dict(exp_w_gate=exp_w_gate, exp_w_up=exp_w_up, exp_w_down=exp_w_down)

    cond = jnp.zeros((N_COND_ROWS, D_MODEL), F32).at[0].set(c_ctx).at[1:1 + dbsz].set(c)
    tables_p = _dft_tables(seq)
    tables_s = _dft_tables(dseq)

    zero_states = (jnp.zeros((bsz, 2, N_HEADS, HEAD_DIM, HEAD_DIM), F32),
                   jnp.zeros((bsz, 2, N_HEADS, HEAD_DIM, HEAD_DIM), F32),
                   jnp.zeros((bsz, 2, N_HEADS, HEAD_DIM, HEAD_DIM), F32),
                   jnp.zeros((bsz, 2, N_HEADS, HEAD_DIM), F32),
                   jnp.zeros((bsz, 2, N_HEADS), F32))
    h = x_prompt.reshape(bsz * seq, D_MODEL)
    z = x_sample.reshape(dbsz * dseq, D_MODEL)
    new_states = [[] for _ in range(5)]
    for l in range(depth):
        prm = {name: val[l] for name, val in per_layer.items()}
        mod = _adaln(cond, w_mod, b_mod[l], l).reshape(N_COND_ROWS, 6, D_MODEL)
        last = l == depth - 1
        h, st = _layer(h, mod[0:1], zero_states, prm, stacked, l, lower[l], seq, None, tables_p, final_g, last)
        for acc, s in zip(new_states, st):
            acc.append(s.astype(x_prompt.dtype))
        cached = (state_gdn[:, l].astype(F32), state_hgrn[:, l].astype(F32), state_mlstm_c[:, l].astype(F32),
                  state_mlstm_n[:, l].astype(F32), state_mlstm_m[:, l].astype(F32))
        z, _ = _layer(z, mod[1:1 + dbsz], cached, prm, stacked, l, lower[l], dseq, dseq // GRID_W, tables_s,
                      final_g, last)

    return (h.reshape(bsz, seq, D_MODEL), z.reshape(dbsz, dseq, D_MODEL),
            *[jnp.stack(acc, axis=1) for acc in new_states])
```

```python
import functools
import math

import numpy as np
import jax
import jax.numpy as jnp
from jax import lax
from jax.experimental import pallas as pl
from jax.experimental.pallas import tpu as pltpu

F32 = jnp.float32
BF16 = jnp.bfloat16

D_MODEL = 1024
N_HEADS = 4
HEAD_DIM = 64
GROUP_W = N_HEADS * HEAD_DIM
GRID_W = 64
CHUNK = 64
HGRN_CHUNK = 16
MLSTM_SEQS = 2
HGRN_GROUP = 2
OUTPROJ_PARTS = 2
HYENA_PREFETCH_MAX_BYTES = 4 * 1024 * 1024
HYENA_SEQS = 2
GDN_GROUP = 4
TOKEN_TILE = 512
MOE_TOKEN_TILE = 1024
ADALN_COL_TILE = 1536
GDN_CONV = 5
HYENA_CONV = 3
HYENA_ORDER = 2
POS_BANDS = 16
POS_DIM = 1 + 2 * POS_BANDS
FILTER_HIDDEN = 64
N_EXPERTS = 64
TOP_K = 8
D_EXPERT = 128
ROUTED_SCALE = 2.5
EPS = 1e-6
LOG2_E = 1.4426950408889634
LANES = 128
CONV_PAD = 8
EXPERTS_PER_STEP = 4
N_COND_ROWS = 16
VMEM_LIMIT = 56 * 1024 * 1024

W_SLAB_A = 4 * GROUP_W + LANES
W_SLAB_B = 5 * GROUP_W
W_SLAB_C = 4 * GROUP_W + LANES
W_SLAB_D = 3 * GROUP_W


def _split2(x):
    hi = x.astype(BF16)
    lo = (x - hi.astype(F32)).astype(BF16)
    return hi, lo


def _split3(x):
    p1 = x.astype(BF16)
    r = x - p1.astype(F32)
    p2 = r.astype(BF16)
    p3 = (r - p2.astype(F32)).astype(BF16)
    return p1, p2, p3


def _dot(a, b):
    return jnp.dot(a, b, preferred_element_type=F32)


def _dot_nt(a, b):
    return lax.dot_general(a, b, (((1,), (1,)), ((), ())), preferred_element_type=F32)


def _dot_tn(a, b):
    return lax.dot_general(a, b, (((0,), (0,)), ((), ())), preferred_element_type=F32)


def _dot3(a, b):
    ah, al = _split2(a)
    bh, bl = _split2(b)
    return _dot(ah, bh) + _dot(al, bh) + _dot(ah, bl)


def _dot_sel(x, sel):
    p1, p2, p3 = _split3(x)
    return _dot(p1, sel) + _dot(p2, sel) + _dot(p3, sel)


def _sel_dot(sel, x):
    p1, p2, p3 = _split3(x)
    return _dot(sel, p1) + _dot(sel, p2) + _dot(sel, p3)


def _iota(shape, dim):
    return lax.broadcasted_iota(jnp.int32, shape, dim)


def _head_ones():
    r = _iota((GROUP_W, GROUP_W), 0) // HEAD_DIM
    c = _iota((GROUP_W, GROUP_W), 1) // HEAD_DIM
    return r == c


def _seg_sum(x, ones_bd):
    hi, lo = _split2(x)
    return _dot(hi, ones_bd) + _dot(lo, ones_bd)


def _sigmoid(x):
    return 1.0 / (1.0 + jnp.exp(-x))


def _silu(x):
    return x * _sigmoid(x)


def _softplus(x):
    return jnp.maximum(x, 0.0) + jnp.log(1.0 + jnp.exp(-jnp.abs(x)))


def _seg_cumsum(x, chunk, reverse):
    n = x.shape[0]
    pos = _iota(x.shape, 0) % chunk
    s = 1
    while s < chunk:
        if reverse:
            x = x + jnp.where(pos < chunk - s, pltpu.roll(x, n - s, axis=0), 0.0)
        else:
            x = x + jnp.where(pos >= s, pltpu.roll(x, s, axis=0), 0.0)
        s *= 2
    return x


def _swap_row_groups(x, a, b):
    n, ch = x.shape
    return jnp.swapaxes(x.reshape(a, b, ch), 0, 1).reshape(n, ch)


def _dir_cumsum(x, chunk, lane):
    backward = (lane // N_HEADS) % 2 == 1
    return jnp.where(backward, _seg_cumsum(x, chunk, True), _seg_cumsum(x, chunk, False))


def _short_conv(pad_ref, x, w_ref, width, n):
    ch = x.shape[1]
    pad_ref[0:CONV_PAD, :] = jnp.zeros((CONV_PAD, ch), F32)
    pad_ref[CONV_PAD + n:2 * CONV_PAD + n, :] = jnp.zeros((CONV_PAD, ch), F32)
    pad_ref[CONV_PAD:CONV_PAD + n, :] = x
    acc = None
    for j in range(width):
        off = CONV_PAD - width // 2 + j
        term = pad_ref[off:off + n, :] * w_ref[j:j + 1, :]
        acc = term if acc is None else acc + term
    return acc


def _chunk_masks(reverse):
    r = _iota((CHUNK, GROUP_W), 0)
    j = _iota((CHUNK, GROUP_W), 1) % HEAD_DIM
    if reverse:
        return j >= r, j > r, j == r
    return j <= r, j < r, j == r


def _adaln_kernel(c_ref, w_ref, b_ref, o_ref):
    c = c_ref[...]
    o_ref[...] = _dot3(_silu(c), w_ref[...]) + b_ref[...]


def _adaln(cond, w_mod, b_mod, layer):
    n_out = w_mod.shape[2]
    tn = ADALN_COL_TILE
    return pl.pallas_call(
        _adaln_kernel,
        grid=(n_out // tn,),
        in_specs=[pl.BlockSpec((N_COND_ROWS, D_MODEL), lambda j: (0, 0)),
                  pl.BlockSpec((None, D_MODEL, tn), lambda j: (layer, 0, j)),
                  pl.BlockSpec((1, tn), lambda j: (0, j))],
        out_specs=pl.BlockSpec((N_COND_ROWS, tn), lambda j: (0, j)),
        out_shape=jax.ShapeDtypeStruct((N_COND_ROWS, n_out), F32),
        compiler_params=pltpu.CompilerParams(dimension_semantics=("arbitrary",), vmem_limit_bytes=VMEM_LIMIT),
        name="adaln",
    )(cond, w_mod, b_mod.reshape(1, n_out))


def _modulated_norm(x, g, shift, scale):
    ms = jnp.mean(x * x, axis=-1, keepdims=True)
    return (x * lax.rsqrt(ms + EPS) * g) * (1.0 + scale) + shift


def _inproj_kernel(x_ref, mod_ref, g_ref, wa_ref, wb_ref, wc_ref, wd_ref, oa_ref, ob_ref, oc_ref, od_ref):
    h = _modulated_norm(x_ref[...], g_ref[...], mod_ref[0, 0:1, :], mod_ref[0, 1:2, :]).astype(BF16)
    oa_ref[...] = _dot(h, wa_ref[...])
    ob_ref[...] = _dot(h, wb_ref[...])
    oc_ref[...] = _dot(h, wc_ref[...])
    od_ref[...] = _dot(h, wd_ref[...])


def _inproj(x, mod, g, w_slabs, seq_len, tm):
    t = x.shape[0]
    per_seq = seq_len // tm if mod.shape[0] > 1 else None

    def mod_idx(i):
        return (i // per_seq if per_seq else 0, 0, 0)

    widths = (W_SLAB_A, W_SLAB_B, W_SLAB_C, W_SLAB_D)
    return pl.pallas_call(
        _inproj_kernel,
        grid=(t // tm,),
        in_specs=[pl.BlockSpec((tm, D_MODEL), lambda i: (i, 0)),
                  pl.BlockSpec((1, 6, D_MODEL), mod_idx),
                  pl.BlockSpec((1, D_MODEL), lambda i: (0, 0))]
                 + [pl.BlockSpec((D_MODEL, w), lambda i: (0, 0)) for w in widths],
        out_specs=[pl.BlockSpec((tm, w), lambda i: (i, 0)) for w in widths],
        out_shape=[jax.ShapeDtypeStruct((t, w), F32) for w in widths],
        compiler_params=pltpu.CompilerParams(dimension_semantics=("parallel",), vmem_limit_bytes=VMEM_LIMIT),
        name="inproj",
    )(x, mod, g, *w_slabs)


def _chunk_masks_f32(reverse):
    return tuple(jnp.where(m, 1.0, 0.0) for m in _chunk_masks(reverse))


def _blockdiag_mul(x, bd_factor):
    return jnp.concatenate([x] * N_HEADS, axis=0) * bd_factor


def _tri_inverse(lms, eyes, same16, same32, bd_b, fillers):
    pending = list(fillers)

    def prod(xs, ys):
        if pending:
            pending.pop(0)()
        staged = []
        for a, b in zip(xs, ys):
            ah, al = _split2(a)
            staged.append((jnp.concatenate([ah, al], axis=0), _blockdiag_mul(b.astype(BF16), bd_b)))
        outs = []
        for a_hl, b_bd in staged:
            n = a_hl.shape[0] // 2
            hl = _dot(a_hl, b_bd)
            outs.append(hl[0:n] + hl[n:2 * n])
        return outs

    ms = [-(lm * same16) for lm in lms]
    ps = [eye + m for eye, m in zip(eyes, ms)]
    ms = prod(ms, ms)
    for _ in range(2):
        pms = prod([jnp.concatenate([p, m], axis=0) for p, m in zip(ps, ms)], ms)
        ps = [p + pm[0:CHUNK] for p, pm in zip(ps, pms)]
        ms = [pm[CHUNK:2 * CHUNK] for pm in pms]
    ps = [p + pm for p, pm in zip(ps, prod(ps, ms))]
    for keep in (same32 - same16, 1.0 - same32):
        offs = [lm * keep for lm in lms]
        ps = [p - c for p, c in zip(ps, prod(ps, prod(offs, ps)))]
    for rest in pending:
        rest()
    return ps


def _gdn_kernel(p_ref, conv_ref, alog_ref, dtb_ref, ng_ref, ea_ref, t4_ref, t4t_ref, s0_ref,
                y_ref, sout_ref,
                q_s, k_s, v_s, gc_s, beta_s, u_s, o_s, w_s, qe_s, attn_s, kdec_s, st_s, pad_s, *, seq_len):
    n_chunks = seq_len // CHUNK
    bd_f = jnp.where(_head_ones(), 1.0, 0.0)
    bd_b = bd_f.astype(BF16)
    ones_bd = bd_b

    qkv = _silu(_short_conv(pad_s, p_ref[0, :, 0:3 * GROUP_W], conv_ref, GDN_CONV, seq_len))
    q = qkv[:, 0:GROUP_W]
    k = qkv[:, GROUP_W:2 * GROUP_W]
    q_s[...] = q * lax.rsqrt(_seg_sum(q * q, ones_bd) + EPS) * (HEAD_DIM ** -0.5)
    k_s[...] = k * lax.rsqrt(_seg_sum(k * k, ones_bd) + EPS)
    v_s[...] = qkv[:, 2 * GROUP_W:3 * GROUP_W]

    raw = p_ref[0, :, 4 * GROUP_W:4 * GROUP_W + LANES]
    lane = _iota(raw.shape, 1)
    log_decay = -jnp.exp(alog_ref[...]) * _softplus(raw + dtb_ref[...])
    compact = jnp.where(lane < 2 * N_HEADS, _dir_cumsum(log_decay, CHUNK, lane),
                        jnp.where(lane < N_GATE_LANES, _sigmoid(raw), 0.0))
    gcb = _expand_gates(compact, ea_ref[...])
    for d in range(2):
        gc_s[d] = gcb[:, d * GROUP_W:(d + 1) * GROUP_W]
        beta_s[d] = gcb[:, (2 + d) * GROUP_W:(3 + d) * GROUP_W]
        st_s[d] = bd_f * _dot_sel(s0_ref[0, d], t4_ref[...])

    masks = [_chunk_masks_f32(False), _chunk_masks_f32(True)]
    r = _iota((CHUNK, GROUP_W), 0)
    j = _iota((CHUNK, GROUP_W), 1) % HEAD_DIM
    same16 = jnp.where((r // 16) == (j // 16), 1.0, 0.0)
    same32 = jnp.where((r // 32) == (j // 32), 1.0, 0.0)
    ones_cc = jnp.ones((CHUNK, CHUNK), BF16)

    n_groups = n_chunks // GDN_GROUP

    def scan_step(n):
        r0 = [pl.multiple_of(n * CHUNK, CHUNK), pl.multiple_of((n_chunks - 1 - n) * CHUNK, CHUNK)]
        rows = [pl.ds(r, CHUNK) for r in r0]
        last = [CHUNK - 1, 0]
        s_bd = [st_s[d] for d in range(2)]
        ws_qs = [_dot(jnp.concatenate([w_s[d, rows[d], :], qe_s[d, rows[d], :]], axis=0), s_bd[d].astype(BF16))
                 for d in range(2)]
        v_new = [(u_s[d, rows[d], :] - ws_qs[d][0:CHUNK]).astype(BF16) for d in range(2)]
        intra = [_dot(attn_s[d, rows[d], :], _blockdiag_mul(v_new[d], bd_b)) for d in range(2)]
        outer = [_dot_tn(kdec_s[d, rows[d], :], v_new[d]) for d in range(2)]
        for d in range(2):
            o_s[d, rows[d], :] = ws_qs[d][CHUNK:2 * CHUNK] + intra[d]
            g_last = jnp.exp(gc_s[d, pl.ds(r0[d] + last[d], 1), :])
            st_s[d] = s_bd[d] * g_last + bd_f * outer[d]

    def prepare_group(t, fillers):
        ds = [d for _ in range(GDN_GROUP) for d in range(2)]
        rows = [pl.ds(pl.multiple_of((GDN_GROUP * t + c if d == 0 else n_chunks - GDN_GROUP * (t + 1) + c) * CHUNK,
                                     CHUNK), CHUNK) for c in range(GDN_GROUP) for d in range(2)]
        q_c = [q_s[r, :] for r in rows]
        k_c = [k_s[r, :] for r in rows]
        gc = [gc_s[d, r, :] for d, r in zip(ds, rows)]
        beta = [beta_s[d, r, :] for d, r in zip(ds, rows)]
        gc_row = [_sel_dot(ones_cc, g * masks[d][2]) for d, g in zip(ds, gc)]
        decay = [masks[d][0] * jnp.exp(jnp.minimum(g - gr, 0.0)) for d, g, gr in zip(ds, gc, gc_row)]
        kb = [k * b for k, b in zip(k_c, beta)]
        kq = [_dot_nt(jnp.concatenate([kbi, qi], axis=0).astype(BF16), _blockdiag_mul(ki.astype(BF16), bd_b))
              for kbi, qi, ki in zip(kb, q_c, k_c)]
        tmat = _tri_inverse([kqi[0:CHUNK] * (dec * masks[d][1]) for d, kqi, dec in zip(ds, kq, decay)],
                            [masks[d][2] for d in ds], same16, same32, bd_b, fillers)
        egc = [jnp.exp(g) for g in gc]
        rhs = [jnp.concatenate([_blockdiag_mul((v_s[r, :] * b).astype(BF16), bd_b),
                                _blockdiag_mul((kbi * e).astype(BF16), bd_b)], axis=1)
               for r, b, kbi, e in zip(rows, beta, kb, egc)]
        uw = [_dot(t.astype(BF16), x) for t, x in zip(tmat, rhs)]
        for i, (d, r) in enumerate(zip(ds, rows)):
            last = 0 if d == 1 else CHUNK - 1
            u_s[d, r, :] = uw[i][:, 0:GROUP_W]
            w_s[d, r, :] = uw[i][:, GROUP_W:2 * GROUP_W].astype(BF16)
            qe_s[d, r, :] = (q_c[i] * egc[i]).astype(BF16)
            attn_s[d, r, :] = (kq[i][CHUNK:2 * CHUNK] * decay[i]).astype(BF16)
            kdec_s[d, r, :] = (k_c[i] * jnp.exp(gc[i][last:last + 1, :] - gc[i])).astype(BF16)

    prepare_group(0, [])

    def pipelined(t, carry):
        prepare_group(t, [functools.partial(scan_step, GDN_GROUP * (t - 1) + k) for k in range(GDN_GROUP)])
        return carry

    lax.fori_loop(1, n_groups, pipelined, 0)
    for k in range(GDN_GROUP):
        scan_step(GDN_GROUP * (n_groups - 1) + k)

    o = o_s[0] + o_s[1]
    o = o * lax.rsqrt(_seg_sum(o * o, ones_bd) * (1.0 / HEAD_DIM) + EPS) * ng_ref[...]
    y_ref[0] = (o * _silu(p_ref[0, :, 3 * GROUP_W:4 * GROUP_W])).astype(BF16)
    for d in range(2):
        sout_ref[0, d] = _dot_sel(st_s[d], t4t_ref[...])


def _compact_row(m, kind):
    row = jnp.zeros((1, LANES), F32)
    return row.at[0, kind * 2 * N_HEADS:(kind + 1) * 2 * N_HEADS].set(m.astype(F32).reshape(2 * N_HEADS))


N_GATE_LANES = 4 * N_HEADS


def _gate_expander():
    sel = np.zeros((LANES, 4 * GROUP_W), np.float32)
    for piece in range(3):
        for kind in range(2):
            for d in range(2):
                for h in range(N_HEADS):
                    c0 = (kind * 2 + d) * GROUP_W + h * HEAD_DIM
                    sel[piece * N_GATE_LANES + kind * 2 * N_HEADS + d * N_HEADS + h, c0:c0 + HEAD_DIM] = 1.0
    return jnp.asarray(sel, BF16)


def _expand_gates(g, sel):
    p1 = g.astype(BF16).astype(F32)
    r1 = g - p1
    p2 = r1.astype(BF16).astype(F32)
    packed = p1 + pltpu.roll(p2, N_GATE_LANES, axis=1) + pltpu.roll(r1 - p2, 2 * N_GATE_LANES, axis=1)
    return _dot(packed.astype(BF16), sel)


def _tile4():
    t = np.concatenate([np.eye(HEAD_DIM, dtype=np.float32)] * N_HEADS, axis=1)
    return jnp.asarray(t, BF16), jnp.asarray(t.T, BF16)


def _seq_spec(seq_len, width):
    return pl.BlockSpec((1, seq_len, width), lambda b: (b, 0, 0))


def _full_spec(shape):
    nd = len(shape)
    return pl.BlockSpec(shape, lambda b: (0,) * nd)


def _state_spec():
    return pl.BlockSpec((1, 2, GROUP_W, HEAD_DIM), lambda b: (b, 0, 0, 0))


def _gdn(slab, conv_w, a_log, dt_bias, norm_g, s0, seq_len):
    assert seq_len % (CHUNK * GDN_GROUP) == 0
    bsz = slab.shape[0]
    t4, t4t = _tile4()
    return pl.pallas_call(
        functools.partial(_gdn_kernel, seq_len=seq_len),
        grid=(bsz,),
        in_specs=[_seq_spec(seq_len, W_SLAB_A), _full_spec((GDN_CONV, 3 * GROUP_W)), _full_spec((1, LANES)),
                  _full_spec((1, LANES)), _full_spec((1, GROUP_W)), _full_spec((LANES, 4 * GROUP_W)),
                  _full_spec((HEAD_DIM, GROUP_W)), _full_spec((GROUP_W, HEAD_DIM)), _state_spec()],
        out_specs=[_seq_spec(seq_len, GROUP_W), _state_spec()],
        out_shape=[jax.ShapeDtypeStruct((bsz, seq_len, GROUP_W), BF16),
                   jax.ShapeDtypeStruct((bsz, 2, GROUP_W, HEAD_DIM), F32)],
        scratch_shapes=[pltpu.VMEM((seq_len, GROUP_W), F32)] * 3
                       + [pltpu.VMEM((2, seq_len, GROUP_W), F32)] * 4
                       + [pltpu.VMEM((2, seq_len, GROUP_W), BF16)] * 4
                       + [pltpu.VMEM((2, GROUP_W, GROUP_W), F32),
                          pltpu.VMEM((seq_len + 2 * CONV_PAD, 3 * GROUP_W), F32)],
        compiler_params=pltpu.CompilerParams(dimension_semantics=("parallel",), vmem_limit_bytes=VMEM_LIMIT),
        name="gdn",
    )(slab, conv_w, _compact_row(a_log, 0), _compact_row(dt_bias, 0), jnp.tile(norm_g, N_HEADS)[None, :],
      _gate_expander(), t4, t4t, s0)


def _mlstm_kernel(p_ref, fb_ref, ng_ref, eg_ref, t4_ref, t4t_ref, c0_ref, n0_ref, m0_ref,
                  y_ref, cout_ref, nout_ref, mout_ref,
                  k_s, bc_s, ig_s, h_s, c_s, n_s, m_s, *, seq_len):
    n_chunks = seq_len // CHUNK
    bd_mask = _head_ones()
    bd_f = jnp.where(bd_mask, 1.0, 0.0)
    bd_b = bd_f.astype(BF16)
    ones_bd = bd_b
    lane_head = _iota((CHUNK, GROUP_W), 1) // HEAD_DIM
    chains = [(s, d) for s in range(MLSTM_SEQS) for d in range(2)]

    for s in range(MLSTM_SEQS):
        k_s[s] = p_ref[s, :, GROUP_W:2 * GROUP_W] * (HEAD_DIM ** -0.5)
        raw = p_ref[s, :, 4 * GROUP_W:4 * GROUP_W + LANES]
        lane = _iota(raw.shape, 1)
        logf = -_softplus(-(raw + fb_ref[...]))
        compact = jnp.where(lane < 2 * N_HEADS, raw,
                            jnp.where(lane < N_GATE_LANES, _dir_cumsum(logf, CHUNK, lane), 0.0))
        gates = _expand_gates(compact, eg_ref[...])
        for d in range(2):
            ig_s[2 * s + d] = gates[:, d * GROUP_W:(d + 1) * GROUP_W]
            bc_s[2 * s + d] = gates[:, (2 + d) * GROUP_W:(3 + d) * GROUP_W]
            c_s[2 * s + d] = jnp.where(bd_mask, _dot_sel(c0_ref[s, d], t4_ref[...]), 0.0)
            n_s[2 * s + d] = n0_ref[s, d]
            m_s[2 * s + d] = m0_ref[s, d]

    masks = [_chunk_masks(False), _chunk_masks(True)]
    ones_cc = jnp.ones((CHUNK, CHUNK), BF16)

    def seg_max(dm):
        dmax = jnp.zeros((CHUNK, GROUP_W), F32)
        for h in range(N_HEADS):
            in_head = lane_head == h
            mh = jnp.max(jnp.where(in_head, dm, -jnp.inf), axis=1, keepdims=True)
            dmax = jnp.where(in_head, mh, dmax)
        return dmax

    def body(n, carry):
        last = [CHUNK - 1 if d == 0 else 0 for _, d in chains]
        fwd_rows = pl.ds(pl.multiple_of(n * CHUNK, CHUNK), CHUNK)
        bwd_rows = pl.ds(pl.multiple_of((n_chunks - 1 - n) * CHUNK, CHUNK), CHUNK)
        rows = [bwd_rows if d == 1 else fwd_rows for _, d in chains]
        ids = range(len(chains))
        q_c = [p_ref[s, rows[i], 0:GROUP_W] for i, (s, _) in enumerate(chains)]
        k_c = [k_s[s, rows[i], :] for i, (s, _) in enumerate(chains)]
        v_b = [p_ref[s, rows[i], 2 * GROUP_W:3 * GROUP_W].astype(BF16) for i, (s, _) in enumerate(chains)]
        bc = [bc_s[i, rows[i], :] for i in ids]
        x = [ig_s[i, rows[i], :] - bc[i] for i in ids]
        x_row = [_sel_dot(ones_cc, jnp.where(masks[d][2], x[i], 0.0))
                 for i, (_, d) in enumerate(chains)]
        dm = [jnp.where(masks[d][0], bc[i] + x_row[i], -jnp.inf) for i, (_, d) in enumerate(chains)]
        dmax = [seg_max(v) for v in dm]
        a = [bc[i] + m_s[i] for i in ids]
        m_t = [jnp.maximum(a[i], dmax[i]) for i in ids]
        inter = [jnp.exp(a[i] - m_t[i]) for i in ids]
        q_b = [q.astype(BF16) for q in q_c]
        qk = [_dot_nt(q_b[i], _blockdiag_mul(k_c[i].astype(BF16), bd_b)) for i in ids]
        w = [(jnp.exp(dm[i] - m_t[i]) * qk[i]).astype(BF16) for i in ids]
        c_bd = [c_s[i] for i in ids]
        n_row = [n_s[i] for i in ids]
        q_cm = [_dot(q_b[i], c_bd[i].astype(BF16)) for i in ids]
        q_n = [_seg_sum(q_c[i] * n_row[i], ones_bd) for i in ids]
        wv = [_dot(w[i], jnp.concatenate([_blockdiag_mul(v_b[i], bd_b), ones_bd], axis=1)) for i in ids]
        m_new = [m_t[i][last[i]:last[i] + 1, :] for i in ids]
        carry_w = [jnp.exp(a[i][last[i]:last[i] + 1, :] - m_new[i]) for i in ids]
        kw = [k_c[i] * jnp.exp(bc[i][last[i]:last[i] + 1, :] + x[i] - m_new[i]) for i in ids]
        outer = [_dot_tn(kw[i].astype(BF16), v_b[i]) for i in ids]
        for i in ids:
            num = inter[i] * q_cm[i] + wv[i][:, 0:GROUP_W]
            den = inter[i] * q_n[i] + wv[i][:, GROUP_W:2 * GROUP_W]
            h_s[i, rows[i], :] = num / jnp.maximum(jnp.abs(den), jnp.exp(-m_t[i]))
            c_s[i] = carry_w[i] * c_bd[i] + bd_f * outer[i]
            n_s[i] = carry_w[i] * n_row[i] + jnp.sum(kw[i], axis=0, keepdims=True)
            m_s[i] = m_new[i]
        return carry

    lax.fori_loop(0, n_chunks, body, 0)

    for s in range(MLSTM_SEQS):
        h = h_s[2 * s] + h_s[2 * s + 1]
        h = h * lax.rsqrt(_seg_sum(h * h, ones_bd) * (1.0 / HEAD_DIM) + EPS) * ng_ref[...]
        y_ref[s] = (h * _sigmoid(p_ref[s, :, 3 * GROUP_W:4 * GROUP_W])).astype(BF16)
        for d in range(2):
            cout_ref[s, d] = _dot_sel(c_s[2 * s + d], t4t_ref[...])
            nout_ref[s, d] = n_s[2 * s + d]
            mout_ref[s, d] = m_s[2 * s + d]


def _mlstm(slab, f_bias, norm_g, c0, n0, m0, seq_len):
    bsz = slab.shape[0]
    ns = MLSTM_SEQS
    assert bsz % ns == 0
    seqs = lambda *tail: pl.BlockSpec((ns,) + tail, lambda b: (b,) + (0,) * len(tail))
    t4, t4t = _tile4()
    return pl.pallas_call(
        functools.partial(_mlstm_kernel, seq_len=seq_len),
        grid=(bsz // ns,),
        in_specs=[seqs(seq_len, W_SLAB_C), _full_spec((1, LANES)), _full_spec((1, GROUP_W)),
                  _full_spec((LANES, 4 * GROUP_W)), _full_spec((HEAD_DIM, GROUP_W)),
                  _full_spec((GROUP_W, HEAD_DIM)), seqs(2, GROUP_W, HEAD_DIM), seqs(2, 1, GROUP_W),
                  seqs(2, 1, GROUP_W)],
        out_specs=[seqs(seq_len, GROUP_W), seqs(2, GROUP_W, HEAD_DIM), seqs(2, 1, GROUP_W), seqs(2, 1, GROUP_W)],
        out_shape=[jax.ShapeDtypeStruct((bsz, seq_len, GROUP_W), BF16),
                   jax.ShapeDtypeStruct((bsz, 2, GROUP_W, HEAD_DIM), F32),
                   jax.ShapeDtypeStruct((bsz, 2, 1, GROUP_W), F32),
                   jax.ShapeDtypeStruct((bsz, 2, 1, GROUP_W), F32)],
        scratch_shapes=[pltpu.VMEM((ns, seq_len, GROUP_W), F32), pltpu.VMEM((2 * ns, seq_len, GROUP_W), F32),
                        pltpu.VMEM((2 * ns, seq_len, GROUP_W), F32), pltpu.VMEM((2 * ns, seq_len, GROUP_W), F32),
                        pltpu.VMEM((2 * ns, GROUP_W, GROUP_W), F32), pltpu.VMEM((2 * ns, 1, GROUP_W), F32),
                        pltpu.VMEM((2 * ns, 1, GROUP_W), F32)],
        compiler_params=pltpu.CompilerParams(dimension_semantics=("parallel",), vmem_limit_bytes=VMEM_LIMIT),
        name="mlstm",
    )(slab, _compact_row(f_bias, 1), jnp.tile(norm_g, N_HEADS)[None, :], _gate_expander(), t4, t4t, c0, n0, m0)


def _hgrn_kernel(p_ref, lb_ref, ng_ref, t4_ref, t4t_ref, s0_ref, y_ref, sout_ref,
                 k_s, gc_s, o_s, st_s, *scan_order_s, seq_len, grid_rows):
    c = HGRN_CHUNK
    n_chunks = seq_len // c
    if grid_rows is None:
        src = p_ref.at[0]
    else:
        (src,) = scan_order_s
        src[...] = _swap_row_groups(p_ref[0], grid_rows, GRID_W)
    bd_mask = _head_ones()
    bd_f = jnp.where(bd_mask, 1.0, 0.0)
    ones_bd = bd_f.astype(BF16)
    sub = _iota((c, GROUP_W), 0)
    row_sum = jnp.where(_iota((c, c * c), 1) // c == _iota((c, c * c), 0), 1.0, 0.0).astype(BF16)

    for d in range(2):
        lb = lb_ref[d:d + 1, :]
        forget = lb + (1.0 - lb) * _sigmoid(src[:, (3 + d) * GROUP_W:(4 + d) * GROUP_W])
        k_s[d] = 1.0 - forget
        gc_s[d] = _seg_cumsum(jnp.log(forget), c, reverse=(d == 1))
        s_bd = jnp.where(bd_mask, _dot_sel(s0_ref[0, d], t4_ref[...]), 0.0)
        st_s[d] = s_bd.T

    def pair_weights(d, q_c, k_c, gc):
        gc2 = gc * LOG2_E
        key_side = gc2 - jnp.log(jnp.maximum(k_c, 0.0)) * LOG2_E
        pieces = []
        for i in range(c):
            keep = (sub >= i) if d == 1 else (sub <= i)
            pieces.append(jnp.where(keep, jnp.exp2(gc2[i:i + 1, :] - key_side), 0.0) * q_c[i:i + 1, :])
        return jnp.concatenate(pieces, axis=0).astype(BF16)

    def body(n, carry):
        ds = [d for _ in range(HGRN_GROUP) for d in range(2)]
        starts = [HGRN_GROUP * n + g if d == 0 else n_chunks - 1 - HGRN_GROUP * n - g
                  for g in range(HGRN_GROUP) for d in range(2)]
        rows = [pl.ds(pl.multiple_of(s * c, c), c) for s in starts]
        q_c = [src[r, 0:GROUP_W] for r in rows]
        v_c = [src[r, GROUP_W:2 * GROUP_W] for r in rows]
        k_c = [k_s[d, r, :] for d, r in zip(ds, rows)]
        gc = [gc_s[d, r, :] for d, r in zip(ds, rows)]
        pairs = [pair_weights(d, q, k, g) for d, q, k, g in zip(ds, q_c, k_c, gc)]
        attn = [_dot(p, ones_bd) for p in pairs]
        weighted = [(a * jnp.concatenate([v] * c, axis=0)).astype(BF16) for a, v in zip(attn, v_c)]
        intra = [_dot(row_sum, w) for w in weighted]
        q_dec = [(q * jnp.exp(g)).astype(BF16) for q, g in zip(q_c, gc)]
        gc_last = [g[(0 if d == 1 else c - 1):(1 if d == 1 else c), :] for d, g in zip(ds, gc)]
        outer = [_dot_tn(v.astype(BF16), (k * jnp.exp(gl - g)).astype(BF16))
                 for v, k, gl, g in zip(v_c, k_c, gc_last, gc)]
        for i, d in enumerate(ds):
            st = st_s[d]
            o_s[d, rows[i], :] = intra[i] + _dot_nt(q_dec[i], st.astype(BF16))
            st_s[d] = st * jnp.exp(gc_last[i]) + bd_f * outer[i]
        return carry

    lax.fori_loop(0, n_chunks // HGRN_GROUP, body, 0)

    o = o_s[0] + o_s[1]
    o = o * lax.rsqrt(_seg_sum(o * o, ones_bd) * (1.0 / HEAD_DIM) + EPS) * ng_ref[...]
    y = o * _sigmoid(src[:, 2 * GROUP_W:3 * GROUP_W])
    if grid_rows is None:
        y_ref[0] = y.astype(BF16)
    else:
        y_ref[0] = _swap_row_groups(y, GRID_W, grid_rows).astype(BF16)
    for d in range(2):
        sout_ref[0, d] = _dot_sel(st_s[d].T, t4t_ref[...])


def _hgrn(slab, lb, norm_g, s0, seq_len, grid_rows):
    bsz = slab.shape[0]
    t4, t4t = _tile4()
    scan_order = [] if grid_rows is None else [pltpu.VMEM((seq_len, W_SLAB_B), F32)]
    return pl.pallas_call(
        functools.partial(_hgrn_kernel, seq_len=seq_len, grid_rows=grid_rows),
        grid=(bsz,),
        in_specs=[_seq_spec(seq_len, W_SLAB_B), _full_spec((2, GROUP_W)), _full_spec((1, GROUP_W)),
                  _full_spec((HEAD_DIM, GROUP_W)), _full_spec((GROUP_W, HEAD_DIM)), _state_spec()],
        out_specs=[_seq_spec(seq_len, GROUP_W), _state_spec()],
        out_shape=[jax.ShapeDtypeStruct((bsz, seq_len, GROUP_W), BF16),
                   jax.ShapeDtypeStruct((bsz, 2, GROUP_W, HEAD_DIM), F32)],
        scratch_shapes=[pltpu.VMEM((2, seq_len, GROUP_W), F32), pltpu.VMEM((2, seq_len, GROUP_W), F32),
                        pltpu.VMEM((2, seq_len, GROUP_W), F32), pltpu.VMEM((2, GROUP_W, GROUP_W), F32)] + scan_order,
        compiler_params=pltpu.CompilerParams(dimension_semantics=("parallel",), vmem_limit_bytes=VMEM_LIMIT),
        name="hgrn",
    )(slab, lb.astype(F32), jnp.tile(norm_g, N_HEADS)[None, :], t4, t4t, s0)


def _dft_tables(n):
    k = np.arange(n, dtype=np.float64)[:, None]
    s = np.arange(n, dtype=np.float64)[None, :]
    ang = np.pi * k * s / n
    cos_m = np.cos(ang)
    sin_m = np.sin(ang)
    alt = (-1.0) ** np.arange(n)
    sin_m[0, :] = alt
    fwd = np.concatenate([cos_m, sin_m], axis=0)
    wgt = np.full((n,), 2.0 / (2 * n))
    wgt[0] = 1.0 / (2 * n)
    inv = np.concatenate([cos_m.T * wgt[None, :], sin_m.T * wgt[None, :]], axis=1)
    out = []
    for m in (fwd, inv):
        m32 = m.astype(np.float32)
        hi = m32.astype(BF16)
        lo = (m32 - hi.astype(np.float32)).astype(BF16)
        out.append(jnp.asarray(np.concatenate([hi, lo], axis=1)))
    return out


def _dft_apply(m_ref, x):
    xb = x.astype(BF16)
    return _dot(m_ref[...], jnp.concatenate([xb, xb], axis=0))


def _dft_apply_f32(m_ref, x):
    k = x.shape[0]
    xh, xl = _split2(x)
    return _dot(m_ref[...], jnp.concatenate([xh, xh], axis=0)) + _dot(m_ref[:, 0:k], xl)


def _filter_kernel(feat_ref, env_ref, w1_ref, b1_ref, w2_ref, b2_ref, w3_ref, freq_ref, f_ref,
                   hr_ref, hs_ref, *, seq_len):
    freq = freq_ref[...]
    hid = jnp.sin(freq * (_dot3(feat_ref[...], w1_ref[...]) + b1_ref[...]))
    hid = jnp.sin(freq * (_dot3(hid, w2_ref[...]) + b2_ref[...]))
    h = _dot3(hid, w3_ref[...])
    env = env_ref[...]
    not_first = _iota((seq_len, GROUP_W), 0) > 0
    first_row = _iota((seq_len, GROUP_W), 0) == 0
    sides = []
    for o in range(HYENA_ORDER):
        pos = h[:, (2 * o) * GROUP_W:(2 * o + 1) * GROUP_W] * env
        neg = jnp.where(not_first, h[:, (2 * o + 1) * GROUP_W:(2 * o + 2) * GROUP_W] * env, 0.0)
        ss = jnp.sum(pos * pos + neg * neg, axis=0, keepdims=True)
        scale = lax.rsqrt(ss + EPS)
        sides += [(pos + neg) * scale, (pos - neg) * scale]
    spec = _dft_apply_f32(f_ref, jnp.concatenate(sides, axis=1))
    for o in range(HYENA_ORDER):
        fa = spec[:, (2 * o) * GROUP_W:(2 * o + 1) * GROUP_W]
        fb = spec[:, (2 * o + 1) * GROUP_W:(2 * o + 2) * GROUP_W]
        hr_ref[o] = fa[0:seq_len]
        hs_ref[o] = jnp.where(first_row, fa[seq_len:2 * seq_len], fb[seq_len:2 * seq_len])


def _hyena_filters(seq_len, w1, b1, w2, b2, w3, freq, tables):
    t = np.arange(seq_len, dtype=np.float64)
    tn = np.linspace(0.0, 1.0, seq_len)
    bands = np.linspace(1e-4, POS_BANDS - 1, POS_BANDS)
    ang = (2.0 * math.pi / seq_len) * t[:, None] * bands[None, :]
    feats = np.zeros((seq_len, LANES), np.float32)
    feats[:, 0:POS_DIM] = np.concatenate([tn[:, None], np.cos(ang), -np.sin(ang)], axis=-1)
    deltas = np.abs(np.linspace(math.log(1e-2) / 1.5, math.log(1e-2) / 0.3, GROUP_W))
    env = np.exp(-tn[:, None] * deltas[None, :]).astype(np.float32)
    w1p = jnp.zeros((LANES, FILTER_HIDDEN), F32).at[0:POS_DIM].set(w1)
    n_h = HYENA_ORDER * 2 * GROUP_W
    full = lambda shape: pl.BlockSpec(shape, lambda: (0,) * len(shape))
    return pl.pallas_call(
        functools.partial(_filter_kernel, seq_len=seq_len),
        in_specs=[full((seq_len, LANES)), full((seq_len, GROUP_W)), full((LANES, FILTER_HIDDEN)),
                  full((1, FILTER_HIDDEN)), full((FILTER_HIDDEN, FILTER_HIDDEN)), full((1, FILTER_HIDDEN)),
                  full((FILTER_HIDDEN, n_h)), full((1, FILTER_HIDDEN)),
                  full((2 * seq_len, 2 * seq_len))],
        out_specs=[full((HYENA_ORDER, seq_len, GROUP_W)), full((HYENA_ORDER, seq_len, GROUP_W))],
        out_shape=[jax.ShapeDtypeStruct((HYENA_ORDER, seq_len, GROUP_W), F32)] * 2,
        compiler_params=pltpu.CompilerParams(vmem_limit_bytes=VMEM_LIMIT),
        name="hyena_filter",
    )(jnp.asarray(feats), jnp.asarray(env), w1p, b1[None, :], w2, b2[None, :], w3, freq[None, :], tables[0])


def _hyena_kernel(p_ref, conv_ref, hr_ref, hs_ref, db_ref, f_ref, i_ref, y_ref, pad_s, *, seq_len):
    us = [_short_conv(pad_s, p_ref[b], conv_ref, HYENA_CONV, seq_len) for b in range(HYENA_SEQS)]
    part = lambda k: jnp.concatenate([u[:, k * GROUP_W:(k + 1) * GROUP_W] for u in us], axis=1)
    wide = lambda t: jnp.concatenate([t] * HYENA_SEQS, axis=1)
    first_row = _iota((seq_len, HYENA_SEQS * GROUP_W), 0) == 0

    def longconv(z, o):
        zf = _dft_apply(f_ref, z)
        zr = zf[0:seq_len]
        zs = zf[seq_len:2 * seq_len]
        hr = wide(hr_ref[o])
        hs = wide(hs_ref[o])
        yr = zr * hr - jnp.where(first_row, 0.0, zs * hs)
        ys = jnp.where(first_row, zs * hs, zr * hs + zs * hr)
        y = _dft_apply(i_ref, jnp.concatenate([yr, ys], axis=0))
        return y + wide(db_ref[o:o + 1, :]) * z

    y = part(2) * longconv(part(1) * longconv(part(0), 0), 1)
    for b in range(HYENA_SEQS):
        y_ref[b] = y[:, b * GROUP_W:(b + 1) * GROUP_W].astype(BF16)


def _hyena(slab, conv_w, hr, hs, dbias, tables, seq_len):
    bsz = slab.shape[0]
    once = lambda shape: pl.BlockSpec(shape, lambda b: (0,) * len(shape), pipeline_mode=pl.Buffered(1))
    in_bytes = HYENA_SEQS * seq_len * W_SLAB_D * 4
    return pl.pallas_call(
        functools.partial(_hyena_kernel, seq_len=seq_len),
        grid=(bsz // HYENA_SEQS,),
        in_specs=[pl.BlockSpec((HYENA_SEQS, seq_len, W_SLAB_D), lambda b: (b, 0, 0),
                               pipeline_mode=pl.Buffered(2 if in_bytes <= HYENA_PREFETCH_MAX_BYTES else 1)),
                  _full_spec((HYENA_CONV, 3 * GROUP_W)),
                  once((HYENA_ORDER, seq_len, GROUP_W)), once((HYENA_ORDER, seq_len, GROUP_W)),
                  _full_spec((HYENA_ORDER, GROUP_W)),
                  once((2 * seq_len, 2 * seq_len)), once((seq_len, 4 * seq_len))],
        out_specs=pl.BlockSpec((HYENA_SEQS, seq_len, GROUP_W), lambda b: (b, 0, 0)),
        out_shape=jax.ShapeDtypeStruct((bsz, seq_len, GROUP_W), BF16),
        scratch_shapes=[pltpu.VMEM((seq_len + 2 * CONV_PAD, 3 * GROUP_W), F32)],
        compiler_params=pltpu.CompilerParams(dimension_semantics=("parallel",), vmem_limit_bytes=VMEM_LIMIT),
        name="hyena",
    )(slab, conv_w, hr, hs, dbias, *tables)


def _outproj_kernel(x_ref, ya_ref, yb_ref, yc_ref, yd_ref, wo_ref, mod_ref, g_ref, rwh_ref, rwl_ref, rb_ref,
                    x1_ref, h2_ref, gates_ref):
    rows_per_part = x_ref.shape[0] // OUTPROJ_PARTS
    parts = [slice(i * rows_per_part, (i + 1) * rows_per_part) for i in range(OUTPROJ_PARTS)]
    y = [(_dot(ya_ref[p, :], wo_ref[0:GROUP_W, :]) + _dot(yb_ref[p, :], wo_ref[GROUP_W:2 * GROUP_W, :])
          + _dot(yc_ref[p, :], wo_ref[2 * GROUP_W:3 * GROUP_W, :])
          + _dot(yd_ref[p, :], wo_ref[3 * GROUP_W:4 * GROUP_W, :])) for p in parts]
    x1 = [x_ref[p, :] + mod_ref[0, 2:3, :] * yi for p, yi in zip(parts, y)]
    for p, v in zip(parts, x1):
        x1_ref[p, :] = v
    split = [_split2(_modulated_norm(v, g_ref[...], mod_ref[0, 3:4, :], mod_ref[0, 4:5, :])) for v in x1]
    for p, (hh, _) in zip(parts, split):
        h2_ref[p, :] = hh
    logits = [_dot(hh, rwh_ref[...]) + _dot(hl, rwh_ref[...]) + _dot(hh, rwl_ref[...]) for hh, hl in split]
    scores = [_sigmoid(v) for v in logits]
    vals = [s + rb_ref[...] for s in scores]
    lane = _iota(vals[0].shape, 1).astype(F32)
    picked = [jnp.zeros(v.shape, F32) for v in vals]
    for _ in range(TOP_K):
        best = [jnp.max(v, axis=-1, keepdims=True) for v in vals]
        first = [jnp.min(jnp.where(v == b, lane, float(LANES)), axis=-1, keepdims=True) for v, b in zip(vals, best)]
        hit = [lane == f for f in first]
        picked = [jnp.where(h, s, pk) for h, s, pk in zip(hit, scores, picked)]
        vals = [jnp.where(h, -jnp.inf, v) for h, v in zip(hit, vals)]
    for p, pk in zip(parts, picked):
        gates_ref[p, :] = pk / jnp.sum(pk, axis=-1, keepdims=True) * ROUTED_SCALE


def _outproj(x, ys, w_out, mod, g, router_w, router_bias, seq_len, tm):
    t = x.shape[0]
    per_seq = seq_len // tm if mod.shape[0] > 1 else None

    def mod_idx(i):
        return (i // per_seq if per_seq else 0, 0, 0)

    rw = jnp.zeros((D_MODEL, LANES), F32).at[:, 0:N_EXPERTS].set(router_w)
    rwh = rw.astype(BF16)
    rwl = (rw - rwh.astype(F32)).astype(BF16)
    rb = jnp.full((1, LANES), -jnp.inf, F32).at[0, 0:N_EXPERTS].set(router_bias.astype(F32))
    row = lambda w: pl.BlockSpec((tm, w), lambda i: (i, 0))
    fix = lambda shape: pl.BlockSpec(shape, lambda i: (0,) * len(shape))
    return pl.pallas_call(
        _outproj_kernel,
        grid=(t // tm,),
        in_specs=[row(D_MODEL), row(GROUP_W), row(GROUP_W), row(GROUP_W), row(GROUP_W),
                  fix((D_MODEL, D_MODEL)), pl.BlockSpec((1, 6, D_MODEL), mod_idx), fix((1, D_MODEL)),
                  fix((D_MODEL, LANES)), fix((D_MODEL, LANES)), fix((1, LANES))],
        out_specs=[row(D_MODEL), row(D_MODEL), row(LANES)],
        out_shape=[jax.ShapeDtypeStruct((t, D_MODEL), F32), jax.ShapeDtypeStruct((t, D_MODEL), BF16),
                   jax.ShapeDtypeStruct((t, LANES), F32)],
        compiler_params=pltpu.CompilerParams(dimension_semantics=("parallel",), vmem_limit_bytes=VMEM_LIMIT),
        name="outproj",
    )(x, *ys, w_out.astype(BF16), mod, g, rwh, rwl, rb)


def _moe_kernel(h_ref, gates_ref, x1_ref, mod_ref, wg_ref, wu_ref, wd_ref, sg_ref, su_ref, sd_ref, fg_ref,
                o_ref, acc_ref, *, final_norm):
    j = pl.program_id(1)
    h = h_ref[...]
    width = EXPERTS_PER_STEP * D_EXPERT

    @pl.when(j == 0)
    def _():
        sh = _silu(_dot(h, sg_ref[...].astype(BF16))) * _dot(h, su_ref[...].astype(BF16))
        acc_ref[...] = _dot(sh.astype(BF16), sd_ref[...].astype(BF16))

    w_gu = jnp.concatenate([wg_ref[e] for e in range(EXPERTS_PER_STEP)]
                           + [wu_ref[e] for e in range(EXPERTS_PER_STEP)], axis=1).astype(BF16)
    gu = _dot(h, w_gu)
    mine = pltpu.roll(gates_ref[...], (LANES - EXPERTS_PER_STEP * j) % LANES, axis=1)
    gate_w = jnp.concatenate([jnp.broadcast_to(mine[:, e:e + 1], (mine.shape[0], D_EXPERT))
                              for e in range(EXPERTS_PER_STEP)], axis=1)
    hid = (_silu(gu[:, 0:width]) * gu[:, width:2 * width]) * gate_w
    w_dn = jnp.concatenate([wd_ref[e] for e in range(EXPERTS_PER_STEP)], axis=0).astype(BF16)
    acc_ref[...] += _dot(hid.astype(BF16), w_dn)

    @pl.when(j == pl.num_programs(1) - 1)
    def _():
        out = x1_ref[...] + mod_ref[0, 5:6, :] * acc_ref[...]
        if final_norm:
            ms = jnp.mean(out * out, axis=-1, keepdims=True)
            out = out * lax.rsqrt(ms + EPS) * fg_ref[...]
        o_ref[...] = out


def _moe(h2, gates, x1, mod, layer, w_gate, w_up, w_down, sh_gate, sh_up, sh_down, final_g, seq_len, tm,
         final_norm):
    t = h2.shape[0]
    per_seq = max(seq_len // tm, 1) if mod.shape[0] > 1 else None
    seqs_per_tile = max(tm // seq_len, 1)
    assert mod.shape[0] == 1 or seqs_per_tile == 1

    def mod_idx(i, j):
        return (i // per_seq if per_seq else 0, 0, 0)

    row = lambda w: pl.BlockSpec((tm, w), lambda i, j: (i, 0))
    fix = lambda shape: pl.BlockSpec(shape, lambda i, j: (0,) * len(shape))
    eps_ = EXPERTS_PER_STEP
    return pl.pallas_call(
        functools.partial(_moe_kernel, final_norm=final_norm),
        grid=(t // tm, N_EXPERTS // eps_),
        in_specs=[row(D_MODEL), row(LANES), row(D_MODEL), pl.BlockSpec((1, 6, D_MODEL), mod_idx),
                  pl.BlockSpec((None, eps_, D_MODEL, D_EXPERT), lambda i, j: (layer, j, 0, 0)),
                  pl.BlockSpec((None, eps_, D_MODEL, D_EXPERT), lambda i, j: (layer, j, 0, 0)),
                  pl.BlockSpec((None, eps_, D_EXPERT, D_MODEL), lambda i, j: (layer, j, 0, 0)),
                  fix((D_MODEL, D_EXPERT)), fix((D_MODEL, D_EXPERT)), fix((D_EXPERT, D_MODEL)),
                  fix((1, D_MODEL))],
        out_specs=row(D_MODEL),
        out_shape=jax.ShapeDtypeStruct((t, D_MODEL), F32),
        scratch_shapes=[pltpu.VMEM((tm, D_MODEL), F32)],
        compiler_params=pltpu.CompilerParams(dimension_semantics=("parallel", "arbitrary"),
                                             vmem_limit_bytes=VMEM_LIMIT),
        name="moe",
    )(h2, gates, x1, mod, w_gate, w_up, w_down, sh_gate, sh_up, sh_down, final_g)


def _in_weight_slabs(w_in):
    ga = 4 * GROUP_W
    gb = ga + 4 * N_HEADS
    hb = gb + 5 * GROUP_W
    mc = hb + 4 * GROUP_W
    md = mc + 4 * N_HEADS
    pad = jnp.zeros((D_MODEL, LANES - 4 * N_HEADS), w_in.dtype)
    wa = jnp.concatenate([w_in[:, 0:gb], pad], axis=1)
    wb = w_in[:, gb:hb]
    wc = jnp.concatenate([w_in[:, hb:md], pad], axis=1)
    wd = w_in[:, md:]
    return [w.astype(BF16) for w in (wa, wb, wc, wd)]


def _layer(x, mod, states, prm, stacked, layer, lb, seq_len, grid_rows, tables, final_g, final_norm):
    bsz = x.shape[0] // seq_len
    tm = TOKEN_TILE
    s_gdn, s_hgrn, s_c, s_n, s_m = states
    pa, pb, pc, pd = _inproj(x, mod, prm['norm1_g'][None, :], _in_weight_slabs(prm['w_in']), seq_len, tm)
    pa = pa.reshape(bsz, seq_len, W_SLAB_A)
    pb = pb.reshape(bsz, seq_len, W_SLAB_B)
    pc = pc.reshape(bsz, seq_len, W_SLAB_C)
    pd = pd.reshape(bsz, seq_len, W_SLAB_D)

    ya, st_a = _gdn(pa, prm['gdn_conv'], prm['gdn_a_log'], prm['gdn_dt_bias'], prm['gdn_norm_g'],
                    s_gdn.reshape(bsz, 2, GROUP_W, HEAD_DIM), seq_len)
    yb, st_b = _hgrn(pb, lb, prm['hgrn_norm_g'], s_hgrn.reshape(bsz, 2, GROUP_W, HEAD_DIM), seq_len, grid_rows)
    yc, st_c, st_n, st_m = _mlstm(pc, prm['mlstm_f_bias'], prm['mlstm_norm_g'],
                                  s_c.reshape(bsz, 2, GROUP_W, HEAD_DIM), s_n.reshape(bsz, 2, 1, GROUP_W),
                                  jnp.repeat(s_m, HEAD_DIM, axis=-1).reshape(bsz, 2, 1, GROUP_W), seq_len)
    hr, hs = _hyena_filters(seq_len, prm['filt_w1'], prm['filt_b1'], prm['filt_w2'], prm['filt_b2'],
                            prm['filt_w3'], prm['filt_freq'], tables)
    yd = _hyena(pd, prm['hyena_conv'], hr, hs, prm['hyena_d'], tables, seq_len)

    t = x.shape[0]
    ys = [y.reshape(t, GROUP_W) for y in (ya, yb, yc, yd)]
    x1, h2, gates = _outproj(x, ys, prm['w_out'], mod, prm['norm2_g'][None, :], prm['router_w'],
                             prm['router_bias'], seq_len, tm)
    tm_moe = MOE_TOKEN_TILE
    out = _moe(h2, gates, x1, mod, layer, stacked['exp_w_gate'], stacked['exp_w_up'], stacked['exp_w_down'],
               prm['sh_w_gate'], prm['sh_w_up'], prm['sh_w_down'], final_g[None, :], seq_len, tm_moe, final_norm)
    new_states = (st_a.reshape(bsz, 2, N_HEADS, HEAD_DIM, HEAD_DIM),
                  st_b.reshape(bsz, 2, N_HEADS, HEAD_DIM, HEAD_DIM),
                  st_c.reshape(bsz, 2, N_HEADS, HEAD_DIM, HEAD_DIM),
                  st_n.reshape(bsz, 2, N_HEADS, HEAD_DIM),
                  st_m.reshape(bsz, 2, N_HEADS, HEAD_DIM)[..., 0])
    return out, new_states


def kernel(x_prompt, x_sample, c, state_gdn, state_hgrn, state_mlstm_c, state_mlstm_n, state_mlstm_m, c_ctx, norm1_g, norm2_g, w_mod, b_mod, w_in, gdn_conv, gdn_a_log, gdn_dt_bias, gdn_norm_g, hgrn_lb_logits, hgrn_norm_g, mlstm_f_bias, mlstm_norm_g, hyena_conv, filt_w1, filt_b1, filt_w2, filt_b2, filt_w3, filt_freq, hyena_d, w_out, router_w, router_bias, exp_w_gate, exp_w_up, exp_w_down, sh_w_gate, sh_w_up, sh_w_down, final_g):
    depth = w_in.shape[0]
    bsz, seq, _ = x_prompt.shape
    dbsz, dseq, _ = x_sample.shape
    p = jax.nn.softmax(hgrn_lb_logits.astype(F32), axis=0)
    lower = jnp.cumsum(p, axis=0) - p[0:1]

    per_layer = dict(norm1_g=norm1_g, norm2_g=norm2_g, w_in=w_in, gdn_conv=gdn_conv, gdn_a_log=gdn_a_log,
                     gdn_dt_bias=gdn_dt_bias, gdn_norm_g=gdn_norm_g, hgrn_norm_g=hgrn_norm_g,
                     mlstm_f_bias=mlstm_f_bias, mlstm_norm_g=mlstm_norm_g, hyena_conv=hyena_conv,
                     filt_w1=filt_w1, filt_b1=filt_b1, filt_w2=filt_w2, filt_b2=filt_b2, filt_w3=filt_w3,
                     filt_freq=filt_freq, hyena_d=hyena_d, w_out=w_out, router_w=router_w,
                     router_bias=router_bias, sh_w_gate=sh_w_gate, sh_w_up=sh_w_up, sh_w_down=sh_w_down)
    stacked = dict(exp_w_gate=exp_w_gate, exp_w_up=exp_w_up, exp_w_down=exp_w_down)

    cond = jnp.zeros((N_COND_ROWS, D_MODEL), F32).at[0].set(c_ctx).at[1:1 + dbsz].set(c)
    tables_p = _dft_tables(seq)
    tables_s = _dft_tables(dseq)

    zero_states = (jnp.zeros((bsz, 2, N_HEADS, HEAD_DIM, HEAD_DIM), F32),
                   jnp.zeros((bsz, 2, N_HEADS, HEAD_DIM, HEAD_DIM), F32),
                   jnp.zeros((bsz, 2, N_HEADS, HEAD_DIM, HEAD_DIM), F32),
                   jnp.zeros((bsz, 2, N_HEADS, HEAD_DIM), F32),
                   jnp.zeros((bsz, 2, N_HEADS), F32))
    h = x_prompt.reshape(bsz * seq, D_MODEL)
    z = x_sample.reshape(dbsz * dseq, D_MODEL)
    new_states = [[] for _ in range(5)]
    for l in range(depth):
        prm = {name: val[l] for name, val in per_layer.items()}
        mod = _adaln(cond, w_mod, b_mod[l], l).reshape(N_COND_ROWS, 6, D_MODEL)
        last = l == depth - 1
        h, st = _layer(h, mod[0:1], zero_states, prm, stacked, l, lower[l], seq, None, tables_p, final_g, last)
        for acc, s in zip(new_states, st):
            acc.append(s.astype(x_prompt.dtype))
        cached = (state_gdn[:, l].astype(F32), state_hgrn[:, l].astype(F32), state_mlstm_c[:, l].astype(F32),
                  state_mlstm_n[:, l].astype(F32), state_mlstm_m[:, l].astype(F32))
        z, _ = _layer(z, mod[1:1 + dbsz], cached, prm, stacked, l, lower[l], dseq, dseq // GRID_W, tables_s,
                      final_g, last)

    return (h.reshape(bsz, seq, D_MODEL), z.reshape(dbsz, dseq, D_MODEL),
            *[jnp.stack(acc, axis=1) for acc in new_states])
```

```python
import functools
import math

import numpy as np
import jax
import jax.numpy as jnp
from jax import lax
from jax.experimental import pallas as pl
from jax.experimental.pallas import tpu as pltpu

F32 = jnp.float32
BF16 = jnp.bfloat16

D_MODEL = 1024
N_HEADS = 4
HEAD_DIM = 64
GROUP_W = N_HEADS * HEAD_DIM
GRID_W = 64
CHUNK = 64
HGRN_CHUNK = 16
MLSTM_SEQS = 2
HGRN_GROUP = 2
OUTPROJ_PARTS = 2
HYENA_PREFETCH_MAX_BYTES = 4 * 1024 * 1024
HYENA_SEQS = 2
GDN_GROUP = 4
TOKEN_TILE = 512
MOE_TOKEN_TILE = 1024
ADALN_COL_TILE = 1536
GDN_CONV = 5
HYENA_CONV = 3
HYENA_ORDER = 2
POS_BANDS = 16
POS_DIM = 1 + 2 * POS_BANDS
FILTER_HIDDEN = 64
N_EXPERTS = 64
TOP_K = 8
D_EXPERT = 128
ROUTED_SCALE = 2.5
EPS = 1e-6
LOG2_E = 1.4426950408889634
LANES = 128
CONV_PAD = 8
EXPERTS_PER_STEP = 4
N_COND_ROWS = 16
VMEM_LIMIT = 56 * 1024 * 1024

W_SLAB_A = 4 * GROUP_W + LANES
W_SLAB_B = 5 * GROUP_W
W_SLAB_C = 4 * GROUP_W + LANES
W_SLAB_D = 3 * GROUP_W


def _split2(x):
    hi = x.astype(BF16)
    lo = (x - hi.astype(F32)).astype(BF16)
    return hi, lo


def _split3(x):
    p1 = x.astype(BF16)
    r = x - p1.astype(F32)
    p2 = r.astype(BF16)
    p3 = (r - p2.astype(F32)).astype(BF16)
    return p1, p2, p3


def _dot(a, b):
    return jnp.dot(a, b, preferred_element_type=F32)


def _dot_nt(a, b):
    return lax.dot_general(a, b, (((1,), (1,)), ((), ())), preferred_element_type=F32)


def _dot_tn(a, b):
    return lax.dot_general(a, b, (((0,), (0,)), ((), ())), preferred_element_type=F32)


def _dot3(a, b):
    ah, al = _split2(a)
    bh, bl = _split2(b)
    return _dot(ah, bh) + _dot(al, bh) + _dot(ah, bl)


def _dot_sel(x, sel):
    p1, p2, p3 = _split3(x)
    return _dot(p1, sel) + _dot(p2, sel) + _dot(p3, sel)


def _sel_dot(sel, x):
    p1, p2, p3 = _split3(x)
    return _dot(sel, p1) + _dot(sel, p2) + _dot(sel, p3)


def _iota(shape, dim):
    return lax.broadcasted_iota(jnp.int32, shape, dim)


def _head_ones():
    r = _iota((GROUP_W, GROUP_W), 0) // HEAD_DIM
    c = _iota((GROUP_W, GROUP_W), 1) // HEAD_DIM
    return r == c


def _seg_sum(x, ones_bd):
    hi, lo = _split2(x)
    return _dot(hi, ones_bd) + _dot(lo, ones_bd)


def _sigmoid(x):
    return 1.0 / (1.0 + jnp.exp(-x))


def _silu(x):
    return x * _sigmoid(x)


def _softplus(x):
    return jnp.maximum(x, 0.0) + jnp.log(1.0 + jnp.exp(-jnp.abs(x)))


def _seg_cumsum(x, chunk, reverse):
    n = x.shape[0]
    pos = _iota(x.shape, 0) % chunk
    s = 1
    while s < chunk:
        if reverse:
            x = x + jnp.where(pos < chunk - s, pltpu.roll(x, n - s, axis=0), 0.0)
        else:
            x = x + jnp.where(pos >= s, pltpu.roll(x, s, axis=0), 0.0)
        s *= 2
    return x


def _swap_row_groups(x, a, b):
    n, ch = x.shape
    return jnp.swapaxes(x.reshape(a, b, ch), 0, 1).reshape(n, ch)


def _dir_cumsum(x, chunk, lane):
    backward = (lane // N_HEADS) % 2 == 1
    return jnp.where(backward, _seg_cumsum(x, chunk, True), _seg_cumsum(x, chunk, False))


def _short_conv(pad_ref, x, w_ref, width, n):
    ch = x.shape[1]
    pad_ref[0:CONV_PAD, :] = jnp.zeros((CONV_PAD, ch), F32)
    pad_ref[CONV_PAD + n:2 * CONV_PAD + n, :] = jnp.zeros((CONV_PAD, ch), F32)
    pad_ref[CONV_PAD:CONV_PAD + n, :] = x
    acc = None
    for j in range(width):
        off = CONV_PAD - width // 2 + j
        term = pad_ref[off:off + n, :] * w_ref[j:j + 1, :]
        acc = term if acc is None else acc + term
    return acc


def _chunk_masks(reverse):
    r = _iota((CHUNK, GROUP_W), 0)
    j = _iota((CHUNK, GROUP_W), 1) % HEAD_DIM
    if reverse:
        return j >= r, j > r, j == r
    return j <= r, j < r, j == r


def _adaln_kernel(c_ref, w_ref, b_ref, o_ref):
    c = c_ref[...]
    o_ref[...] = _dot3(_silu(c), w_ref[...]) + b_ref[...]


def _adaln(cond, w_mod, b_mod, layer):
    n_out = w_mod.shape[2]
    tn = ADALN_COL_TILE
    return pl.pallas_call(
        _adaln_kernel,
        grid=(n_out // tn,),
        in_specs=[pl.BlockSpec((N_COND_ROWS, D_MODEL), lambda j: (0, 0)),
                  pl.BlockSpec((None, D_MODEL, tn), lambda j: (layer, 0, j)),
                  pl.BlockSpec((1, tn), lambda j: (0, j))],
        out_specs=pl.BlockSpec((N_COND_ROWS, tn), lambda j: (0, j)),
        out_shape=jax.ShapeDtypeStruct((N_COND_ROWS, n_out), F32),
        compiler_params=pltpu.CompilerParams(dimension_semantics=("arbitrary",), vmem_limit_bytes=VMEM_LIMIT),
        name="adaln",
    )(cond, w_mod, b_mod.reshape(1, n_out))


def _modulated_norm(x, g, shift, scale):
    ms = jnp.mean(x * x, axis=-1, keepdims=True)
    return (x * lax.rsqrt(ms + EPS) * g) * (1.0 + scale) + shift


def _inproj_kernel(x_ref, mod_ref, g_ref, wa_ref, wb_ref, wc_ref, wd_ref, oa_ref, ob_ref, oc_ref, od_ref):
    h = _modulated_norm(x_ref[...], g_ref[...], mod_ref[0, 0:1, :], mod_ref[0, 1:2, :]).astype(BF16)
    oa_ref[...] = _dot(h, wa_ref[...])
    ob_ref[...] = _dot(h, wb_ref[...])
    oc_ref[...] = _dot(h, wc_ref[...])
    od_ref[...] = _dot(h, wd_ref[...])


def _inproj(x, mod, g, w_slabs, seq_len, tm):
    t = x.shape[0]
    per_seq = seq_len // tm if mod.shape[0] > 1 else None

    def mod_idx(i):
        return (i // per_seq if per_seq else 0, 0, 0)

    widths = (W_SLAB_A, W_SLAB_B, W_SLAB_C, W_SLAB_D)
    return pl.pallas_call(
        _inproj_kernel,
        grid=(t // tm,),
        in_specs=[pl.BlockSpec((tm, D_MODEL), lambda i: (i, 0)),
                  pl.BlockSpec((1, 6, D_MODEL), mod_idx),
                  pl.BlockSpec((1, D_MODEL), lambda i: (0, 0))]
                 + [pl.BlockSpec((D_MODEL, w), lambda i: (0, 0)) for w in widths],
        out_specs=[pl.BlockSpec((tm, w), lambda i: (i, 0)) for w in widths],
        out_shape=[jax.ShapeDtypeStruct((t, w), F32) for w in widths],
        compiler_params=pltpu.CompilerParams(dimension_semantics=("parallel",), vmem_limit_bytes=VMEM_LIMIT),
        name="inproj",
    )(x, mod, g, *w_slabs)


def _chunk_masks_f32(reverse):
    return tuple(jnp.where(m, 1.0, 0.0) for m in _chunk_masks(reverse))


def _blockdiag_mul(x, bd_factor):
    return jnp.concatenate([x] * N_HEADS, axis=0) * bd_factor


def _tri_inverse(lms, eyes, same16, same32, bd_b, fillers):
    pending = list(fillers)

    def prod(xs, ys):
        if pending:
            pending.pop(0)()
        staged = []
        for a, b in zip(xs, ys):
            ah, al = _split2(a)
            staged.append((jnp.concatenate([ah, al], axis=0), _blockdiag_mul(b.astype(BF16), bd_b)))
        outs = []
        for a_hl, b_bd in staged:
            n = a_hl.shape[0] // 2
            hl = _dot(a_hl, b_bd)
            outs.append(hl[0:n] + hl[n:2 * n])
        return outs

    ms = [-(lm * same16) for lm in lms]
    ps = [eye + m for eye, m in zip(eyes, ms)]
    ms = prod(ms, ms)
    for _ in range(2):
        pms = prod([jnp.concatenate([p, m], axis=0) for p, m in zip(ps, ms)], ms)
        ps = [p + pm[0:CHUNK] for p, pm in zip(ps, pms)]
        ms = [pm[CHUNK:2 * CHUNK] for pm in pms]
    ps = [p + pm for p, pm in zip(ps, prod(ps, ms))]
    for keep in (same32 - same16, 1.0 - same32):
        offs = [lm * keep for lm in lms]
        ps = [p - c for p, c in zip(ps, prod(ps, prod(offs, ps)))]
    for rest in pending:
        rest()
    return ps


def _gdn_kernel(p_ref, conv_ref, alog_ref, dtb_ref, ng_ref, ea_ref, t4_ref, t4t_ref, s0_ref,
                y_ref, sout_ref,
                q_s, k_s, v_s, gc_s, beta_s, u_s, o_s, w_s, qe_s, attn_s, kdec_s, st_s, pad_s, *, seq_len):
    n_chunks = seq_len // CHUNK
    bd_f = jnp.where(_head_ones(), 1.0, 0.0)
    bd_b = bd_f.astype(BF16)
    ones_bd = bd_b

    qkv = _silu(_short_conv(pad_s, p_ref[0, :, 0:3 * GROUP_W], conv_ref, GDN_CONV, seq_len))
    q = qkv[:, 0:GROUP_W]
    k = qkv[:, GROUP_W:2 * GROUP_W]
    q_s[...] = q * lax.rsqrt(_seg_sum(q * q, ones_bd) + EPS) * (HEAD_DIM ** -0.5)
    k_s[...] = k * lax.rsqrt(_seg_sum(k * k, ones_bd) + EPS)
    v_s[...] = qkv[:, 2 * GROUP_W:3 * GROUP_W]

    raw = p_ref[0, :, 4 * GROUP_W:4 * GROUP_W + LANES]
    lane = _iota(raw.shape, 1)
    log_decay = -jnp.exp(alog_ref[...]) * _softplus(raw + dtb_ref[...])
    compact = jnp.where(lane < 2 * N_HEADS, _dir_cumsum(log_decay, CHUNK, lane),
                        jnp.where(lane < N_GATE_LANES, _sigmoid(raw), 0.0))
    gcb = _expand_gates(compact, ea_ref[...])
    for d in range(2):
        gc_s[d] = gcb[:, d * GROUP_W:(d + 1) * GROUP_W]
        beta_s[d] = gcb[:, (2 + d) * GROUP_W:(3 + d) * GROUP_W]
        st_s[d] = bd_f * _dot_sel(s0_ref[0, d], t4_ref[...])

    masks = [_chunk_masks_f32(False), _chunk_masks_f32(True)]
    r = _iota((CHUNK, GROUP_W), 0)
    j = _iota((CHUNK, GROUP_W), 1) % HEAD_DIM
    same16 = jnp.where((r // 16) == (j // 16), 1.0, 0.0)
    same32 = jnp.where((r // 32) == (j // 32), 1.0, 0.0)
    ones_cc = jnp.ones((CHUNK, CHUNK), BF16)

    n_groups = n_chunks // GDN_GROUP

    def scan_step(n):
        r0 = [pl.multiple_of(n * CHUNK, CHUNK), pl.multiple_of((n_chunks - 1 - n) * CHUNK, CHUNK)]
        rows = [pl.ds(r, CHUNK) for r in r0]
        last = [CHUNK - 1, 0]
        s_bd = [st_s[d] for d in range(2)]
        ws_qs = [_dot(jnp.concatenate([w_s[d, rows[d], :], qe_s[d, rows[d], :]], axis=0), s_bd[d].astype(BF16))
                 for d in range(2)]
        v_new = [(u_s[d, rows[d], :] - ws_qs[d][0:CHUNK]).astype(BF16) for d in range(2)]
        intra = [_dot(attn_s[d, rows[d], :], _blockdiag_mul(v_new[d], bd_b)) for d in range(2)]
        outer = [_dot_tn(kdec_s[d, rows[d], :], v_new[d]) for d in range(2)]
        for d in range(2):
            o_s[d, rows[d], :] = ws_qs[d][CHUNK:2 * CHUNK] + intra[d]
            g_last = jnp.exp(gc_s[d, pl.ds(r0[d] + last[d], 1), :])
            st_s[d] = s_bd[d] * g_last + bd_f * outer[d]

    def prepare_group(t, fillers):
        ds = [d for _ in range(GDN_GROUP) for d in range(2)]
        rows = [pl.ds(pl.multiple_of((GDN_GROUP * t + c if d == 0 else n_chunks - GDN_GROUP * (t + 1) + c) * CHUNK,
                                     CHUNK), CHUNK) for c in range(GDN_GROUP) for d in range(2)]
        q_c = [q_s[r, :] for r in rows]
        k_c = [k_s[r, :] for r in rows]
        gc = [gc_s[d, r, :] for d, r in zip(ds, rows)]
        beta = [beta_s[d, r, :] for d, r in zip(ds, rows)]
        gc_row = [_sel_dot(ones_cc, g * masks[d][2]) for d, g in zip(ds, gc)]
        decay = [masks[d][0] * jnp.exp(jnp.minimum(g - gr, 0.0)) for d, g, gr in zip(ds, gc, gc_row)]
        kb = [k * b for k, b in zip(k_c, beta)]
        kq = [_dot_nt(jnp.concatenate([kbi, qi], axis=0).astype(BF16), _blockdiag_mul(ki.astype(BF16), bd_b))
              for kbi, qi, ki in zip(kb, q_c, k_c)]
        tmat = _tri_inverse([kqi[0:CHUNK] * (dec * masks[d][1]) for d, kqi, dec in zip(ds, kq, decay)],
                            [masks[d][2] for d in ds], same16, same32, bd_b, fillers)
        egc = [jnp.exp(g) for g in gc]
        rhs = [jnp.concatenate([_blockdiag_mul((v_s[r, :] * b).astype(BF16), bd_b),
                                _blockdiag_mul((kbi * e).astype(BF16), bd_b)], axis=1)
               for r, b, kbi, e in zip(rows, beta, kb, egc)]
        uw = [_dot(t.astype(BF16), x) for t, x in zip(tmat, rhs)]
        for i, (d, r) in enumerate(zip(ds, rows)):
            last = 0 if d == 1 else CHUNK - 1
            u_s[d, r, :] = uw[i][:, 0:GROUP_W]
            w_s[d, r, :] = uw[i][:, GROUP_W:2 * GROUP_W].astype(BF16)
            qe_s[d, r, :] = (q_c[i] * egc[i]).astype(BF16)
            attn_s[d, r, :] = (kq[i][CHUNK:2 * CHUNK] * decay[i]).astype(BF16)
            kdec_s[d, r, :] = (k_c[i] * jnp.exp(gc[i][last:last + 1, :] - gc[i])).astype(BF16)

    prepare_group(0, [])

    def pipelined(t, carry):
        prepare_group(t, [functools.partial(scan_step, GDN_GROUP * (t - 1) + k) for k in range(GDN_GROUP)])
        return carry

    lax.fori_loop(1, n_groups, pipelined, 0)
    for k in range(GDN_GROUP):
        scan_step(GDN_GROUP * (n_groups - 1) + k)

    o = o_s[0] + o_s[1]
    o = o * lax.rsqrt(_seg_sum(o * o, ones_bd) * (1.0 / HEAD_DIM) + EPS) * ng_ref[...]
    y_ref[0] = (o * _silu(p_ref[0, :, 3 * GROUP_W:4 * GROUP_W])).astype(BF16)
    for d in range(2):
        sout_ref[0, d] = _dot_sel(st_s[d], t4t_ref[...])


def _compact_row(m, kind):
    row = jnp.zeros((1, LANES), F32)
    return row.at[0, kind * 2 * N_HEADS:(kind + 1) * 2 * N_HEADS].set(m.astype(F32).reshape(2 * N_HEADS))


N_GATE_LANES = 4 * N_HEADS


def _gate_expander():
    sel = np.zeros((LANES, 4 * GROUP_W), np.float32)
    for piece in range(3):
        for kind in range(2):
            for d in range(2):
                for h in range(N_HEADS):
                    c0 = (kind * 2 + d) * GROUP_W + h * HEAD_DIM
                    sel[piece * N_GATE_LANES + kind * 2 * N_HEADS + d * N_HEADS + h, c0:c0 + HEAD_DIM] = 1.0
    return jnp.asarray(sel, BF16)


def _expand_gates(g, sel):
    p1 = g.astype(BF16).astype(F32)
    r1 = g - p1
    p2 = r1.astype(BF16).astype(F32)
    packed = p1 + pltpu.roll(p2, N_GATE_LANES, axis=1) + pltpu.roll(r1 - p2, 2 * N_GATE_LANES, axis=1)
    return _dot(packed.astype(BF16), sel)


def _tile4():
    t = np.concatenate([np.eye(HEAD_DIM, dtype=np.float32)] * N_HEADS, axis=1)
    return jnp.asarray(t, BF16), jnp.asarray(t.T, BF16)


def _seq_spec(seq_len, width):
    return pl.BlockSpec((1, seq_len, width), lambda b: (b, 0, 0))


def _full_spec(shape):
    nd = len(shape)
    return pl.BlockSpec(shape, lambda b: (0,) * nd)


def _state_spec():
    return pl.BlockSpec((1, 2, GROUP_W, HEAD_DIM), lambda b: (b, 0, 0, 0))


def _gdn(slab, conv_w, a_log, dt_bias, norm_g, s0, seq_len):
    assert seq_len % (CHUNK * GDN_GROUP) == 0
    bsz = slab.shape[0]
    t4, t4t = _tile4()
    return pl.pallas_call(
        functools.partial(_gdn_kernel, seq_len=seq_len),
        grid=(bsz,),
        in_specs=[_seq_spec(seq_len, W_SLAB_A), _full_spec((GDN_CONV, 3 * GROUP_W)), _full_spec((1, LANES)),
                  _full_spec((1, LANES)), _full_spec((1, GROUP_W)), _full_spec((LANES, 4 * GROUP_W)),
                  _full_spec((HEAD_DIM, GROUP_W)), _full_spec((GROUP_W, HEAD_DIM)), _state_spec()],
        out_specs=[_seq_spec(seq_len, GROUP_W), _state_spec()],
        out_shape=[jax.ShapeDtypeStruct((bsz, seq_len, GROUP_W), BF16),
                   jax.ShapeDtypeStruct((bsz, 2, GROUP_W, HEAD_DIM), F32)],
        scratch_shapes=[pltpu.VMEM((seq_len, GROUP_W), F32)] * 3
                       + [pltpu.VMEM((2, seq_len, GROUP_W), F32)] * 4
                       + [pltpu.VMEM((2, seq_len, GROUP_W), BF16)] * 4
                       + [pltpu.VMEM((2, GROUP_W, GROUP_W), F32),
                          pltpu.VMEM((seq_len + 2 * CONV_PAD, 3 * GROUP_W), F32)],
        compiler_params=pltpu.CompilerParams(dimension_semantics=("parallel",), vmem_limit_bytes=VMEM_LIMIT),
        name="gdn",
    )(slab, conv_w, _compact_row(a_log, 0), _compact_row(dt_bias, 0), jnp.tile(norm_g, N_HEADS)[None, :],
      _gate_expander(), t4, t4t, s0)


def _mlstm_kernel(p_ref, fb_ref, ng_ref, eg_ref, t4_ref, t4t_ref, c0_ref, n0_ref, m0_ref,
                  y_ref, cout_ref, nout_ref, mout_ref,
                  k_s, bc_s, ig_s, h_s, c_s, n_s, m_s, *, seq_len):
    n_chunks = seq_len // CHUNK
    bd_mask = _head_ones()
    bd_f = jnp.where(bd_mask, 1.0, 0.0)
    bd_b = bd_f.astype(BF16)
    ones_bd = bd_b
    lane_head = _iota((CHUNK, GROUP_W), 1) // HEAD_DIM
    chains = [(s, d) for s in range(MLSTM_SEQS) for d in range(2)]

    for s in range(MLSTM_SEQS):
        k_s[s] = p_ref[s, :, GROUP_W:2 * GROUP_W] * (HEAD_DIM ** -0.5)
        raw = p_ref[s, :, 4 * GROUP_W:4 * GROUP_W + LANES]
        lane = _iota(raw.shape, 1)
        logf = -_softplus(-(raw + fb_ref[...]))
        compact = jnp.where(lane < 2 * N_HEADS, raw,
                            jnp.where(lane < N_GATE_LANES, _dir_cumsum(logf, CHUNK, lane), 0.0))
        gates = _expand_gates(compact, eg_ref[...])
        for d in range(2):
            ig_s[2 * s + d] = gates[:, d * GROUP_W:(d + 1) * GROUP_W]
            bc_s[2 * s + d] = gates[:, (2 + d) * GROUP_W:(3 + d) * GROUP_W]
            c_s[2 * s + d] = jnp.where(bd_mask, _dot_sel(c0_ref[s, d], t4_ref[...]), 0.0)
            n_s[2 * s + d] = n0_ref[s, d]
            m_s[2 * s + d] = m0_ref[s, d]

    masks = [_chunk_masks(False), _chunk_masks(True)]
    ones_cc = jnp.ones((CHUNK, CHUNK), BF16)

    def seg_max(dm):
        dmax = jnp.zeros((CHUNK, GROUP_W), F32)
        for h in range(N_HEADS):
            in_head = lane_head == h
            mh = jnp.max(jnp.where(in_head, dm, -jnp.inf), axis=1, keepdims=True)
            dmax = jnp.where(in_head, mh, dmax)
        return dmax

    def body(n, carry):
        last = [CHUNK - 1 if d == 0 else 0 for _, d in chains]
        fwd_rows = pl.ds(pl.multiple_of(n * CHUNK, CHUNK), CHUNK)
        bwd_rows = pl.ds(pl.multiple_of((n_chunks - 1 - n) * CHUNK, CHUNK), CHUNK)
        rows = [bwd_rows if d == 1 else fwd_rows for _, d in chains]
        ids = range(len(chains))
        q_c = [p_ref[s, rows[i], 0:GROUP_W] for i, (s, _) in enumerate(chains)]
        k_c = [k_s[s, rows[i], :] for i, (s, _) in enumerate(chains)]
        v_b = [p_ref[s, rows[i], 2 * GROUP_W:3 * GROUP_W].astype(BF16) for i, (s, _) in enumerate(chains)]
        bc = [bc_s[i, rows[i], :] for i in ids]
        x = [ig_s[i, rows[i], :] - bc[i] for i in ids]
        x_row = [_sel_dot(ones_cc, jnp.where(masks[d][2], x[i], 0.0))
                 for i, (_, d) in enumerate(chains)]
        dm = [jnp.where(masks[d][0], bc[i] + x_row[i], -jnp.inf) for i, (_, d) in enumerate(chains)]
        dmax = [seg_max(v) for v in dm]
        a = [bc[i] + m_s[i] for i in ids]
        m_t = [jnp.maximum(a[i], dmax[i]) for i in ids]
        inter = [jnp.exp(a[i] - m_t[i]) for i in ids]
        q_b = [q.astype(BF16) for q in q_c]
        qk = [_dot_nt(q_b[i], _blockdiag_mul(k_c[i].astype(BF16), bd_b)) for i in ids]
        w = [(jnp.exp(dm[i] - m_t[i]) * qk[i]).astype(BF16) for i in ids]
        c_bd = [c_s[i] for i in ids]
        n_row = [n_s[i] for i in ids]
        q_cm = [_dot(q_b[i], c_bd[i].astype(BF16)) for i in ids]
        q_n = [_seg_sum(q_c[i] * n_row[i], ones_bd) for i in ids]
        wv = [_dot(w[i], jnp.concatenate([_blockdiag_mul(v_b[i], bd_b), ones_bd], axis=1)) for i in ids]
        m_new = [m_t[i][last[i]:last[i] + 1, :] for i in ids]
        carry_w = [jnp.exp(a[i][last[i]:last[i] + 1, :] - m_new[i]) for i in ids]
        kw = [k_c[i] * jnp.exp(bc[i][last[i]:last[i] + 1, :] + x[i] - m_new[i]) for i in ids]
        outer = [_dot_tn(kw[i].astype(BF16), v_b[i]) for i in ids]
        for i in ids:
            num = inter[i] * q_cm[i] + wv[i][:, 0:GROUP_W]
            den = inter[i] * q_n[i] + wv[i][:, GROUP_W:2 * GROUP_W]
            h_s[i, rows[i], :] = num / jnp.maximum(jnp.abs(den), jnp.exp(-m_t[i]))
            c_s[i] = carry_w[i] * c_bd[i] + bd_f * outer[i]
            n_s[i] = carry_w[i] * n_row[i] + jnp.sum(kw[i], axis=0, keepdims=True)
            m_s[i] = m_new[i]
        return carry

    lax.fori_loop(0, n_chunks, body, 0)

    for s in range(MLSTM_SEQS):
        h = h_s[2 * s] + h_s[2 * s + 1]
        h = h * lax.rsqrt(_seg_sum(h * h, ones_bd) * (1.0 / HEAD_DIM) + EPS) * ng_ref[...]
        y_ref[s] = (h * _sigmoid(p_ref[s, :, 3 * GROUP_W:4 * GROUP_W])).astype(BF16)
        for d in range(2):
            cout_ref[s, d] = _dot_sel(c_s[2 * s + d], t4t_ref[...])
            nout_ref[s, d] = n_s[2 * s + d]
            mout_ref[s, d] = m_s[2 * s + d]


def _mlstm(slab, f_bias, norm_g, c0, n0, m0, seq_len):
    bsz = slab.shape[0]
    ns = MLSTM_SEQS
    assert bsz % ns == 0
    seqs = lambda *tail: pl.BlockSpec((ns,) + tail, lambda b: (b,) + (0,) * len(tail))
    t4, t4t = _tile4()
    return pl.pallas_call(
        functools.partial(_mlstm_kernel, seq_len=seq_len),
        grid=(bsz // ns,),
        in_specs=[seqs(seq_len, W_SLAB_C), _full_spec((1, LANES)), _full_spec((1, GROUP_W)),
                  _full_spec((LANES, 4 * GROUP_W)), _full_spec((HEAD_DIM, GROUP_W)),
                  _full_spec((GROUP_W, HEAD_DIM)), seqs(2, GROUP_W, HEAD_DIM), seqs(2, 1, GROUP_W),
                  seqs(2, 1, GROUP_W)],
        out_specs=[seqs(seq_len, GROUP_W), seqs(2, GROUP_W, HEAD_DIM), seqs(2, 1, GROUP_W), seqs(2, 1, GROUP_W)],
        out_shape=[jax.ShapeDtypeStruct((bsz, seq_len, GROUP_W), BF16),
                   jax.ShapeDtypeStruct((bsz, 2, GROUP_W, HEAD_DIM), F32),
                   jax.ShapeDtypeStruct((bsz, 2, 1, GROUP_W), F32),
                   jax.ShapeDtypeStruct((bsz, 2, 1, GROUP_W), F32)],
        scratch_shapes=[pltpu.VMEM((ns, seq_len, GROUP_W), F32), pltpu.VMEM((2 * ns, seq_len, GROUP_W), F32),
                        pltpu.VMEM((2 * ns, seq_len, GROUP_W), F32), pltpu.VMEM((2 * ns, seq_len, GROUP_W), F32),
                        pltpu.VMEM((2 * ns, GROUP_W, GROUP_W), F32), pltpu.VMEM((2 * ns, 1, GROUP_W), F32),
                        pltpu.VMEM((2 * ns, 1, GROUP_W), F32)],
        compiler_params=pltpu.CompilerParams(dimension_semantics=("parallel",), vmem_limit_bytes=VMEM_LIMIT),
        name="mlstm",
    )(slab, _compact_row(f_bias, 1), jnp.tile(norm_g, N_HEADS)[None, :], _gate_expander(), t4, t4t, c0, n0, m0)


def _hgrn_kernel(p_ref, lb_ref, ng_ref, t4_ref, t4t_ref, s0_ref, y_ref, sout_ref,
                 k_s, gc_s, o_s, st_s, pairs_s, *scan_order_s, seq_len, grid_rows):
    c = HGRN_CHUNK
    n_chunks = seq_len // c
    if grid_rows is None:
        src = p_ref.at[0]
    else:
        (src,) = scan_order_s
        src[...] = _swap_row_groups(p_ref[0], grid_rows, GRID_W)
    bd_mask = _head_ones()
    bd_f = jnp.where(bd_mask, 1.0, 0.0)
    ones_bd = bd_f.astype(BF16)
    sub = _iota((c, GROUP_W), 0)
    row_sum = jnp.where(_iota((c, c * c), 1) // c == _iota((c, c * c), 0), 1.0, 0.0).astype(BF16)

    for d in range(2):
        lb = lb_ref[d:d + 1, :]
        forget = lb + (1.0 - lb) * _sigmoid(src[:, (3 + d) * GROUP_W:(4 + d) * GROUP_W])
        k_s[d] = 1.0 - forget
        gc_s[d] = _seg_cumsum(jnp.log(forget), c, reverse=(d == 1))
        s_bd = jnp.where(bd_mask, _dot_sel(s0_ref[0, d], t4_ref[...]), 0.0)
        st_s[d] = s_bd.T

    def pair_weights(d, q_c, k_c, gc):
        gc2 = gc * LOG2_E
        key_side = gc2 - jnp.log(jnp.maximum(k_c, 0.0)) * LOG2_E
        pieces = []
        for i in range(c):
            keep = (sub >= i) if d == 1 else (sub <= i)
            pieces.append(jnp.where(keep, jnp.exp2(gc2[i:i + 1, :] - key_side), 0.0) * q_c[i:i + 1, :])
        return jnp.concatenate(pieces, axis=0).astype(BF16)

    n_trips = n_chunks // HGRN_GROUP
    ds = [d for _ in range(HGRN_GROUP) for d in range(2)]

    def trip_rows(n):
        starts = [HGRN_GROUP * n + g if d == 0 else n_chunks - 1 - HGRN_GROUP * n - g
                  for g in range(HGRN_GROUP) for d in range(2)]
        return [pl.ds(pl.multiple_of(s * c, c), c) for s in starts]

    def build_pairs(n, slot):
        for i, (d, r) in enumerate(zip(ds, trip_rows(n))):
            pairs_s[slot, i] = pair_weights(d, src[r, 0:GROUP_W], k_s[d, r, :], gc_s[d, r, :])

    def trip(n, slot):
        rows = trip_rows(n)
        q_c = [src[r, 0:GROUP_W] for r in rows]
        v_c = [src[r, GROUP_W:2 * GROUP_W] for r in rows]
        k_c = [k_s[d, r, :] for d, r in zip(ds, rows)]
        gc = [gc_s[d, r, :] for d, r in zip(ds, rows)]
        attn = [_dot(pairs_s[slot, i], ones_bd) for i in range(len(ds))]
        build_pairs(jnp.minimum(n + 1, n_trips - 1), 1 - slot)
        weighted = [(a * jnp.concatenate([v] * c, axis=0)).astype(BF16) for a, v in zip(attn, v_c)]
        intra = [_dot(row_sum, w) for w in weighted]
        q_dec = [(q * jnp.exp(g)).astype(BF16) for q, g in zip(q_c, gc)]
        gc_last = [g[(0 if d == 1 else c - 1):(1 if d == 1 else c), :] for d, g in zip(ds, gc)]
        outer = [_dot_tn(v.astype(BF16), (k * jnp.exp(gl - g)).astype(BF16))
                 for v, k, gl, g in zip(v_c, k_c, gc_last, gc)]
        for i, d in enumerate(ds):
            st = st_s[d]
            o_s[d, rows[i], :] = intra[i] + _dot_nt(q_dec[i], st.astype(BF16))
            st_s[d] = st * jnp.exp(gc_last[i]) + bd_f * outer[i]

    build_pairs(0, 0)

    def body(m, carry):
        trip(2 * m, 0)
        trip(2 * m + 1, 1)
        return carry

    lax.fori_loop(0, n_trips // 2, body, 0)

    o = o_s[0] + o_s[1]
    o = o * lax.rsqrt(_seg_sum(o * o, ones_bd) * (1.0 / HEAD_DIM) + EPS) * ng_ref[...]
    y = o * _sigmoid(src[:, 2 * GROUP_W:3 * GROUP_W])
    if grid_rows is None:
        y_ref[0] = y.astype(BF16)
    else:
        y_ref[0] = _swap_row_groups(y, GRID_W, grid_rows).astype(BF16)
    for d in range(2):
        sout_ref[0, d] = _dot_sel(st_s[d].T, t4t_ref[...])


def _hgrn(slab, lb, norm_g, s0, seq_len, grid_rows):
    bsz = slab.shape[0]
    t4, t4t = _tile4()
    scan_order = [] if grid_rows is None else [pltpu.VMEM((seq_len, W_SLAB_B), F32)]
    return pl.pallas_call(
        functools.partial(_hgrn_kernel, seq_len=seq_len, grid_rows=grid_rows),
        grid=(bsz,),
        in_specs=[_seq_spec(seq_len, W_SLAB_B), _full_spec((2, GROUP_W)), _full_spec((1, GROUP_W)),
                  _full_spec((HEAD_DIM, GROUP_W)), _full_spec((GROUP_W, HEAD_DIM)), _state_spec()],
        out_specs=[_seq_spec(seq_len, GROUP_W), _state_spec()],
        out_shape=[jax.ShapeDtypeStruct((bsz, seq_len, GROUP_W), BF16),
                   jax.ShapeDtypeStruct((bsz, 2, GROUP_W, HEAD_DIM), F32)],
        scratch_shapes=[pltpu.VMEM((2, seq_len, GROUP_W), F32), pltpu.VMEM((2, seq_len, GROUP_W), F32),
                        pltpu.VMEM((2, seq_len, GROUP_W), F32), pltpu.VMEM((2, GROUP_W, GROUP_W), F32),
                        pltpu.VMEM((2, 2 * HGRN_GROUP, HGRN_CHUNK * HGRN_CHUNK, GROUP_W), BF16)] + scan_order,
        compiler_params=pltpu.CompilerParams(dimension_semantics=("parallel",), vmem_limit_bytes=VMEM_LIMIT),
        name="hgrn",
    )(slab, lb.astype(F32), jnp.tile(norm_g, N_HEADS)[None, :], t4, t4t, s0)


def _dft_tables(n):
    k = np.arange(n, dtype=np.float64)[:, None]
    s = np.arange(n, dtype=np.float64)[None, :]
    ang = np.pi * k * s / n
    cos_m = np.cos(ang)
    sin_m = np.sin(ang)
    alt = (-1.0) ** np.arange(n)
    sin_m[0, :] = alt
    fwd = np.concatenate([cos_m, sin_m], axis=0)
    wgt = np.full((n,), 2.0 / (2 * n))
    wgt[0] = 1.0 / (2 * n)
    inv = np.concatenate([cos_m.T * wgt[None, :], sin_m.T * wgt[None, :]], axis=1)
    out = []
    for m in (fwd, inv):
        m32 = m.astype(np.float32)
        hi = m32.astype(BF16)
        lo = (m32 - hi.astype(np.float32)).astype(BF16)
        out.append(jnp.asarray(np.concatenate([hi, lo], axis=1)))
    return out


def _dft_apply(m_ref, x):
    xb = x.astype(BF16)
    return _dot(m_ref[...], jnp.concatenate([xb, xb], axis=0))


def _dft_apply_f32(m_ref, x):
    k = x.shape[0]
    xh, xl = _split2(x)
    return _dot(m_ref[...], jnp.concatenate([xh, xh], axis=0)) + _dot(m_ref[:, 0:k], xl)


def _filter_kernel(feat_ref, env_ref, w1_ref, b1_ref, w2_ref, b2_ref, w3_ref, freq_ref, f_ref,
                   hr_ref, hs_ref, *, seq_len):
    freq = freq_ref[...]
    hid = jnp.sin(freq * (_dot3(feat_ref[...], w1_ref[...]) + b1_ref[...]))
    hid = jnp.sin(freq * (_dot3(hid, w2_ref[...]) + b2_ref[...]))
    h = _dot3(hid, w3_ref[...])
    env = env_ref[...]
    not_first = _iota((seq_len, GROUP_W), 0) > 0
    first_row = _iota((seq_len, GROUP_W), 0) == 0
    sides = []
    for o in range(HYENA_ORDER):
        pos = h[:, (2 * o) * GROUP_W:(2 * o + 1) * GROUP_W] * env
        neg = jnp.where(not_first, h[:, (2 * o + 1) * GROUP_W:(2 * o + 2) * GROUP_W] * env, 0.0)
        ss = jnp.sum(pos * pos + neg * neg, axis=0, keepdims=True)
        scale = lax.rsqrt(ss + EPS)
        sides += [(pos + neg) * scale, (pos - neg) * scale]
    spec = _dft_apply_f32(f_ref, jnp.concatenate(sides, axis=1))
    for o in range(HYENA_ORDER):
        fa = spec[:, (2 * o) * GROUP_W:(2 * o + 1) * GROUP_W]
        fb = spec[:, (2 * o + 1) * GROUP_W:(2 * o + 2) * GROUP_W]
        hr_ref[o] = fa[0:seq_len]
        hs_ref[o] = jnp.where(first_row, fa[seq_len:2 * seq_len], fb[seq_len:2 * seq_len])


def _hyena_filters(seq_len, w1, b1, w2, b2, w3, freq, tables):
    t = np.arange(seq_len, dtype=np.float64)
    tn = np.linspace(0.0, 1.0, seq_len)
    bands = np.linspace(1e-4, POS_BANDS - 1, POS_BANDS)
    ang = (2.0 * math.pi / seq_len) * t[:, None] * bands[None, :]
    feats = np.zeros((seq_len, LANES), np.float32)
    feats[:, 0:POS_DIM] = np.concatenate([tn[:, None], np.cos(ang), -np.sin(ang)], axis=-1)
    deltas = np.abs(np.linspace(math.log(1e-2) / 1.5, math.log(1e-2) / 0.3, GROUP_W))
    env = np.exp(-tn[:, None] * deltas[None, :]).astype(np.float32)
    w1p = jnp.zeros((LANES, FILTER_HIDDEN), F32).at[0:POS_DIM].set(w1)
    n_h = HYENA_ORDER * 2 * GROUP_W
    full = lambda shape: pl.BlockSpec(shape, lambda: (0,) * len(shape))
    return pl.pallas_call(
        functools.partial(_filter_kernel, seq_len=seq_len),
        in_specs=[full((seq_len, LANES)), full((seq_len, GROUP_W)), full((LANES, FILTER_HIDDEN)),
                  full((1, FILTER_HIDDEN)), full((FILTER_HIDDEN, FILTER_HIDDEN)), full((1, FILTER_HIDDEN)),
                  full((FILTER_HIDDEN, n_h)), full((1, FILTER_HIDDEN)),
                  full((2 * seq_len, 2 * seq_len))],
        out_specs=[full((HYENA_ORDER, seq_len, GROUP_W)), full((HYENA_ORDER, seq_len, GROUP_W))],
        out_shape=[jax.ShapeDtypeStruct((HYENA_ORDER, seq_len, GROUP_W), F32)] * 2,
        compiler_params=pltpu.CompilerParams(vmem_limit_bytes=VMEM_LIMIT),
        name="hyena_filter",
    )(jnp.asarray(feats), jnp.asarray(env), w1p, b1[None, :], w2, b2[None, :], w3, freq[None, :], tables[0])


def _hyena_kernel(p_ref, conv_ref, hr_ref, hs_ref, db_ref, f_ref, i_ref, y_ref, pad_s, *, seq_len):
    us = [_short_conv(pad_s, p_ref[b], conv_ref, HYENA_CONV, seq_len) for b in range(HYENA_SEQS)]
    part = lambda k: jnp.concatenate([u[:, k * GROUP_W:(k + 1) * GROUP_W] for u in us], axis=1)
    wide = lambda t: jnp.concatenate([t] * HYENA_SEQS, axis=1)
    first_row = _iota((seq_len, HYENA_SEQS * GROUP_W), 0) == 0

    def longconv(z, o):
        zf = _dft_apply(f_ref, z)
        zr = zf[0:seq_len]
        zs = zf[seq_len:2 * seq_len]
        hr = wide(hr_ref[o])
        hs = wide(hs_ref[o])
        yr = zr * hr - jnp.where(first_row, 0.0, zs * hs)
        ys = jnp.where(first_row, zs * hs, zr * hs + zs * hr)
        y = _dft_apply(i_ref, jnp.concatenate([yr, ys], axis=0))
        return y + wide(db_ref[o:o + 1, :]) * z

    y = part(2) * longconv(part(1) * longconv(part(0), 0), 1)
    for b in range(HYENA_SEQS):
        y_ref[b] = y[:, b * GROUP_W:(b + 1) * GROUP_W].astype(BF16)


def _hyena(slab, conv_w, hr, hs, dbias, tables, seq_len):
    bsz = slab.shape[0]
    once = lambda shape: pl.BlockSpec(shape, lambda b: (0,) * len(shape), pipeline_mode=pl.Buffered(1))
    in_bytes = HYENA_SEQS * seq_len * W_SLAB_D * 4
    return pl.pallas_call(
        functools.partial(_hyena_kernel, seq_len=seq_len),
        grid=(bsz // HYENA_SEQS,),
        in_specs=[pl.BlockSpec((HYENA_SEQS, seq_len, W_SLAB_D), lambda b: (b, 0, 0),
                               pipeline_mode=pl.Buffered(2 if in_bytes <= HYENA_PREFETCH_MAX_BYTES else 1)),
                  _full_spec((HYENA_CONV, 3 * GROUP_W)),
                  once((HYENA_ORDER, seq_len, GROUP_W)), once((HYENA_ORDER, seq_len, GROUP_W)),
                  _full_spec((HYENA_ORDER, GROUP_W)),
                  once((2 * seq_len, 2 * seq_len)), once((seq_len, 4 * seq_len))],
        out_specs=pl.BlockSpec((HYENA_SEQS, seq_len, GROUP_W), lambda b: (b, 0, 0)),
        out_shape=jax.ShapeDtypeStruct((bsz, seq_len, GROUP_W), BF16),
        scratch_shapes=[pltpu.VMEM((seq_len + 2 * CONV_PAD, 3 * GROUP_W), F32)],
        compiler_params=pltpu.CompilerParams(dimension_semantics=("parallel",), vmem_limit_bytes=VMEM_LIMIT),
        name="hyena",
    )(slab, conv_w, hr, hs, dbias, *tables)


def _outproj_kernel(x_ref, ya_ref, yb_ref, yc_ref, yd_ref, wo_ref, mod_ref, g_ref, rwh_ref, rwl_ref, rb_ref,
                    x1_ref, h2_ref, gates_ref):
    rows_per_part = x_ref.shape[0] // OUTPROJ_PARTS
    parts = [slice(i * rows_per_part, (i + 1) * rows_per_part) for i in range(OUTPROJ_PARTS)]
    y = [(_dot(ya_ref[p, :], wo_ref[0:GROUP_W, :]) + _dot(yb_ref[p, :], wo_ref[GROUP_W:2 * GROUP_W, :])
          + _dot(yc_ref[p, :], wo_ref[2 * GROUP_W:3 * GROUP_W, :])
          + _dot(yd_ref[p, :], wo_ref[3 * GROUP_W:4 * GROUP_W, :])) for p in parts]
    x1 = [x_ref[p, :] + mod_ref[0, 2:3, :] * yi for p, yi in zip(parts, y)]
    for p, v in zip(parts, x1):
        x1_ref[p, :] = v
    split = [_split2(_modulated_norm(v, g_ref[...], mod_ref[0, 3:4, :], mod_ref[0, 4:5, :])) for v in x1]
    for p, (hh, _) in zip(parts, split):
        h2_ref[p, :] = hh
    logits = [_dot(hh, rwh_ref[...]) + _dot(hl, rwh_ref[...]) + _dot(hh, rwl_ref[...]) for hh, hl in split]
    scores = [_sigmoid(v) for v in logits]
    vals = [s + rb_ref[...] for s in scores]
    lane = _iota(vals[0].shape, 1).astype(F32)
    picked = [jnp.zeros(v.shape, F32) for v in vals]
    for _ in range(TOP_K):
        best = [jnp.max(v, axis=-1, keepdims=True) for v in vals]
        first = [jnp.min(jnp.where(v == b, lane, float(LANES)), axis=-1, keepdims=True) for v, b in zip(vals, best)]
        hit = [lane == f for f in first]
        picked = [jnp.where(h, s, pk) for h, s, pk in zip(hit, scores, picked)]
        vals = [jnp.where(h, -jnp.inf, v) for h, v in zip(hit, vals)]
    for p, pk in zip(parts, picked):
        gates_ref[p, :] = pk / jnp.sum(pk, axis=-1, keepdims=True) * ROUTED_SCALE


def _outproj(x, ys, w_out, mod, g, router_w, router_bias, seq_len, tm):
    t = x.shape[0]
    per_seq = seq_len // tm if mod.shape[0] > 1 else None

    def mod_idx(i):
        return (i // per_seq if per_seq else 0, 0, 0)

    rw = jnp.zeros((D_MODEL, LANES), F32).at[:, 0:N_EXPERTS].set(router_w)
    rwh = rw.astype(BF16)
    rwl = (rw - rwh.astype(F32)).astype(BF16)
    rb = jnp.full((1, LANES), -jnp.inf, F32).at[0, 0:N_EXPERTS].set(router_bias.astype(F32))
    row = lambda w: pl.BlockSpec((tm, w), lambda i: (i, 0))
    fix = lambda shape: pl.BlockSpec(shape, lambda i: (0,) * len(shape))
    return pl.pallas_call(
        _outproj_kernel,
        grid=(t // tm,),
        in_specs=[row(D_MODEL), row(GROUP_W), row(GROUP_W), row(GROUP_W), row(GROUP_W),
                  fix((D_MODEL, D_MODEL)), pl.BlockSpec((1, 6, D_MODEL), mod_idx), fix((1, D_MODEL)),
                  fix((D_MODEL, LANES)), fix((D_MODEL, LANES)), fix((1, LANES))],
        out_specs=[row(D_MODEL), row(D_MODEL), row(LANES)],
        out_shape=[jax.ShapeDtypeStruct((t, D_MODEL), F32), jax.ShapeDtypeStruct((t, D_MODEL), BF16),
                   jax.ShapeDtypeStruct((t, LANES), F32)],
        compiler_params=pltpu.CompilerParams(dimension_semantics=("parallel",), vmem_limit_bytes=VMEM_LIMIT),
        name="outproj",
    )(x, *ys, w_out.astype(BF16), mod, g, rwh, rwl, rb)


def _moe_kernel(h_ref, gates_ref, x1_ref, mod_ref, wg_ref, wu_ref, wd_ref, sg_ref, su_ref, sd_ref, fg_ref,
                o_ref, acc_ref, *, final_norm):
    j = pl.program_id(1)
    h = h_ref[...]
    width = EXPERTS_PER_STEP * D_EXPERT

    @pl.when(j == 0)
    def _():
        sh = _silu(_dot(h, sg_ref[...].astype(BF16))) * _dot(h, su_ref[...].astype(BF16))
        acc_ref[...] = _dot(sh.astype(BF16), sd_ref[...].astype(BF16))

    w_gu = jnp.concatenate([wg_ref[e] for e in range(EXPERTS_PER_STEP)]
                           + [wu_ref[e] for e in range(EXPERTS_PER_STEP)], axis=1).astype(BF16)
    gu = _dot(h, w_gu)
    mine = pltpu.roll(gates_ref[...], (LANES - EXPERTS_PER_STEP * j) % LANES, axis=1)
    gate_w = jnp.concatenate([jnp.broadcast_to(mine[:, e:e + 1], (mine.shape[0], D_EXPERT))
                              for e in range(EXPERTS_PER_STEP)], axis=1)
    hid = (_silu(gu[:, 0:width]) * gu[:, width:2 * width]) * gate_w
    w_dn = jnp.concatenate([wd_ref[e] for e in range(EXPERTS_PER_STEP)], axis=0).astype(BF16)
    acc_ref[...] += _dot(hid.astype(BF16), w_dn)

    @pl.when(j == pl.num_programs(1) - 1)
    def _():
        out = x1_ref[...] + mod_ref[0, 5:6, :] * acc_ref[...]
        if final_norm:
            ms = jnp.mean(out * out, axis=-1, keepdims=True)
            out = out * lax.rsqrt(ms + EPS) * fg_ref[...]
        o_ref[...] = out


def _moe(h2, gates, x1, mod, layer, w_gate, w_up, w_down, sh_gate, sh_up, sh_down, final_g, seq_len, tm,
         final_norm):
    t = h2.shape[0]
    per_seq = max(seq_len // tm, 1) if mod.shape[0] > 1 else None
    seqs_per_tile = max(tm // seq_len, 1)
    assert mod.shape[0] == 1 or seqs_per_tile == 1

    def mod_idx(i, j):
        return (i // per_seq if per_seq else 0, 0, 0)

    row = lambda w: pl.BlockSpec((tm, w), lambda i, j: (i, 0))
    fix = lambda shape: pl.BlockSpec(shape, lambda i, j: (0,) * len(shape))
    eps_ = EXPERTS_PER_STEP
    return pl.pallas_call(
        functools.partial(_moe_kernel, final_norm=final_norm),
        grid=(t // tm, N_EXPERTS // eps_),
        in_specs=[row(D_MODEL), row(LANES), row(D_MODEL), pl.BlockSpec((1, 6, D_MODEL), mod_idx),
                  pl.BlockSpec((None, eps_, D_MODEL, D_EXPERT), lambda i, j: (layer, j, 0, 0)),
                  pl.BlockSpec((None, eps_, D_MODEL, D_EXPERT), lambda i, j: (layer, j, 0, 0)),
                  pl.BlockSpec((None, eps_, D_EXPERT, D_MODEL), lambda i, j: (layer, j, 0, 0)),
                  fix((D_MODEL, D_EXPERT)), fix((D_MODEL, D_EXPERT)), fix((D_EXPERT, D_MODEL)),
                  fix((1, D_MODEL))],
        out_specs=row(D_MODEL),
        out_shape=jax.ShapeDtypeStruct((t, D_MODEL), F32),
        scratch_shapes=[pltpu.VMEM((tm, D_MODEL), F32)],
        compiler_params=pltpu.CompilerParams(dimension_semantics=("parallel", "arbitrary"),
                                             vmem_limit_bytes=VMEM_LIMIT),
        name="moe",
    )(h2, gates, x1, mod, w_gate, w_up, w_down, sh_gate, sh_up, sh_down, final_g)


def _in_weight_slabs(w_in):
    ga = 4 * GROUP_W
    gb = ga + 4 * N_HEADS
    hb = gb + 5 * GROUP_W
    mc = hb + 4 * GROUP_W
    md = mc + 4 * N_HEADS
    pad = jnp.zeros((D_MODEL, LANES - 4 * N_HEADS), w_in.dtype)
    wa = jnp.concatenate([w_in[:, 0:gb], pad], axis=1)
    wb = w_in[:, gb:hb]
    wc = jnp.concatenate([w_in[:, hb:md], pad], axis=1)
    wd = w_in[:, md:]
    return [w.astype(BF16) for w in (wa, wb, wc, wd)]


def _layer(x, mod, states, prm, stacked, layer, lb, seq_len, grid_rows, tables, final_g, final_norm):
    bsz = x.shape[0] // seq_len
    tm = TOKEN_TILE
    s_gdn, s_hgrn, s_c, s_n, s_m = states
    pa, pb, pc, pd = _inproj(x, mod, prm['norm1_g'][None, :], _in_weight_slabs(prm['w_in']), seq_len, tm)
    pa = pa.reshape(bsz, seq_len, W_SLAB_A)
    pb = pb.reshape(bsz, seq_len, W_SLAB_B)
    pc = pc.reshape(bsz, seq_len, W_SLAB_C)
    pd = pd.reshape(bsz, seq_len, W_SLAB_D)

    ya, st_a = _gdn(pa, prm['gdn_conv'], prm['gdn_a_log'], prm['gdn_dt_bias'], prm['gdn_norm_g'],
                    s_gdn.reshape(bsz, 2, GROUP_W, HEAD_DIM), seq_len)
    yb, st_b = _hgrn(pb, lb, prm['hgrn_norm_g'], s_hgrn.reshape(bsz, 2, GROUP_W, HEAD_DIM), seq_len, grid_rows)
    yc, st_c, st_n, st_m = _mlstm(pc, prm['mlstm_f_bias'], prm['mlstm_norm_g'],
                                  s_c.reshape(bsz, 2, GROUP_W, HEAD_DIM), s_n.reshape(bsz, 2, 1, GROUP_W),
                                  jnp.repeat(s_m, HEAD_DIM, axis=-1).reshape(bsz, 2, 1, GROUP_W), seq_len)
    hr, hs = _hyena_filters(seq_len, prm['filt_w1'], prm['filt_b1'], prm['filt_w2'], prm['filt_b2'],
                            prm['filt_w3'], prm['filt_freq'], tables)
    yd = _hyena(pd, prm['hyena_conv'], hr, hs, prm['hyena_d'], tables, seq_len)

    t = x.shape[0]
    ys = [y.reshape(t, GROUP_W) for y in (ya, yb, yc, yd)]
    x1, h2, gates = _outproj(x, ys, prm['w_out'], mod, prm['norm2_g'][None, :], prm['router_w'],
                             prm['router_bias'], seq_len, tm)
    tm_moe = MOE_TOKEN_TILE
    out = _moe(h2, gates, x1, mod, layer, stacked['exp_w_gate'], stacked['exp_w_up'], stacked['exp_w_down'],
               prm['sh_w_gate'], prm['sh_w_up'], prm['sh_w_down'], final_g[None, :], seq_len, tm_moe, final_norm)
    new_states = (st_a.reshape(bsz, 2, N_HEADS, HEAD_DIM, HEAD_DIM),
                  st_b.reshape(bsz, 2, N_HEADS, HEAD_DIM, HEAD_DIM),
                  st_c.reshape(bsz, 2, N_HEADS, HEAD_DIM, HEAD_DIM),
                  st_n.reshape(bsz, 2, N_HEADS, HEAD_DIM),
                  st_m.reshape(bsz, 2, N_HEADS, HEAD_DIM)[..., 0])
    return out, new_states


def kernel(x_prompt, x_sample, c, state_gdn, state_hgrn, state_mlstm_c, state_mlstm_n, state_mlstm_m, c_ctx, norm1_g, norm2_g, w_mod, b_mod, w_in, gdn_conv, gdn_a_log, gdn_dt_bias, gdn_norm_g, hgrn_lb_logits, hgrn_norm_g, mlstm_f_bias, mlstm_norm_g, hyena_conv, filt_w1, filt_b1, filt_w2, filt_b2, filt_w3, filt_freq, hyena_d, w_out, router_w, router_bias, exp_w_gate, exp_w_up, exp_w_down, sh_w_gate, sh_w_up, sh_w_down, final_g):
    depth = w_in.shape[0]
    bsz, seq, _ = x_prompt.shape
    dbsz, dseq, _ = x_sample.shape
    p = jax.nn.softmax(hgrn_lb_logits.astype(F32), axis=0)
    lower = jnp.cumsum(p, axis=0) - p[0:1]

    per_layer = dict(norm1_g=norm1_g, norm2_g=norm2_g, w_in=w_in, gdn_conv=gdn_conv, gdn_a_log=gdn_a_log,
                     gdn_dt_bias=gdn_dt_bias, gdn_norm_g=gdn_norm_g, hgrn_norm_g=hgrn_norm_g,
                     mlstm_f_bias=mlstm_f_bias, mlstm_norm_g=mlstm_norm_g, hyena_conv=hyena_conv,
                     filt_w1=filt_w1, filt_b1=filt_b1, filt_w2=filt_w2, filt_b2=filt_b2, filt_w3=filt_w3,
                     filt_freq=filt_freq, hyena_d=hyena_d, w_out=w_out, router_w=router_w,
                     router_bias=router_bias, sh_w_gate=sh_w_gate, sh_w_up=sh_w_up, sh_w_down=sh_w_down)
    stacked = dict(exp_w_gate=exp_w_gate, exp_w_up=exp_w_up, exp_w_down=exp_w_down)

    cond = jnp.zeros((N_COND_ROWS, D_MODEL), F32).at[0].set(c_ctx).at[1:1 + dbsz].set(c)
    tables_p = _dft_tables(seq)
    tables_s = _dft_tables(dseq)

    zero_states = (jnp.zeros((bsz, 2, N_HEADS, HEAD_DIM, HEAD_DIM), F32),
                   jnp.zeros((bsz, 2, N_HEADS, HEAD_DIM, HEAD_DIM), F32),
                   jnp.zeros((bsz, 2, N_HEADS, HEAD_DIM, HEAD_DIM), F32),
                   jnp.zeros((bsz, 2, N_HEADS, HEAD_DIM), F32),
                   jnp.zeros((bsz, 2, N_HEADS), F32))
    h = x_prompt.reshape(bsz * seq, D_MODEL)
    z = x_sample.reshape(dbsz * dseq, D_MODEL)
    new_states = [[] for _ in range(5)]
    for l in range(depth):
        prm = {name: val[l] for name, val in per_layer.items()}
        mod = _adaln(cond, w_mod, b_mod[l], l).reshape(N_COND_ROWS, 6, D_MODEL)
        last = l == depth - 1
        h, st = _layer(h, mod[0:1], zero_states, prm, stacked, l, lower[l], seq, None, tables_p, final_g, last)
        for acc, s in zip(new_states, st):
            acc.append(s.astype(x_prompt.dtype))
        cached = (state_gdn[:, l].astype(F32), state_hgrn[:, l].astype(F32), state_mlstm_c[:, l].astype(F32),
                  state_mlstm_n[:, l].astype(F32), state_mlstm_m[:, l].astype(F32))
        z, _ = _layer(z, mod[1:1 + dbsz], cached, prm, stacked, l, lower[l], dseq, dseq // GRID_W, tables_s,
                      final_g, last)

    return (h.reshape(bsz, seq, D_MODEL), z.reshape(dbsz, dseq, D_MODEL),
            *[jnp.stack(acc, axis=1) for acc in new_states])
```
